```python
import math
import numpy as np
import jax
import jax.numpy as jnp
from jax import lax

D_MODEL = 2048
BATCH = 4
SEQ = 2048
DEPTH = 2
DEC_BATCH = 32
DEC_SEQ = 8
PAST_LEN = 8192
PAGE_SIZE = 128

N_EVEN = (DEPTH + 1) // 2
N_ODD = DEPTH // 2
HEAD_DIM = 128
A_HEADS = 8
A_KV = 4
A_HALF = HEAD_DIM // 2
F_HEADS = 8
F_KV = 4
C_HEADS = 16
C_KV = 4
CMP_BLOCK = 64
SEL_BLOCK = 64
TOP_N = 16
WINDOW = 512
N_MEM = 256
X_HEADS = 4
N_GROUPS = 4
EXP_PER_GROUP = 8
N_EXPERTS = N_GROUPS * EXP_PER_GROUP
D_EXPERT = 512
TOP_K_EXPERTS = 2
QBLOCK = 128
TOKEN_CHUNK = 1024
FORGET_BIAS_INIT = 3.0
EPS = 1e-6
NEG = -1e30
FORCE = 1e4
E_SIZES = (A_HEADS * HEAD_DIM, A_KV * HEAD_DIM, A_KV * HEAD_DIM, F_HEADS * HEAD_DIM, F_KV * HEAD_DIM, F_KV * HEAD_DIM, F_HEADS)
E_WIDTH = (A_HEADS + 2 * A_KV + F_HEADS + 2 * F_KV) * HEAD_DIM + F_HEADS
O_SIZES = (C_HEADS * HEAD_DIM,) + (C_KV * HEAD_DIM,) * 6 + (3 * C_HEADS,)
O_WIDTH = (C_HEADS + 6 * C_KV) * HEAD_DIM + 3 * C_HEADS

kernel_name = 'hybrid_diff_fox_nsa_hmoe_step'


def _rmsnorm(x, g):
    xf = x.astype(jnp.float32)
    y = xf * lax.rsqrt(jnp.mean(xf * xf, axis=-1, keepdims=True) + EPS)
    return (y * g.astype(jnp.float32)).astype(x.dtype)


def _split(a, sizes):
    return jnp.split(a, np.cumsum(sizes)[:-1].tolist(), axis=-1)


def _alibi_slopes(n_heads, n_kv):
    m = np.exp2(-8.0 * np.arange(1, n_heads + 1) / n_heads).astype(np.float32)
    return jnp.asarray(m.reshape(n_kv, n_heads // n_kv))


def _lambda_init(layer):
    return 0.8 - 0.6 * math.exp(-0.3 * layer)


def _gqa_probs(q, k, bias, mask):
    s = jnp.einsum('bqhgd,bshd->bhgqs', q, k).astype(jnp.float32) * (q.shape[-1] ** -0.5) + bias
    return jax.nn.softmax(jnp.where(mask, s, NEG), axis=-1)


def _gqa_apply(p, v):
    return jnp.einsum('bhgqs,bshd->bqhgd', p.astype(v.dtype), v)


def _sweep_queries(fn, q_side):
    t = q_side[0].shape[1]
    qb = math.gcd(t, QBLOCK)
    nb = t // qb
    blocks = tuple(jnp.moveaxis(a.reshape(a.shape[0], nb, qb, *a.shape[2:]), 1, 0) for a in q_side)
    outs = lax.map(lambda args: fn(args[0], qb, *args[1]), (jnp.arange(nb), blocks))
    return jax.tree_util.tree_map(lambda o: jnp.moveaxis(o, 0, 1).reshape(o.shape[1], t, *o.shape[3:]), outs)


def _even_mixer(h, w_in, b_f, lam_q1, lam_k1, lam_q2, lam_k2, g_head, w_out, lam_init, past, pos0):
    b, t, _ = h.shape
    ga, gf = A_HEADS // A_KV, F_HEADS // F_KV
    qa, ka, va, qf, kf, vf, fl = _split(h @ w_in, E_SIZES)
    qa = qa.reshape(b, t, A_KV, ga, 2, A_HALF)
    ka = ka.reshape(b, t, A_KV, HEAD_DIM)
    va = va.reshape(b, t, A_KV, HEAD_DIM)
    qf = qf.reshape(b, t, F_KV, gf, HEAD_DIM)
    kf = kf.reshape(b, t, F_KV, HEAD_DIM)
    vf = vf.reshape(b, t, F_KV, HEAD_DIM)
    logf = jax.nn.log_sigmoid(fl.astype(jnp.float32) + b_f.astype(jnp.float32))
    new_rows = (ka, va, kf, vf, logf)
    if past is None:
        ka_all, va_all, kf_all, vf_all, logf_all = new_rows
    else:
        ka_all, va_all, kf_all, vf_all, logf_all = (jnp.concatenate([p.astype(n.dtype), n], axis=1) for p, n in zip(past, new_rows))
    n_keys = ka_all.shape[1]
    kpos = jnp.arange(n_keys)
    k1, k2 = ka_all[..., :A_HALF], ka_all[..., A_HALF:]
    c_all = jnp.cumsum(logf_all, axis=1).reshape(b, n_keys, F_KV, gf)
    c_q = c_all[:, n_keys - t:]
    c_kt = jnp.moveaxis(c_all, 1, -1)[:, :, :, None, :]
    lam = (jnp.exp(jnp.sum(lam_q1.astype(jnp.float32) * lam_k1.astype(jnp.float32)))
           - jnp.exp(jnp.sum(lam_q2.astype(jnp.float32) * lam_k2.astype(jnp.float32))) + lam_init)
    slopes = _alibi_slopes(A_HEADS, A_KV)[:, :, None, None]

    def block(i, qb, q1, q2, qfb, cqb):
        qpos = pos0 + i * qb + jnp.arange(qb)
        causal = kpos[None, :] <= qpos[:, None]
        alibi = -slopes * (qpos[:, None] - kpos[None, :]).astype(jnp.float32)
        p1 = _gqa_probs(q1, k1, alibi, causal)
        p2 = _gqa_probs(q2, k2, alibi, causal)
        o_dif = _gqa_apply(p1 - lam * p2, va_all)
        forget = jnp.moveaxis(cqb, 1, -1)[..., None] - c_kt
        o_fox = _gqa_apply(_gqa_probs(qfb, kf_all, forget, causal), vf_all)
        return o_dif, o_fox

    o_dif, o_fox = _sweep_queries(block, (qa[..., 0, :], qa[..., 1, :], qf, c_q))
    o_dif = _rmsnorm(o_dif, g_head) * (1.0 - lam_init)
    y = jnp.concatenate([o_dif.reshape(b, t, -1), o_fox.reshape(b, t, -1)], axis=-1) @ w_out
    return y, new_rows


def _odd_mixer(h, w_in, pos_k, pos_v, w_out, past, pos0):
    b, t, _ = h.shape
    gc = C_HEADS // C_KV
    parts = _split(h @ w_in, O_SIZES)
    q = parts[0].reshape(b, t, C_KV, gc, HEAD_DIM)
    kc, vc, ks, vs, kw, vw = (p.reshape(b, t, C_KV, HEAD_DIM) for p in parts[1:7])
    gates = jax.nn.sigmoid(parts[7].astype(jnp.float32)).reshape(b, t, C_KV, gc, 3)
    if past is None:
        kc_all, vc_all, ks_all, vs_all = kc, vc, ks, vs
        buf_k, buf_v = kw[:, :0], vw[:, :0]
        n_keep = min(WINDOW, t)
    else:
        pkc, pvc, pks, pvs, buf_k, buf_v = past
        kc_all, vc_all, ks_all, vs_all = (jnp.concatenate([p, n], axis=1) for p, n in zip((pkc, pvc, pks, pvs), (kc, vc, ks, vs)))
        n_keep = buf_k.shape[1]
    hist_k = jnp.concatenate([buf_k, kw], axis=1)
    hist_v = jnp.concatenate([buf_v, vw], axis=1)
    new_rows = (kc, vc, ks, vs, hist_k[:, -n_keep:], hist_v[:, -n_keep:])
    pad_w = ((0, 0), (WINDOW - buf_k.shape[1], 0), (0, 0), (0, 0))
    kw_ext, vw_ext = jnp.pad(hist_k, pad_w), jnp.pad(hist_v, pad_w)

    n_keys = kc_all.shape[1]
    nb = n_keys // CMP_BLOCK
    a_k = jax.nn.softmax(pos_k.astype(jnp.float32), axis=0).astype(kc_all.dtype)
    a_v = jax.nn.softmax(pos_v.astype(jnp.float32), axis=0).astype(vc_all.dtype)
    k_cmp = jnp.einsum('bnjhd,jh->bnhd', kc_all[:, :nb * CMP_BLOCK].reshape(b, nb, CMP_BLOCK, C_KV, HEAD_DIM), a_k)
    v_cmp = jnp.einsum('bnjhd,jh->bnhd', vc_all[:, :nb * CMP_BLOCK].reshape(b, nb, CMP_BLOCK, C_KV, HEAD_DIM), a_v)
    bend = (jnp.arange(nb) + 1) * CMP_BLOCK - 1
    nbs = -(-n_keys // SEL_BLOCK)

    def to_blocks(a):
        a = jnp.pad(a, ((0, 0), (0, nbs * SEL_BLOCK - n_keys), (0, 0), (0, 0)))
        return a.reshape(b, nbs, SEL_BLOCK, C_KV, HEAD_DIM).transpose(0, 3, 1, 2, 4)

    ks_blk, vs_blk = to_blocks(ks_all), to_blocks(vs_all)
    n_top = min(TOP_N, nbs)
    blk = jnp.arange(nbs)
    bi = jnp.arange(b)[:, None, None, None]
    hi = jnp.arange(C_KV)[None, :, None, None]
    slopes = _alibi_slopes(C_HEADS, C_KV)
    scale = HEAD_DIM ** -0.5

    def block(i, qb, qq, gg):
        q0 = pos0 + i * qb
        qpos = q0 + jnp.arange(qb)
        mask_c = bend[None, :] <= qpos[:, None]
        bias_c = -slopes[:, :, None, None] * (qpos[:, None] - bend[None, :]).astype(jnp.float32)
        p_c = jnp.where(mask_c, _gqa_probs(qq, k_cmp, bias_c, mask_c), 0.0)
        o_c = _gqa_apply(p_c, v_cmp)
        imp = jnp.pad(jnp.sum(p_c, axis=2), ((0, 0), (0, 0), (0, 0), (0, nbs - nb)))
        cur = (qpos // SEL_BLOCK)[:, None]
        forced = (blk[None, :] == 0) | (blk[None, :] == cur) | (blk[None, :] == cur - 1)
        valid = blk[None, :] * SEL_BLOCK <= qpos[:, None]
        score = jnp.where(valid, imp + jnp.where(forced, FORCE, 0.0), NEG)
        top_s, top_i = lax.top_k(score, n_top)
        k_sel = ks_blk[bi, hi, top_i]
        v_sel = vs_blk[bi, hi, top_i]
        kpos_s = top_i[..., None] * SEL_BLOCK + jnp.arange(SEL_BLOCK)
        qpos_s = qpos[None, None, :, None, None]
        mask_s = ((top_s > 0.5 * NEG)[..., None] & (kpos_s <= qpos_s))[:, :, None]
        s_s = (jnp.einsum('bqhgd,bhqnjd->bhgqnj', qq, k_sel).astype(jnp.float32) * scale
               - slopes[None, :, :, None, None, None] * (qpos_s - kpos_s).astype(jnp.float32)[:, :, None])
        s_s = jnp.where(mask_s, s_s, NEG).reshape(b, C_KV, gc, qb, n_top * SEL_BLOCK)
        p_s = jax.nn.softmax(s_s, axis=-1).reshape(b, C_KV, gc, qb, n_top, SEL_BLOCK)
        o_s = jnp.einsum('bhgqnj,bhqnjd->bqhgd', p_s.astype(v_sel.dtype), v_sel)
        k_win = lax.dynamic_slice_in_dim(kw_ext, i * qb, WINDOW + qb, axis=1)
        v_win = lax.dynamic_slice_in_dim(vw_ext, i * qb, WINDOW + qb, axis=1)
        kpos_w = q0 - WINDOW + jnp.arange(WINDOW + qb)
        delta = qpos[:, None] - kpos_w[None, :]
        mask_w = (kpos_w[None, :] >= 0) & (delta >= 0) & (delta < WINDOW)
        p_w = _gqa_probs(qq, k_win, -slopes[:, :, None, None] * delta.astype(jnp.float32), mask_w)
        o_w = _gqa_apply(p_w, v_win)
        o = gg[..., 0:1] * o_c + gg[..., 1:2] * o_s + gg[..., 2:3] * o_w
        return o.astype(qq.dtype)

    o = _sweep_queries(block, (q, gates))
    return o.reshape(b, t, C_HEADS * HEAD_DIM) @ w_out, new_rows


def _mem_kv(mem, g, w_k, w_v):
    b, m, _ = mem.shape
    mn = _rmsnorm(mem, g)
    return (mn @ w_k).reshape(b, m, X_HEADS, HEAD_DIM), (mn @ w_v).reshape(b, m, X_HEADS, HEAD_DIM)


def _mem_attend(h, mk, mv, w_q, w_o):
    b, t, _ = h.shape
    q = (h @ w_q).reshape(b, t, X_HEADS, HEAD_DIM)
    s = jnp.einsum('bthd,bmhd->bhtm', q, mk).astype(jnp.float32) * (HEAD_DIM ** -0.5)
    p = jax.nn.softmax(s, axis=-1)
    o = jnp.einsum('bhtm,bmhd->bthd', p.astype(mv.dtype), mv)
    return o.reshape(b, t, X_HEADS * HEAD_DIM) @ w_o


def _hmoe(h, w_grp, b_grp, w_er, b_er, w_gate, w_up, w_down):
    b, t, d = h.shape
    n = b * t
    ch = math.gcd(n, TOKEN_CHUNK)

    def chunk(xc):
        pg = jax.nn.softmax((xc @ w_grp).astype(jnp.float32) + b_grp.astype(jnp.float32), axis=-1)
        pg_top, g_idx = lax.top_k(pg, 1)
        le = ((xc @ w_er).astype(jnp.float32) + b_er.astype(jnp.float32)).reshape(-1, N_GROUPS, EXP_PER_GROUP)
        le = jnp.einsum('cge,cg->ce', le, jax.nn.one_hot(g_idx[:, 0], N_GROUPS, dtype=jnp.float32))
        pe_top, e_idx = lax.top_k(jax.nn.softmax(le, axis=-1), TOP_K_EXPERTS)
        wts = pg_top * pe_top / jnp.sum(pe_top, axis=-1, keepdims=True)
        comb = jnp.einsum('ck,cke->ce', wts, jax.nn.one_hot(g_idx * EXP_PER_GROUP + e_idx, N_EXPERTS, dtype=jnp.float32))
        a = jax.nn.silu(jnp.einsum('cd,edf->cef', xc, w_gate)) * jnp.einsum('cd,edf->cef', xc, w_up)
        return jnp.einsum('cef,efd->cd', a * comb[:, :, None].astype(a.dtype), w_down)

    return lax.map(chunk, h.reshape(n // ch, ch, d)).reshape(b, t, d)


def setup_inputs(seed: int = 0) -> dict:
    key = jax.random.key(seed)
    keys = iter(jax.random.split(key, 64))

    def nrm(shape, scale=1.0):
        return jax.random.normal(next(keys), shape, jnp.float32) * scale

    def gain(shape):
        return 1.0 + nrm(shape, 0.05)

    d = D_MODEL
    n_pages = PAST_LEN // PAGE_SIZE
    used = DEC_BATCH * n_pages
    pool = used + max(1, used // 4)
    win_rows = min(WINDOW, PAST_LEN)
    kv_e = (N_EVEN, pool, PAGE_SIZE, A_KV, HEAD_DIM)
    kv_o = (N_ODD, pool, PAGE_SIZE, C_KV, HEAD_DIM)
    win = (N_ODD, DEC_BATCH, win_rows, C_KV, HEAD_DIM)
    mem_c = (DEPTH, DEC_BATCH, N_MEM, X_HEADS, HEAD_DIM)
    page_table = jax.random.permutation(next(keys), pool)[:used].reshape(DEC_BATCH, n_pages).astype(jnp.int32)
    e_out = (A_HEADS + F_HEADS) * HEAD_DIM
    c_out = C_HEADS * HEAD_DIM
    x_w = X_HEADS * HEAD_DIM
    return {
        'x_prompt': nrm((BATCH, SEQ, d)),
        'x_sample': nrm((DEC_BATCH, DEC_SEQ, d)),
        'mem_prompt': nrm((BATCH, N_MEM, d)),
        'cache_dif_k': nrm(kv_e),
        'cache_dif_v': nrm(kv_e),
        'cache_fox_k': nrm(kv_e),
        'cache_fox_v': nrm(kv_e),
        'cache_fox_logf': jax.nn.log_sigmoid(FORGET_BIAS_INIT + nrm((N_EVEN, pool, PAGE_SIZE, F_HEADS))),
        'cache_cmp_k': nrm(kv_o),
        'cache_cmp_v': nrm(kv_o),
        'cache_slc_k': nrm(kv_o),
        'cache_slc_v': nrm(kv_o),
        'state_win_k': nrm(win),
        'state_win_v': nrm(win),
        'cache_mem_k': nrm(mem_c),
        'cache_mem_v': nrm(mem_c),
        'page_table': page_table,
        'g_mix': gain((DEPTH, d)),
        'w_in_e': nrm((N_EVEN, d, E_WIDTH), d ** -0.5),
        'b_forget': FORGET_BIAS_INIT + nrm((N_EVEN, F_HEADS), 0.1),
        'lam_q1': nrm((N_EVEN, A_HALF), 0.1),
        'lam_k1': nrm((N_EVEN, A_HALF), 0.1),
        'lam_q2': nrm((N_EVEN, A_HALF), 0.1),
        'lam_k2': nrm((N_EVEN, A_HALF), 0.1),
        'g_dif_head': gain((N_EVEN, HEAD_DIM)),
        'w_out_e': nrm((N_EVEN, e_out, d), e_out ** -0.5),
        'w_in_o': nrm((N_ODD, d, O_WIDTH), d ** -0.5),
        'cmp_pos_k': nrm((N_ODD, CMP_BLOCK, C_KV), 0.5),
        'cmp_pos_v': nrm((N_ODD, CMP_BLOCK, C_KV), 0.5),
        'w_out_o': nrm((N_ODD, c_out, d), c_out ** -0.5),
        'g_xattn': gain((DEPTH, d)),
        'g_mem': gain((DEPTH, d)),
        'w_xq': nrm((DEPTH, d, x_w), d ** -0.5),
        'w_xk': nrm((DEPTH, d, x_w), d ** -0.5),
        'w_xv': nrm((DEPTH, d, x_w), d ** -0.5),
        'w_xo': nrm((DEPTH, x_w, d), x_w ** -0.5),
        'g_ffn': gain((DEPTH, d)),
        'w_grp': nrm((DEPTH, d, N_GROUPS), d ** -0.5),
        'b_grp': nrm((DEPTH, N_GROUPS), 0.01),
        'w_er': nrm((DEPTH, d, N_EXPERTS), d ** -0.5),
        'b_er': nrm((DEPTH, N_EXPERTS), 0.01),
        'w_gate': nrm((DEPTH, N_EXPERTS, d, D_EXPERT), d ** -0.5),
        'w_up': nrm((DEPTH, N_EXPERTS, d, D_EXPERT), d ** -0.5),
        'w_down': nrm((DEPTH, N_EXPERTS, D_EXPERT, d), D_EXPERT ** -0.5),
        'g_final': gain((d,)),
    }


def reference(x_prompt, x_sample, mem_prompt,
              cache_dif_k, cache_dif_v, cache_fox_k, cache_fox_v, cache_fox_logf,
              cache_cmp_k, cache_cmp_v, cache_slc_k, cache_slc_v,
              state_win_k, state_win_v, cache_mem_k, cache_mem_v, page_table,
              g_mix, w_in_e, b_forget, lam_q1, lam_k1, lam_q2, lam_k2, g_dif_head, w_out_e,
              w_in_o, cmp_pos_k, cmp_pos_v, w_out_o,
              g_xattn, g_mem, w_xq, w_xk, w_xv, w_xo,
              g_ffn, w_grp, b_grp, w_er, b_er, w_gate, w_up, w_down, g_final):
    n_seq, n_pages = page_table.shape
    past_len = n_pages * PAGE_SIZE

    def paged(pool):
        return pool[page_table].reshape(n_seq, past_len, *pool.shape[2:])

    def run(x, mem_k, mem_v, past_even, past_odd, pos0):
        rows_even, rows_odd = [], []
        for l in range(DEPTH):
            j = l // 2
            h = _rmsnorm(x, g_mix[l])
            if l % 2 == 0:
                y, rows = _even_mixer(h, w_in_e[j], b_forget[j], lam_q1[j], lam_k1[j], lam_q2[j], lam_k2[j],
                                      g_dif_head[j], w_out_e[j], _lambda_init(l), past_even(j), pos0)
                rows_even.append(rows)
            else:
                y, rows = _odd_mixer(h, w_in_o[j], cmp_pos_k[j], cmp_pos_v[j], w_out_o[j], past_odd(j), pos0)
                rows_odd.append(rows)
            x = x + y
            x = x + _mem_attend(_rmsnorm(x, g_xattn[l]), mem_k[l], mem_v[l], w_xq[l], w_xo[l])
            x = x + _hmoe(_rmsnorm(x, g_ffn[l]), w_grp[l], b_grp[l], w_er[l], b_er[l], w_gate[l], w_up[l], w_down[l])
        stacked_even = [jnp.stack(a) for a in zip(*rows_even)]
        stacked_odd = [jnp.stack(a) for a in zip(*rows_odd)]
        return _rmsnorm(x, g_final), stacked_even, stacked_odd

    mem_rows = [_mem_kv(mem_prompt, g_mem[l], w_xk[l], w_xv[l]) for l in range(DEPTH)]
    mk_p = [m[0] for m in mem_rows]
    mv_p = [m[1] for m in mem_rows]
    y_prompt, even_p, odd_p = run(x_prompt, mk_p, mv_p, lambda j: None, lambda j: None, 0)
    y_sample, even_s, odd_s = run(
        x_sample, cache_mem_k, cache_mem_v,
        lambda j: (paged(cache_dif_k[j]), paged(cache_dif_v[j]), paged(cache_fox_k[j]), paged(cache_fox_v[j]), paged(cache_fox_logf[j])),
        lambda j: (paged(cache_cmp_k[j]), paged(cache_cmp_v[j]), paged(cache_slc_k[j]), paged(cache_slc_v[j]), state_win_k[j], state_win_v[j]),
        past_len)
    dif_k_p, dif_v_p, fox_k_p, fox_v_p, fox_logf_p = even_p
    cmp_k_p, cmp_v_p, slc_k_p, slc_v_p, win_k_p, win_v_p = odd_p
    dif_k_s, dif_v_s, fox_k_s, fox_v_s, fox_logf_s = even_s
    cmp_k_s, cmp_v_s, slc_k_s, slc_v_s, win_k_s, win_v_s = odd_s
    mem_k_p = jnp.stack(mk_p)
    mem_v_p = jnp.stack(mv_p)
    return (y_prompt, y_sample,
            dif_k_p, dif_v_p, fox_k_p, fox_v_p, fox_logf_p,
            cmp_k_p, cmp_v_p, slc_k_p, slc_v_p, win_k_p, win_v_p,
            mem_k_p, mem_v_p,
            dif_k_s, dif_v_s, fox_k_s, fox_v_s, fox_logf_s,
            cmp_k_s, cmp_v_s, slc_k_s, slc_v_s, win_k_s, win_v_s)
```

```python
import functools
import math

import numpy as np
import jax
import jax.numpy as jnp
from jax import lax
from jax.experimental import pallas as pl
from jax.experimental.pallas import tpu as pltpu

F32 = jnp.float32
BF16 = jnp.bfloat16

HEAD_DIM = 128
A_HEADS, A_KV = 8, 4
F_HEADS, F_KV = 8, 4
C_HEADS, C_KV = 16, 4
CMP_BLOCK = 64
SEL_BLOCK = 64
SEL_SHIFT = 6
TOP_N = 16
WINDOW = 512
X_HEADS = 4
N_GROUPS = 4
EXP_PER_GROUP = 8
N_EXPERTS = N_GROUPS * EXP_PER_GROUP
EPS = 1e-6
NEG = -1e30
FORCE = 1e4
E_MAIN = (A_HEADS + 2 * A_KV + F_HEADS + 2 * F_KV) * HEAD_DIM
O_MAIN = (C_HEADS + 6 * C_KV) * HEAD_DIM

LANE = 128
VMEM_LIMIT = 56 * 1024 * 1024
PAGES_PER_STEP = 8
MOE_TM = 256


def _cparams(sem):
    return pltpu.CompilerParams(dimension_semantics=sem, vmem_limit_bytes=VMEM_LIMIT)


def _dot_nt(a, b):
    return lax.dot_general(a, b, (((1,), (1,)), ((), ())), preferred_element_type=F32)


def _dot(a, b):
    return jnp.dot(a, b, preferred_element_type=F32)


def _alibi_slopes(n_heads):
    return jnp.asarray(np.exp2(-8.0 * np.arange(1, n_heads + 1) / n_heads).astype(np.float32))


def _lambda_init(layer):
    return 0.8 - 0.6 * math.exp(-0.3 * layer)


def _nmm_body(*refs, norm, epilogue, n_out):
    it = iter(refs)
    a_ref = next(it)
    g_ref = next(it) if norm else None
    b_ref = next(it)
    e_ref = next(it) if epilogue is not None else None
    outs = [next(it) for _ in range(n_out)]
    as_ref = next(it)

    @pl.when(pl.program_id(1) == 0)
    def _():
        a = a_ref[...].astype(F32)
        if norm:
            a = a * lax.rsqrt(jnp.mean(a * a, axis=-1, keepdims=True) + EPS) * g_ref[...]
        as_ref[...] = a.astype(BF16)

    acc = _dot(as_ref[...], b_ref[...])
    if epilogue == "res":
        acc = acc + e_ref[...]
    elif epilogue == "logsig":
        z = acc + e_ref[...]
        acc = jnp.minimum(z, 0.0) - jnp.log1p(jnp.exp(-jnp.abs(z)))
    for o in outs:
        o[...] = acc.astype(o.dtype)


def _nmm(a, b, *, g=None, res=None, bias=None, out_dtypes=(F32,), tm, tn):
    m, k = a.shape
    n = b.shape[1]
    assert m % tm == 0 and n % tn == 0
    norm = g is not None
    epilogue = "res" if res is not None else ("logsig" if bias is not None else None)
    ins = [a]
    specs = [pl.BlockSpec((tm, k), lambda i, j: (i, 0))]
    if norm:
        ins.append(g.reshape(1, k).astype(F32))
        specs.append(pl.BlockSpec((1, k), lambda i, j: (0, 0)))
    ins.append(b)
    specs.append(pl.BlockSpec((k, tn), lambda i, j: (0, j)))
    if epilogue == "res":
        ins.append(res)
        specs.append(pl.BlockSpec((tm, tn), lambda i, j: (i, j)))
    elif epilogue == "logsig":
        ins.append(bias.reshape(1, n).astype(F32))
        specs.append(pl.BlockSpec((1, tn), lambda i, j: (0, j)))
    out_shape = [jax.ShapeDtypeStruct((m, n), dt) for dt in out_dtypes]
    out_specs = [pl.BlockSpec((tm, tn), lambda i, j: (i, j)) for _ in out_dtypes]
    outs = pl.pallas_call(
        functools.partial(_nmm_body, norm=norm, epilogue=epilogue, n_out=len(out_dtypes)),
        grid=(m // tm, n // tn),
        in_specs=specs,
        out_specs=out_specs,
        out_shape=out_shape,
        scratch_shapes=[pltpu.VMEM((tm, k), BF16)],
        compiler_params=_cparams(("parallel", "arbitrary")),
    )(*ins)
    return outs[0] if len(out_dtypes) == 1 else outs


def _cumsum_body(x_ref, o_ref):
    n_chunks = x_ref.shape[-1] // LANE
    r = lax.broadcasted_iota(jnp.int32, (LANE, LANE), 0)
    c = lax.broadcasted_iota(jnp.int32, (LANE, LANE), 1)
    upper = (r <= c).astype(F32)
    carry = jnp.zeros((x_ref.shape[0], 1), F32)
    for j in range(n_chunks):
        blk = x_ref[:, j * LANE:(j + 1) * LANE]
        cs = jnp.dot(blk, upper, preferred_element_type=F32, precision=lax.Precision.HIGHEST)
        o_ref[:, j * LANE:(j + 1) * LANE] = cs + carry
        carry = carry + cs[:, LANE - 1:LANE]


def _cumsum(x):
    nb, h, l = x.shape
    return pl.pallas_call(
        _cumsum_body,
        grid=(nb,),
        in_specs=[pl.BlockSpec((None, h, l), lambda i: (i, 0, 0))],
        out_specs=pl.BlockSpec((None, h, l), lambda i: (i, 0, 0)),
        out_shape=jax.ShapeDtypeStruct((nb, h, l), F32),
        compiler_params=_cparams(("parallel",)),
    )(x)


def _online_update(s, v, m_ref, l_ref, acc_ref, idx):
    m_prev = m_ref[idx]
    m_new = jnp.maximum(m_prev, jnp.max(s, axis=-1, keepdims=True))
    alpha = jnp.exp(m_prev - m_new)
    p = jnp.exp(s - m_new)
    l_ref[idx] = alpha * l_ref[idx] + jnp.sum(p, axis=-1, keepdims=True)
    acc_ref[idx] = alpha * acc_ref[idx] + _dot(p.astype(BF16), v)
    m_ref[idx] = m_new


def _lambda_value(lp_ref, lam_init):
    lp = lp_ref[...]
    a = jnp.sum(lp[0:1] * lp[1:2], axis=-1, keepdims=True)
    b = jnp.sum(lp[2:3] * lp[3:4], axis=-1, keepdims=True)
    return jnp.exp(a) - jnp.exp(b) + lam_init


def _dif_head_out(acc_ref, l_ref, i1, i2, lam, gh, lam_init):
    o = acc_ref[i1] / l_ref[i1] - lam * (acc_ref[i2] / l_ref[i2])
    o = o * lax.rsqrt(jnp.mean(o * o, axis=-1, keepdims=True) + EPS) * gh
    return o * (1.0 - lam_init)


EP_T = 256


def _even_prompt_body(sl_ref, qa_ref, ka_ref, va_ref, qf_ref, kf_ref, vf_ref, ck_ref, lp_ref, gh_ref,
                      o_ref, qs_ref, m_ref, l_ref, acc_ref, *, lam_init):
    h = pl.program_id(1)
    qi = pl.program_id(2)
    t = EP_T
    lane = lax.broadcasted_iota(jnp.int32, (t, HEAD_DIM), 1)
    zero = jnp.zeros((t, HEAD_DIM), BF16)
    for g in range(2):
        q = qa_ref[:, g * HEAD_DIM:(g + 1) * HEAD_DIM] * jnp.asarray(0.125, BF16)
        qs_ref[g, 0] = jnp.where(lane < HEAD_DIM // 2, q, zero)
        qs_ref[g, 1] = jnp.where(lane >= HEAD_DIM // 2, q, zero)
        qs_ref[g, 2] = qf_ref[:, g * HEAD_DIM:(g + 1) * HEAD_DIM]
    m_ref[...] = jnp.full(m_ref.shape, NEG, F32)
    l_ref[...] = jnp.zeros(l_ref.shape, F32)
    acc_ref[...] = jnp.zeros(acc_ref.shape, F32)
    scale_f = HEAD_DIM ** -0.5

    def tile(ki, diag):
        k0 = pl.multiple_of(ki * t, t)
        ka = ka_ref[pl.ds(k0, t), :]
        va = va_ref[pl.ds(k0, t), :]
        kf = kf_ref[pl.ds(k0, t), :]
        vf = vf_ref[pl.ds(k0, t), :]
        rel = (k0 - qi * t) + lax.broadcasted_iota(jnp.int32, (1, t), 1)
        relf = rel.astype(F32)
        if diag:
            mask = rel <= lax.broadcasted_iota(jnp.int32, (t, 1), 0)
        for g in range(2):
            bias_a = sl_ref[2 * h + g] * relf
            bias_f = -ck_ref[g:g + 1, pl.ds(k0, t)]
            for kind in range(3):
                if kind < 2:
                    s = _dot_nt(qs_ref[g, kind], ka) + bias_a
                    v = va
                else:
                    s = _dot_nt(qs_ref[g, 2], kf) * scale_f + bias_f
                    v = vf
                if diag:
                    s = jnp.where(mask, s, NEG)
                _online_update(s, v, m_ref, l_ref, acc_ref, (g, kind))

    def body(ki, carry):
        tile(ki, False)
        return carry

    lax.fori_loop(0, qi, body, 0)
    tile(qi, True)

    lam = _lambda_value(lp_ref, lam_init)
    gh = gh_ref[...]
    for g in range(2):
        od = _dif_head_out(acc_ref, l_ref, (g, 0), (g, 1), lam, gh, lam_init)
        of = acc_ref[g, 2] / l_ref[g, 2]
        o_ref[:, g * HEAD_DIM:(g + 1) * HEAD_DIM] = od.astype(o_ref.dtype)
        o_ref[:, (2 + g) * HEAD_DIM:(3 + g) * HEAD_DIM] = of.astype(o_ref.dtype)


def _even_prompt_attn(qkv16, ck, lam_p, g_head, n_batch, seq, lam_init):
    t = EP_T
    nq = seq // t
    d = HEAD_DIM
    kv_spec = lambda col0: pl.BlockSpec((seq, d), lambda b, h, qi: (b, col0 + h))
    q_spec = lambda col0: pl.BlockSpec((t, 2 * d), lambda b, h, qi: (b * nq + qi, col0 + h))
    return pl.pallas_call(
        functools.partial(_even_prompt_body, lam_init=lam_init),
        grid=(n_batch, A_KV, nq),
        in_specs=[
            pl.BlockSpec(memory_space=pltpu.SMEM),
            q_spec(0),
            kv_spec(8), kv_spec(12),
            q_spec(8),
            kv_spec(24), kv_spec(28),
            pl.BlockSpec((None, None, 2, seq), lambda b, h, qi: (b, h, 0, 0)),
            pl.BlockSpec((4, d // 2), lambda b, h, qi: (0, 0)),
            pl.BlockSpec((1, d), lambda b, h, qi: (0, 0)),
        ],
        out_specs=pl.BlockSpec((t, 4 * d), lambda b, h, qi: (b * nq + qi, h)),
        scratch_shapes=[
            pltpu.VMEM((2, 3, t, d), BF16),
            pltpu.VMEM((2, 3, t, 1), F32),
            pltpu.VMEM((2, 3, t, 1), F32),
            pltpu.VMEM((2, 3, t, d), F32),
        ],
        out_shape=jax.ShapeDtypeStruct((n_batch * seq, 4 * A_KV * d), BF16),
        compiler_params=_cparams(("parallel", "parallel", "arbitrary")),
    )(_alibi_slopes(A_HEADS), qkv16, qkv16, qkv16, qkv16, qkv16, qkv16, ck, lam_p, g_head.reshape(1, d))


def _even_sample_body(*refs, n_pages, past_len, dec, lam_init):
    p_cnt = PAGES_PER_STEP
    pt_ref, sl_ref, qkv_ref, ck_ref, ckn_ref, lp_ref, gh_ref = refs[:7]
    pages = refs[7:7 + 4 * p_cnt]
    dk, dv, fk, fv = (pages[i * p_cnt:(i + 1) * p_cnt] for i in range(4))
    o_ref, qs_ref, m_ref, l_ref, acc_ref = refs[7 + 4 * p_cnt:]
    c = pl.program_id(1)
    d = HEAD_DIM
    rows = 2 * dec
    row = lax.broadcasted_iota(jnp.int32, (rows, 1), 0)
    first = row < dec
    scale_f = d ** -0.5
    lane = lax.broadcasted_iota(jnp.int32, (rows, d), 1)

    def stacked(col0):
        return jnp.concatenate([qkv_ref[:, col0:col0 + d], qkv_ref[:, col0 + d:col0 + 2 * d]], axis=0)

    @pl.when(c == 0)
    def _():
        for h in range(A_KV):
            q = (stacked(h * 2 * d) * 0.125).astype(BF16)
            zero = jnp.zeros_like(q)
            qs_ref[h, 0] = jnp.where(lane < d // 2, q, zero)
            qs_ref[h, 1] = jnp.where(lane >= d // 2, q, zero)
            qs_ref[h, 2] = stacked(2048 + h * 2 * d).astype(BF16)
        m_ref[...] = jnp.full(m_ref.shape, NEG, F32)
        l_ref[...] = jnp.zeros(l_ref.shape, F32)
        acc_ref[...] = jnp.zeros(acc_ref.shape, F32)

    def head_cols(prefs, h):
        return jnp.concatenate([r[:, h * d:(h + 1) * d] for r in prefs], axis=0).astype(BF16)

    n_keys = p_cnt * LANE
    relf = (c * n_keys - past_len + lax.broadcasted_iota(jnp.int32, (1, n_keys), 1)).astype(F32)

    def slope_col(h):
        return jnp.where(first, sl_ref[2 * h], sl_ref[2 * h + 1])

    def ck_rows(ref, h):
        return jnp.where(first, ref[2 * h:2 * h + 1, :], ref[2 * h + 1:2 * h + 2, :])

    for h in range(A_KV):
        ka, va, kf, vf = head_cols(dk, h), head_cols(dv, h), head_cols(fk, h), head_cols(fv, h)
        bias_a = slope_col(h) * relf
        bias_f = -ck_rows(ck_ref, h)
        for kind in range(3):
            if kind < 2:
                s = _dot_nt(qs_ref[h, kind], ka) + bias_a
                v = va
            else:
                s = _dot_nt(qs_ref[h, 2], kf) * scale_f + bias_f
                v = vf
            _online_update(s, v, m_ref, l_ref, acc_ref, (h, kind))

    @pl.when(c == n_pages // p_cnt - 1)
    def _():
        jn = lax.broadcasted_iota(jnp.int32, (1, LANE), 1)
        tok = jnp.where(first, row, row - dec)
        mask_n = (jn < dec) & (jn <= tok)
        reln = jn.astype(F32)
        pad = jnp.zeros((LANE - dec, d), F32)
        lam = _lambda_value(lp_ref, lam_init)
        gh = gh_ref[...]
        for h in range(A_KV):
            def new_rows(col0):
                return jnp.concatenate([qkv_ref[:, col0 + h * d:col0 + (h + 1) * d], pad], axis=0).astype(BF16)
            ka, va, kf, vf = new_rows(1024), new_rows(1536), new_rows(3072), new_rows(3584)
            bias_a = slope_col(h) * reln
            bias_f = -ck_rows(ckn_ref, h)
            for kind in range(3):
                if kind < 2:
                    s = _dot_nt(qs_ref[h, kind], ka) + bias_a
                    v = va
                else:
                    s = _dot_nt(qs_ref[h, 2], kf) * scale_f + bias_f
                    v = vf
                s = jnp.where(mask_n, s, NEG)
                _online_update(s, v, m_ref, l_ref, acc_ref, (h, kind))
            od = _dif_head_out(acc_ref, l_ref, (h, 0), (h, 1), lam, gh, lam_init)
            of = acc_ref[h, 2] / l_ref[h, 2]
            for g in range(2):
                o_ref[:, (4 * h + g) * d:(4 * h + g + 1) * d] = od[g * dec:(g + 1) * dec]
                o_ref[:, (4 * h + 2 + g) * d:(4 * h + 3 + g) * d] = of[g * dec:(g + 1) * dec]


def _even_sample_attn(qkv32, row_block0, caches, page_table, ck, lam_p, g_head, lam_init, dec):
    n_seq, n_pages = page_table.shape
    past_len = n_pages * LANE
    p_cnt = PAGES_PER_STEP
    d = HEAD_DIM
    n_chunks = n_pages // p_cnt
    in_specs = [
        pl.BlockSpec(memory_space=pltpu.SMEM),
        pl.BlockSpec((dec, qkv32.shape[1]), lambda s, c, pt: (row_block0 + s, 0)),
        pl.BlockSpec((None, 8, p_cnt * LANE), lambda s, c, pt: (s, 0, c)),
        pl.BlockSpec((None, 8, LANE), lambda s, c, pt: (s, 0, n_pages)),
        pl.BlockSpec((4, d // 2), lambda s, c, pt: (0, 0)),
        pl.BlockSpec((1, d), lambda s, c, pt: (0, 0)),
    ]
    ins = [_alibi_slopes(A_HEADS), qkv32, ck, ck, lam_p, g_head.reshape(1, d)]
    for arr in caches:
        for i in range(p_cnt):
            in_specs.append(pl.BlockSpec((None, LANE, 4 * d),
                                         lambda s, c, pt, i=i: (pt[s, c * p_cnt + i], 0, 0)))
            ins.append(arr)
    rows = 2 * dec
    grid_spec = pltpu.PrefetchScalarGridSpec(
        num_scalar_prefetch=1,
        grid=(n_seq, n_chunks),
        in_specs=in_specs,
        out_specs=pl.BlockSpec((dec, 16 * d), lambda s, c, pt: (s, 0)),
        scratch_shapes=[
            pltpu.VMEM((A_KV, 3, rows, d), BF16),
            pltpu.VMEM((A_KV, 3, rows, 1), F32),
            pltpu.VMEM((A_KV, 3, rows, 1), F32),
            pltpu.VMEM((A_KV, 3, rows, d), F32),
        ],
    )
    return pl.pallas_call(
        functools.partial(_even_sample_body, n_pages=n_pages, past_len=past_len, dec=dec, lam_init=lam_init),
        grid_spec=grid_spec,
        out_shape=jax.ShapeDtypeStruct((n_seq * dec, 16 * d), F32),
        compiler_params=_cparams(("parallel", "arbitrary")),
    )(page_table, *ins)


def _even_out_perm():
    idx = []
    for h in range(A_KV):
        for base in (0, A_HEADS):
            for g in range(2):
                head = base + 2 * h + g
                idx.extend(range(head * HEAD_DIM, (head + 1) * HEAD_DIM))
    return np.asarray(idx, np.int32)


def _pos_softmax_body(pk_ref, pv_ref, wk_ref, wv_ref):
    for src, dst in ((pk_ref, wk_ref), (pv_ref, wv_ref)):
        x = src[...]
        e = jnp.exp(x - jnp.max(x, axis=0, keepdims=True))
        w = e / jnp.sum(e, axis=0, keepdims=True)
        dst[...] = jnp.concatenate([w, w], axis=0)


def _pos_softmax(pos_k, pos_v):
    width = C_KV * HEAD_DIM
    expand = lambda p: jnp.repeat(p.astype(F32), HEAD_DIM, axis=1)
    shape = jax.ShapeDtypeStruct((2 * CMP_BLOCK, width), F32)
    return pl.pallas_call(_pos_softmax_body, out_shape=(shape, shape))(expand(pos_k), expand(pos_v))


def _block_sums(x, w):
    r, width = x.shape
    xw = x.reshape(r // LANE, LANE, width) * w[None]
    return jnp.sum(xw.reshape(r // CMP_BLOCK, CMP_BLOCK, width), axis=1)


def _compress_rows_body(k_ref, v_ref, wk_ref, wv_ref, ok_ref, ov_ref):
    ok_ref[...] = _block_sums(k_ref[...], wk_ref[...])
    ov_ref[...] = _block_sums(v_ref[...], wv_ref[...])


def _compress_rows(qkv32, wk, wv, n_rows, col_k, col_v):
    tr = 1024
    width = C_KV * HEAD_DIM
    shape = jax.ShapeDtypeStruct((n_rows // CMP_BLOCK, width), F32)
    w_spec = pl.BlockSpec((2 * CMP_BLOCK, width), lambda i: (0, 0))
    o_spec = pl.BlockSpec((tr // CMP_BLOCK, width), lambda i: (i, 0))
    return pl.pallas_call(
        _compress_rows_body,
        grid=(n_rows // tr,),
        in_specs=[pl.BlockSpec((tr, width), lambda i: (i, col_k)),
                  pl.BlockSpec((tr, width), lambda i: (i, col_v)), w_spec, w_spec],
        out_specs=(o_spec, o_spec),
        out_shape=(shape, shape),
        compiler_params=_cparams(("parallel",)),
    )(qkv32, qkv32, wk, wv)


def _compress_paged_body(*refs):
    p_cnt = PAGES_PER_STEP
    pt_ref = refs[0]
    kp = refs[1:1 + p_cnt]
    vp = refs[1 + p_cnt:1 + 2 * p_cnt]
    wk_ref, wv_ref, ok_ref, ov_ref = refs[1 + 2 * p_cnt:]
    del pt_ref
    wk, wv = wk_ref[...], wv_ref[...]
    ok_ref[...] = jnp.concatenate([_block_sums(r[...], wk) for r in kp], axis=0)
    ov_ref[...] = jnp.concatenate([_block_sums(r[...], wv) for r in vp], axis=0)


def _compress_paged(cache_k, cache_v, page_table, wk, wv):
    n_seq, n_pages = page_table.shape
    p_cnt = PAGES_PER_STEP
    width = C_KV * HEAD_DIM
    per_step = p_cnt * LANE // CMP_BLOCK
    page_spec = lambda i: pl.BlockSpec((None, LANE, width), lambda s, c, pt: (pt[s, c * p_cnt + i], 0, 0))
    w_spec = pl.BlockSpec((2 * CMP_BLOCK, width), lambda s, c, pt: (0, 0))
    o_spec = pl.BlockSpec((None, per_step, width), lambda s, c, pt: (s, c, 0))
    shape = jax.ShapeDtypeStruct((n_seq, n_pages * LANE // CMP_BLOCK, width), F32)
    grid_spec = pltpu.PrefetchScalarGridSpec(
        num_scalar_prefetch=1,
        grid=(n_seq, n_pages // p_cnt),
        in_specs=[page_spec(i) for i in range(p_cnt)] * 2 + [w_spec, w_spec],
        out_specs=(o_spec, o_spec),
    )
    return pl.pallas_call(
        _compress_paged_body,
        grid_spec=grid_spec,
        out_shape=(shape, shape),
        compiler_params=_cparams(("parallel", "arbitrary")),
    )(page_table, *([cache_k] * p_cnt), *([cache_v] * p_cnt), wk, wv)


def _softmax_rows(s):
    e = jnp.exp(s - jnp.max(s, axis=-1, keepdims=True))
    return e / jnp.sum(e, axis=-1, keepdims=True)


def _top_n_mask(score, n_blocks, valid):
    lane = lax.broadcasted_iota(jnp.int32, score.shape, 1)
    rank = jnp.zeros(score.shape, F32)
    for i in range(n_blocks):
        col = score[:, i:i + 1]
        ahead = (col > score) | ((col == score) & (lane > i))
        rank = rank + jnp.where(ahead, 1.0, 0.0)
    keep = (rank < float(TOP_N)) & (lane < n_blocks) & valid
    return jnp.where(keep, 1.0, 0.0)


def _stack_heads(ref, col0, n):
    return jnp.concatenate([ref[:, col0 + g * HEAD_DIM:col0 + (g + 1) * HEAD_DIM] for g in range(n)], axis=0)


def _row_select(row, step, vals):
    out = vals[-1]
    for g in range(len(vals) - 2, -1, -1):
        out = jnp.where(row < (g + 1) * step, vals[g], out)
    return out


OP_T = 128


def _odd_prompt_body(sl_ref, q_ref, ks_ref, vs_ref, kw_ref, vw_ref, kc_ref, vc_ref, gt_ref,
                     o_ref, sx_ref, m_ref, l_ref, acc_ref, *, seq):
    h = pl.program_id(1)
    qi = pl.program_id(2)
    t = OP_T
    d = HEAD_DIM
    gc = C_HEADS // C_KV
    rows = gc * t
    nb = seq // CMP_BLOCK
    scale = d ** -0.5
    q = _stack_heads(q_ref, 0, gc)
    row = lax.broadcasted_iota(jnp.int32, (rows, 1), 0)
    slope = _row_select(row, t, [sl_ref[gc * h + g] for g in range(gc)])
    tok = _row_select(row, t, [row - g * t for g in range(gc)])
    q0 = qi * t

    blk = lax.broadcasted_iota(jnp.int32, (1, nb), 1)
    bend_rel = (blk + 1) * CMP_BLOCK - 1 - q0
    mask_c = bend_rel <= tok
    s_c = _dot_nt(q, kc_ref[...].astype(BF16)) * scale + slope * bend_rel.astype(F32)
    p_c = jnp.where(mask_c, _softmax_rows(jnp.where(mask_c, s_c, NEG)), 0.0)
    o_c = _dot(p_c.astype(BF16), vc_ref[...].astype(BF16))

    imp = p_c[0:t]
    for g in range(1, gc):
        imp = imp + p_c[g * t:(g + 1) * t]
    tok1 = lax.broadcasted_iota(jnp.int32, (t, 1), 0)
    cur = (q0 + tok1) >> SEL_SHIFT
    forced = (blk == 0) | (blk == cur) | (blk == cur - 1)
    valid = blk * SEL_BLOCK <= q0 + tok1
    score = jnp.where(valid, imp + jnp.where(forced, FORCE, 0.0), NEG)
    sel = _top_n_mask(score, nb, valid).astype(BF16)
    er = lax.broadcasted_iota(jnp.int32, (nb, seq), 0)
    ec = lax.broadcasted_iota(jnp.int32, (nb, seq), 1)
    expand = jnp.where(er == (ec >> SEL_SHIFT), 1.0, 0.0).astype(BF16)
    sel_keys = _dot(sel, expand)
    for kt in range(seq // t):
        sx_ref[kt] = sel_keys[:, kt * t:(kt + 1) * t]

    m_ref[...] = jnp.full(m_ref.shape, NEG, F32)
    l_ref[...] = jnp.zeros(l_ref.shape, F32)
    acc_ref[...] = jnp.zeros(acc_ref.shape, F32)

    def tile(ki, diag):
        k0 = pl.multiple_of(ki * t, t)
        k = ks_ref[pl.ds(k0, t), :]
        v = vs_ref[pl.ds(k0, t), :]
        rel = (k0 - q0) + lax.broadcasted_iota(jnp.int32, (1, t), 1)
        sx = sx_ref[ki]
        keep = jnp.concatenate([sx] * gc, axis=0) > 0.5
        if diag:
            keep = keep & (rel <= tok)
        s = _dot_nt(q, k) * scale + slope * rel.astype(F32)
        _online_update(jnp.where(keep, s, NEG), v, m_ref, l_ref, acc_ref, 0)

    def body(ki, carry):
        tile(ki, False)
        return carry

    lax.fori_loop(0, qi, body, 0)
    tile(qi, True)
    o_s = acc_ref[0] / l_ref[0]

    n_slab = WINDOW + t
    start = pl.multiple_of(jnp.maximum(q0 - WINDOW, 0), t)
    kw = kw_ref[pl.ds(start, n_slab), :]
    vw = vw_ref[pl.ds(start, n_slab), :]
    rel_w = (start - q0) + lax.broadcasted_iota(jnp.int32, (1, n_slab), 1)
    delta = tok - rel_w
    mask_w = (delta >= 0) & (delta < WINDOW)
    s_w = _dot_nt(q, kw) * scale + slope * rel_w.astype(F32)
    p_w = _softmax_rows(jnp.where(mask_w, s_w, NEG))
    o_w = _dot(p_w.astype(BF16), vw)

    gates = jax.nn.sigmoid(gt_ref[...])
    for g in range(gc):
        rs = slice(g * t, (g + 1) * t)
        o = (gates[:, 3 * g:3 * g + 1] * o_c[rs] + gates[:, 3 * g + 1:3 * g + 2] * o_s[rs]
             + gates[:, 3 * g + 2:3 * g + 3] * o_w[rs])
        o_ref[:, g * d:(g + 1) * d] = o.astype(o_ref.dtype)


def _odd_prompt_attn(qkv16, k_cmp, v_cmp, gates, n_batch, seq):
    t = OP_T
    nq = seq // t
    d = HEAD_DIM
    gc = C_HEADS // C_KV
    kv_spec = lambda col0: pl.BlockSpec((seq, d), lambda b, h, qi: (b, col0 + h))
    cmp_spec = pl.BlockSpec((None, seq // CMP_BLOCK, d), lambda b, h, qi: (b, 0, h))
    return pl.pallas_call(
        functools.partial(_odd_prompt_body, seq=seq),
        grid=(n_batch, C_KV, nq),
        in_specs=[
            pl.BlockSpec(memory_space=pltpu.SMEM),
            pl.BlockSpec((t, gc * d), lambda b, h, qi: (b * nq + qi, h)),
            kv_spec(24), kv_spec(28), kv_spec(32), kv_spec(36),
            cmp_spec, cmp_spec,
            pl.BlockSpec((t, d), lambda b, h, qi: (b * nq + qi, h)),
        ],
        out_specs=pl.BlockSpec((t, gc * d), lambda b, h, qi: (b * nq + qi, h)),
        scratch_shapes=[
            pltpu.VMEM((seq // t, t, t), F32),
            pltpu.VMEM((1, gc * t, 1), F32),
            pltpu.VMEM((1, gc * t, 1), F32),
            pltpu.VMEM((1, gc * t, d), F32),
        ],
        out_shape=jax.ShapeDtypeStruct((n_batch * seq, C_HEADS * d), BF16),
        compiler_params=_cparams(("parallel", "parallel", "arbitrary")),
    )(_alibi_slopes(C_HEADS), qkv16, qkv16, qkv16, qkv16, qkv16, k_cmp, v_cmp, gates)


def _odd_sample_body(*refs, n_pages, past_len, dec):
    p_cnt = PAGES_PER_STEP
    pt_ref, sl_ref, qkv_ref, gt_ref, kc_ref, vc_ref, wk_ref, wv_ref = refs[:8]
    sk = refs[8:8 + p_cnt]
    sv = refs[8 + p_cnt:8 + 2 * p_cnt]
    o_ref, wko_ref, wvo_ref, qs_ref, sel_ref, oc_ref, m_ref, l_ref, acc_ref = refs[8 + 2 * p_cnt:]
    del pt_ref
    c = pl.program_id(1)
    d = HEAD_DIM
    gc = C_HEADS // C_KV
    rows = gc * dec
    width = C_KV * d
    nb = past_len // CMP_BLOCK
    nbs = -(-(past_len + dec) // SEL_BLOCK)
    sel_w = 2 * LANE
    assert nb == LANE and nbs <= sel_w and (past_len + dec - 1) // SEL_BLOCK == past_len // SEL_BLOCK
    cur = past_len // SEL_BLOCK
    scale = d ** -0.5
    row = lax.broadcasted_iota(jnp.int32, (rows, 1), 0)
    tok = _row_select(row, dec, [row - g * dec for g in range(gc)])
    slopes = [_row_select(row, dec, [sl_ref[gc * h + g] for g in range(gc)]) for h in range(C_KV)]
    c_ks, c_vs, c_kw, c_vw = (C_HEADS + 2 * C_KV) * d, (C_HEADS + 3 * C_KV) * d, (C_HEADS + 4 * C_KV) * d, (C_HEADS + 5 * C_KV) * d

    @pl.when(c == 0)
    def _():
        blk = lax.broadcasted_iota(jnp.int32, (1, nb), 1)
        bend_rel = ((blk + 1) * CMP_BLOCK - 1 - past_len).astype(F32)
        lane = lax.broadcasted_iota(jnp.int32, (dec, sel_w), 1)
        forced = (lane == 0) | (lane == cur) | (lane == cur - 1)
        for h in range(C_KV):
            q = _stack_heads(qkv_ref, h * gc * d, gc).astype(BF16)
            qs_ref[h] = q
            kc = kc_ref[:, h * d:(h + 1) * d].astype(BF16)
            vc = vc_ref[:, h * d:(h + 1) * d].astype(BF16)
            p_c = _softmax_rows(_dot_nt(q, kc) * scale + slopes[h] * bend_rel)
            oc_ref[h] = _dot(p_c.astype(BF16), vc)
            imp = p_c[0:dec]
            for g in range(1, gc):
                imp = imp + p_c[g * dec:(g + 1) * dec]
            imp = jnp.concatenate([imp, jnp.zeros((dec, sel_w - nb), F32)], axis=1)
            score = imp + jnp.where(forced, FORCE, 0.0)
            sel_ref[h] = _top_n_mask(score, nbs, lane < nbs)
        m_ref[...] = jnp.full(m_ref.shape, NEG, F32)
        l_ref[...] = jnp.zeros(l_ref.shape, F32)
        acc_ref[...] = jnp.zeros(acc_ref.shape, F32)

    n_keys = p_cnt * LANE
    relf = (c * n_keys - past_len + lax.broadcasted_iota(jnp.int32, (1, n_keys), 1)).astype(F32)
    er = lax.broadcasted_iota(jnp.int32, (sel_w, n_keys), 0)
    ec = lax.broadcasted_iota(jnp.int32, (sel_w, n_keys), 1)
    expand = jnp.where(er - c * (n_keys // SEL_BLOCK) == (ec >> SEL_SHIFT), 1.0, 0.0).astype(BF16)
    for h in range(C_KV):
        k = jnp.concatenate([r[:, h * d:(h + 1) * d] for r in sk], axis=0).astype(BF16)
        v = jnp.concatenate([r[:, h * d:(h + 1) * d] for r in sv], axis=0).astype(BF16)
        keys = _dot(sel_ref[h].astype(BF16), expand)
        keep = jnp.concatenate([keys] * gc, axis=0) > 0.5
        s = _dot_nt(qs_ref[h], k) * scale + slopes[h] * relf
        _online_update(jnp.where(keep, s, NEG), v, m_ref, l_ref, acc_ref, h)

    @pl.when(c == n_pages // p_cnt - 1)
    def _():
        jn = lax.broadcasted_iota(jnp.int32, (1, LANE), 1)
        reln = jn.astype(F32)
        causal_n = (jn < dec) & (jn <= tok)
        pad = jnp.zeros((LANE - dec, d), F32)
        gates = jax.nn.sigmoid(gt_ref[...])
        n_win = WINDOW + LANE
        je = lax.broadcasted_iota(jnp.int32, (1, n_win), 1)
        rel_w = je - WINDOW
        delta = tok - rel_w
        mask_w = (delta >= 0) & (delta < WINDOW) & (je < WINDOW + dec)
        for h in range(C_KV):
            q = qs_ref[h]
            new = lambda col0: jnp.concatenate([qkv_ref[:, col0 + h * d:col0 + (h + 1) * d], pad], axis=0)
            picked = jnp.concatenate([sel_ref[h][:, cur:cur + 1]] * gc, axis=0) > 0.5
            s = _dot_nt(q, new(c_ks).astype(BF16)) * scale + slopes[h] * reln
            _online_update(jnp.where(causal_n & picked, s, NEG), new(c_vs).astype(BF16), m_ref, l_ref, acc_ref, h)
            o_s = acc_ref[h] / l_ref[h]
            kw = jnp.concatenate([wk_ref[:, h * d:(h + 1) * d], new(c_kw)], axis=0).astype(BF16)
            vw = jnp.concatenate([wv_ref[:, h * d:(h + 1) * d], new(c_vw)], axis=0).astype(BF16)
            s_w = _dot_nt(q, kw) * scale + slopes[h] * rel_w.astype(F32)
            p_w = _softmax_rows(jnp.where(mask_w, s_w, NEG))
            o_w = _dot(p_w.astype(BF16), vw)
            o_c = oc_ref[h]
            for g in range(gc):
                rs = slice(g * dec, (g + 1) * dec)
                l0 = h * d + 3 * g
                o = (gates[:, l0:l0 + 1] * o_c[rs] + gates[:, l0 + 1:l0 + 2] * o_s[rs]
                     + gates[:, l0 + 2:l0 + 3] * o_w[rs])
                o_ref[:, (gc * h + g) * d:(gc * h + g + 1) * d] = o
        wko_ref[0:WINDOW - dec, :] = wk_ref[dec:WINDOW, :]
        wko_ref[WINDOW - dec:WINDOW, :] = qkv_ref[:, c_kw:c_kw + width]
        wvo_ref[0:WINDOW - dec, :] = wv_ref[dec:WINDOW, :]
        wvo_ref[WINDOW - dec:WINDOW, :] = qkv_ref[:, c_vw:c_vw + width]


def _odd_sample_attn(qkv32, gates, row_block0, k_cmp, v_cmp, win_k, win_v, cache_k, cache_v, page_table, dec):
    n_seq, n_pages = page_table.shape
    past_len = n_pages * LANE
    p_cnt = PAGES_PER_STEP
    d = HEAD_DIM
    gc = C_HEADS // C_KV
    width = C_KV * d
    rows = gc * dec
    page_spec = lambda i: pl.BlockSpec((None, LANE, width), lambda s, c, pt: (pt[s, c * p_cnt + i], 0, 0))
    seq_spec = lambda r: pl.BlockSpec((None, r, width), lambda s, c, pt: (s, 0, 0))
    in_specs = [
        pl.BlockSpec(memory_space=pltpu.SMEM),
        pl.BlockSpec((dec, qkv32.shape[1]), lambda s, c, pt: (row_block0 + s, 0)),
        pl.BlockSpec((dec, width), lambda s, c, pt: (row_block0 + s, 0)),
        seq_spec(past_len // CMP_BLOCK), seq_spec(past_len // CMP_BLOCK),
        seq_spec(WINDOW), seq_spec(WINDOW),
    ] + [page_spec(i) for i in range(p_cnt)] * 2
    grid_spec = pltpu.PrefetchScalarGridSpec(
        num_scalar_prefetch=1,
        grid=(n_seq, n_pages // p_cnt),
        in_specs=in_specs,
        out_specs=(pl.BlockSpec((dec, C_HEADS * d), lambda s, c, pt: (s, 0)), seq_spec(WINDOW), seq_spec(WINDOW)),
        scratch_shapes=[
            pltpu.VMEM((C_KV, rows, d), BF16),
            pltpu.VMEM((C_KV, dec, 2 * LANE), F32),
            pltpu.VMEM((C_KV, rows, d), F32),
            pltpu.VMEM((C_KV, rows, 1), F32),
            pltpu.VMEM((C_KV, rows, 1), F32),
            pltpu.VMEM((C_KV, rows, d), F32),
        ],
    )
    win_shape = jax.ShapeDtypeStruct((n_seq, WINDOW, width), F32)
    return pl.pallas_call(
        functools.partial(_odd_sample_body, n_pages=n_pages, past_len=past_len, dec=dec),
        grid_spec=grid_spec,
        out_shape=(jax.ShapeDtypeStruct((n_seq * dec, C_HEADS * d), F32), win_shape, win_shape),
        compiler_params=_cparams(("parallel", "arbitrary")),
    )(page_table, _alibi_slopes(C_HEADS), qkv32, gates, k_cmp, v_cmp, win_k, win_v,
      *([cache_k] * p_cnt), *([cache_v] * p_cnt))


def _xattn_body(q_ref, k_ref, v_ref, o_ref):
    d = HEAD_DIM
    scale = d ** -0.5
    tq = q_ref.shape[0]
    pad_rows = (-tq) % 16
    for h in range(X_HEADS):
        cols = slice(h * d, (h + 1) * d)
        q = q_ref[:, cols]
        if pad_rows:
            q = jnp.concatenate([q, jnp.zeros((pad_rows, d), F32)], axis=0)
        p = _softmax_rows(_dot_nt(q.astype(BF16), k_ref[:, cols].astype(BF16)) * scale)
        o = _dot(p.astype(BF16), v_ref[:, cols].astype(BF16))
        o_ref[:, cols] = o[0:tq].astype(o_ref.dtype)


def _xattn(q, mem_k, mem_v, row_block0, tq, n_b, nq):
    width = X_HEADS * HEAD_DIM
    n_mem = mem_k.shape[1]
    mem_spec = pl.BlockSpec((None, n_mem, width), lambda b, i: (b, 0, 0))
    return pl.pallas_call(
        _xattn_body,
        grid=(n_b, nq),
        in_specs=[pl.BlockSpec((tq, width), lambda b, i: (row_block0 + b * nq + i, 0)), mem_spec, mem_spec],
        out_specs=pl.BlockSpec((tq, width), lambda b, i: (b * nq + i, 0)),
        out_shape=jax.ShapeDtypeStruct((n_b * nq * tq, width), F32),
        compiler_params=_cparams(("parallel", "arbitrary")),
    )(q, mem_k, mem_v)


def _router_body(x_ref, g_ref, w_ref, b_ref, o_ref):
    x = x_ref[...]
    hn = x * lax.rsqrt(jnp.mean(x * x, axis=-1, keepdims=True) + EPS) * g_ref[...]
    logits = jnp.dot(hn, w_ref[...], preferred_element_type=F32, precision=lax.Precision.HIGHEST) + b_ref[...]
    lg = logits[:, 0:LANE]
    le = logits[:, LANE:2 * LANE]
    lane = lax.broadcasted_iota(jnp.int32, lg.shape, 1).astype(F32)
    big = float(2 * LANE)
    ninf = -jnp.inf

    def first_max(v):
        m = jnp.max(v, axis=-1, keepdims=True)
        return m, jnp.min(jnp.where(v == m, lane, big), axis=-1, keepdims=True)

    lgm = jnp.where(lane < N_GROUPS, lg, ninf)
    mg, gi = first_max(lgm)
    pg_top = 1.0 / jnp.sum(jnp.exp(lgm - mg), axis=-1, keepdims=True)
    lo = gi * EXP_PER_GROUP
    lem = jnp.where((lane >= lo) & (lane < lo + EXP_PER_GROUP), le, ninf)
    m1, i1 = first_max(lem)
    lem2 = jnp.where(lane == i1, ninf, lem)
    m2, i2 = first_max(lem2)
    z = jnp.sum(jnp.exp(lem - m1), axis=-1, keepdims=True)
    pe1 = 1.0 / z
    pe2 = jnp.exp(m2 - m1) / z
    w1 = pg_top * pe1 / (pe1 + pe2)
    w2 = pg_top * pe2 / (pe1 + pe2)
    o_ref[...] = jnp.where(lane == 0, i1, jnp.where(lane == 1, i2, jnp.where(lane == 2, w1, jnp.where(lane == 3, w2, 0.0))))


def _router(x, g, w_grp, b_grp, w_er, b_er, tm):
    n, d = x.shape
    w = jnp.zeros((d, 2 * LANE), F32).at[:, :N_GROUPS].set(w_grp).at[:, LANE:LANE + N_EXPERTS].set(w_er)
    b = jnp.zeros((1, 2 * LANE), F32).at[0, :N_GROUPS].set(b_grp).at[0, LANE:LANE + N_EXPERTS].set(b_er)
    return pl.pallas_call(
        _router_body,
        grid=(n // tm,),
        in_specs=[pl.BlockSpec((tm, d), lambda i: (i, 0)), pl.BlockSpec((1, d), lambda i: (0, 0)),
                  pl.BlockSpec((d, 2 * LANE), lambda i: (0, 0)), pl.BlockSpec((1, 2 * LANE), lambda i: (0, 0))],
        out_specs=pl.BlockSpec((tm, LANE), lambda i: (i, 0)),
        out_shape=jax.ShapeDtypeStruct((n, LANE), F32),
        compiler_params=_cparams(("parallel",)),
    )(x, g.reshape(1, d).astype(F32), w, b)


def _dispatch_tables(e1, e2, n_tiles):
    n = e1.shape[0]
    tm = MOE_TM
    e = jnp.concatenate([e1, e2])
    tok = jnp.concatenate([jnp.arange(n, dtype=jnp.int32)] * 2)
    onehot = (e[:, None] == jnp.arange(N_EXPERTS, dtype=jnp.int32)[None, :]).astype(jnp.int32)
    counts = jnp.sum(onehot, axis=0)
    rank = jnp.take_along_axis(jnp.cumsum(onehot, axis=0) - onehot, e[:, None], axis=1)[:, 0]
    padded = ((counts + tm - 1) // tm) * tm
    ends = jnp.cumsum(padded)
    pos = (ends - padded)[e] + rank
    src = jnp.zeros((n_tiles * tm,), jnp.int32).at[pos].set(tok)
    n_used = (ends[-1] // tm).astype(jnp.int32)
    tile_start = jnp.arange(n_tiles, dtype=jnp.int32) * tm
    tile_e = jnp.minimum(jnp.searchsorted(ends, tile_start, side="right"), N_EXPERTS - 1).astype(jnp.int32)
    tile_e = jnp.where(jnp.arange(n_tiles) < n_used, tile_e, tile_e[jnp.maximum(n_used - 1, 0)])
    return tile_e, n_used.reshape(1), src, pos[:n], pos[n:]


def _gather_rows(idx_ref, base, n, src_hbm, dst, sem):
    def copy(r):
        return pltpu.make_async_copy(src_hbm.at[pl.ds(idx_ref[base + r], 1)], dst.at[pl.ds(r, 1)], sem)

    def start(r, carry):
        copy(r).start()
        return carry

    def wait(r, carry):
        copy(r).wait()
        return carry

    lax.fori_loop(0, n, start, 0)
    lax.fori_loop(0, n, wait, 0)


def _experts_body(te_ref, nu_ref, src_ref, x_hbm, g_ref, wg_ref, wu_ref, wd_ref, y_ref, xbuf, sem):
    t = pl.program_id(0)
    del te_ref

    @pl.when(t < nu_ref[0])
    def _():
        _gather_rows(src_ref, t * MOE_TM, MOE_TM, x_hbm, xbuf, sem)
        x = xbuf[...]
        hn = (x * lax.rsqrt(jnp.mean(x * x, axis=-1, keepdims=True) + EPS) * g_ref[...]).astype(BF16)
        gate = _dot(hn, wg_ref[...].astype(BF16))
        up = _dot(hn, wu_ref[...].astype(BF16))
        act = (gate * jax.nn.sigmoid(gate) * up).astype(BF16)
        y_ref[...] = _dot(act, wd_ref[...].astype(BF16))

    @pl.when(t >= nu_ref[0])
    def _():
        y_ref[...] = jnp.zeros(y_ref.shape, F32)


def _experts(tile_e, n_used, src, x, g, w_gate, w_up, w_down):
    n_tiles = tile_e.shape[0]
    d = x.shape[1]
    f = w_gate.shape[2]
    tm = MOE_TM
    grid_spec = pltpu.PrefetchScalarGridSpec(
        num_scalar_prefetch=3,
        grid=(n_tiles,),
        in_specs=[
            pl.BlockSpec(memory_space=pl.ANY),
            pl.BlockSpec((1, d), lambda t, te, nu, sr: (0, 0)),
            pl.BlockSpec((None, d, f), lambda t, te, nu, sr: (te[t], 0, 0)),
            pl.BlockSpec((None, d, f), lambda t, te, nu, sr: (te[t], 0, 0)),
            pl.BlockSpec((None, f, d), lambda t, te, nu, sr: (te[t], 0, 0)),
        ],
        out_specs=pl.BlockSpec((tm, d), lambda t, te, nu, sr: (t, 0)),
        scratch_shapes=[pltpu.VMEM((tm, d), F32), pltpu.SemaphoreType.DMA(())],
    )
    return pl.pallas_call(
        _experts_body,
        grid_spec=grid_spec,
        out_shape=jax.ShapeDtypeStruct((n_tiles * tm, d), F32),
        compiler_params=_cparams(("arbitrary",)),
    )(tile_e, n_used, src, x, g.reshape(1, d).astype(F32), w_gate, w_up, w_down)


def _combine_body(p1_ref, p2_ref, x_ref, r_ref, y_hbm, o_ref, ybuf, sem):
    i = pl.program_id(0)
    tm = x_ref.shape[0]
    _gather_rows(p1_ref, i * tm, tm, y_hbm, ybuf.at[0], sem)
    _gather_rows(p2_ref, i * tm, tm, y_hbm, ybuf.at[1], sem)
    r = r_ref[...]
    o_ref[...] = x_ref[...] + r[:, 2:3] * ybuf[0] + r[:, 3:4] * ybuf[1]


def _combine(p1, p2, x, r, y, tm):
    n, d = x.shape
    grid_spec = pltpu.PrefetchScalarGridSpec(
        num_scalar_prefetch=2,
        grid=(n // tm,),
        in_specs=[
            pl.BlockSpec((tm, d), lambda i, a, b: (i, 0)),
            pl.BlockSpec((tm, LANE), lambda i, a, b: (i, 0)),
            pl.BlockSpec(memory_space=pl.ANY),
        ],
        out_specs=pl.BlockSpec((tm, d), lambda i, a, b: (i, 0)),
        scratch_shapes=[pltpu.VMEM((2, tm, d), F32), pltpu.SemaphoreType.DMA(())],
    )
    return pl.pallas_call(
        _combine_body,
        grid_spec=grid_spec,
        out_shape=jax.ShapeDtypeStruct((n, d), F32),
        compiler_params=_cparams(("arbitrary",)),
    )(p1, p2, x, r, y)


def _moe(x, g, w_grp, b_grp, w_er, b_er, w_gate, w_up, w_down, tm):
    n = x.shape[0]
    r = _router(x, g, w_grp, b_grp, w_er, b_er, tm)
    e1 = r[:, 0].astype(jnp.int32)
    e2 = r[:, 1].astype(jnp.int32)
    n_tiles = -(-(2 * n + N_EXPERTS * (MOE_TM - 1)) // MOE_TM)
    tile_e, n_used, src, p1, p2 = _dispatch_tables(e1, e2, n_tiles)
    y = _experts(tile_e, n_used, src, x, g, w_gate, w_up, w_down)
    return _combine(p1, p2, x, r, y, MOE_TM)


def _rmsnorm_body(x_ref, g_ref, o_ref):
    x = x_ref[...]
    o_ref[...] = x * lax.rsqrt(jnp.mean(x * x, axis=-1, keepdims=True) + EPS) * g_ref[...]


def _rmsnorm(x, g, tm):
    n, d = x.shape
    return pl.pallas_call(
        _rmsnorm_body,
        grid=(n // tm,),
        in_specs=[pl.BlockSpec((tm, d), lambda i: (i, 0)), pl.BlockSpec((1, d), lambda i: (0, 0))],
        out_specs=pl.BlockSpec((tm, d), lambda i: (i, 0)),
        out_shape=jax.ShapeDtypeStruct((n, d), F32),
        compiler_params=_cparams(("parallel",)),
    )(x, g.reshape(1, d).astype(F32))


def _joint_tile(n):
    for tm in (768, 512, 256, 128):
        if n % tm == 0:
            return tm
    raise ValueError(f"token count {n} has no supported row tile")


def kernel(x_prompt, x_sample, mem_prompt, cache_dif_k, cache_dif_v, cache_fox_k, cache_fox_v, cache_fox_logf, cache_cmp_k, cache_cmp_v, cache_slc_k, cache_slc_v, state_win_k, state_win_v, cache_mem_k, cache_mem_v, page_table, g_mix, w_in_e, b_forget, lam_q1, lam_k1, lam_q2, lam_k2, g_dif_head, w_out_e, w_in_o, cmp_pos_k, cmp_pos_v, w_out_o, g_xattn, g_mem, w_xq, w_xk, w_xv, w_xo, g_ffn, w_grp, b_grp, w_er, b_er, w_gate, w_up, w_down, g_final):
    n_b, seq, d = x_prompt.shape
    n_s, dec, _ = x_sample.shape
    n_mem = mem_prompt.shape[1]
    depth = g_mix.shape[0]
    n_pages = page_table.shape[1]
    past_len = n_pages * LANE
    n_p = n_b * seq
    n_tok = n_p + n_s * dec
    hd = HEAD_DIM
    kvw = 4 * hd
    rb0 = n_p // dec
    tmj = _joint_tile(n_tok)
    assert seq % EP_T == 0 and seq % 1024 == 0 and n_p % dec == 0 and n_pages % PAGES_PER_STEP == 0
    assert cache_dif_k.shape[2] == LANE and state_win_k.shape[2] == WINDOW

    x = jnp.concatenate([x_prompt.reshape(n_p, d), x_sample.reshape(n_s * dec, d)], axis=0)
    paged = lambda a: a.reshape(a.shape[0], LANE, kvw)

    mem = mem_prompt.reshape(n_b * n_mem, d)
    mem_kv = []
    for l in range(depth):
        w_kv = jnp.concatenate([w_xk[l], w_xv[l]], axis=1).astype(BF16)
        mem_kv.append(_nmm(mem, w_kv, g=g_mem[l], tm=_joint_tile(n_b * n_mem), tn=512))

    even_p, even_s, odd_p, odd_s = [], [], [], []
    for l in range(depth):
        j = l // 2
        if l % 2 == 0:
            w_main = w_in_e[j][:, :E_MAIN].astype(BF16)
            w_fg = jnp.pad(w_in_e[j][:, E_MAIN:], ((0, 0), (0, LANE - F_HEADS))).astype(BF16)
            qkv32, qkv16 = _nmm(x, w_main, g=g_mix[l], out_dtypes=(F32, BF16), tm=tmj, tn=512)
            logf = _nmm(x, w_fg, g=g_mix[l], bias=jnp.pad(b_forget[j], (0, LANE - F_HEADS)), tm=tmj, tn=LANE)
            logf = logf[:, :F_HEADS]
            lf_p = logf[:n_p].reshape(n_b, seq, F_HEADS)
            lf_s = logf[n_p:].reshape(n_s, dec, F_HEADS)
            ck_p = _cumsum(lf_p.transpose(0, 2, 1)).reshape(n_b, F_KV, F_HEADS // F_KV, seq)
            lf_past = cache_fox_logf[j][page_table].reshape(n_s, past_len, F_HEADS)
            lf_all = jnp.concatenate([lf_past, lf_s, jnp.zeros((n_s, LANE - dec, F_HEADS), F32)], axis=1)
            ck_s = _cumsum(lf_all.transpose(0, 2, 1))
            lam_p = jnp.stack([lam_q1[j], lam_k1[j], lam_q2[j], lam_k2[j]]).astype(F32)
            lam_init = _lambda_init(l)
            o_p = _even_prompt_attn(qkv16, ck_p, lam_p, g_dif_head[j], n_b, seq, lam_init)
            o_s = _even_sample_attn(qkv32, rb0, [paged(cache_dif_k[j]), paged(cache_dif_v[j]),
                                                 paged(cache_fox_k[j]), paged(cache_fox_v[j])],
                                    page_table, ck_s, lam_p, g_dif_head[j], lam_init, dec)
            w_out = w_out_e[j][_even_out_perm()].astype(BF16)
            cols = lambda a, c0: a[:, c0:c0 + kvw]
            even_p.append([cols(qkv32[:n_p], c0).reshape(n_b, seq, 4, hd) for c0 in (1024, 1536, 3072, 3584)] + [lf_p])
            even_s.append([cols(qkv32[n_p:], c0).reshape(n_s, dec, 4, hd) for c0 in (1024, 1536, 3072, 3584)] + [lf_s])
        else:
            w_main = w_in_o[j][:, :O_MAIN].astype(BF16)
            w_g = w_in_o[j][:, O_MAIN:].reshape(d, C_KV, 3 * C_HEADS // C_KV)
            w_g = jnp.pad(w_g, ((0, 0), (0, 0), (0, LANE - w_g.shape[2]))).reshape(d, C_KV * LANE).astype(BF16)
            qkv32, qkv16 = _nmm(x, w_main, g=g_mix[l], out_dtypes=(F32, BF16), tm=tmj, tn=512)
            gates = _nmm(x, w_g, g=g_mix[l], tm=tmj, tn=512)
            wk, wv = _pos_softmax(cmp_pos_k[j], cmp_pos_v[j])
            kc_p, vc_p = _compress_rows(qkv32, wk, wv, n_p, 4, 5)
            kc_p = kc_p.reshape(n_b, seq // CMP_BLOCK, kvw)
            vc_p = vc_p.reshape(n_b, seq // CMP_BLOCK, kvw)
            kc_s, vc_s = _compress_paged(paged(cache_cmp_k[j]), paged(cache_cmp_v[j]), page_table, wk, wv)
            o_p = _odd_prompt_attn(qkv16, kc_p, vc_p, gates, n_b, seq)
            o_s, win_k, win_v = _odd_sample_attn(
                qkv32, gates, rb0, kc_s, vc_s, state_win_k[j].reshape(n_s, WINDOW, kvw),
                state_win_v[j].reshape(n_s, WINDOW, kvw), paged(cache_slc_k[j]), paged(cache_slc_v[j]),
                page_table, dec)
            w_out = w_out_o[j].astype(BF16)
            n_keep = min(WINDOW, seq)
            cols_p = lambda c0: qkv32[:n_p, c0:c0 + kvw].reshape(n_b, seq, 4, hd)
            cols_s = lambda c0: qkv32[n_p:, c0:c0 + kvw].reshape(n_s, dec, 4, hd)
            odd_p.append([cols_p(c0) for c0 in (2048, 2560, 3072, 3584)]
                         + [cols_p(c0)[:, seq - n_keep:] for c0 in (4096, 4608)])
            odd_s.append([cols_s(c0) for c0 in (2048, 2560, 3072, 3584)]
                         + [win_k.reshape(n_s, WINDOW, 4, hd), win_v.reshape(n_s, WINDOW, 4, hd)])
        o = jnp.concatenate([o_p, o_s.astype(BF16)], axis=0)
        x = _nmm(o, w_out, res=x, tm=tmj, tn=512)

        q = _nmm(x, w_xq[l].astype(BF16), g=g_xattn[l], tm=tmj, tn=512)
        mk_p = mem_kv[l][:, :X_HEADS * hd].reshape(n_b, n_mem, X_HEADS * hd)
        mv_p = mem_kv[l][:, X_HEADS * hd:].reshape(n_b, n_mem, X_HEADS * hd)
        ox_p = _xattn(q, mk_p, mv_p, 0, 512, n_b, seq // 512)
        ox_s = _xattn(q, cache_mem_k[l].reshape(n_s, n_mem, X_HEADS * hd),
                      cache_mem_v[l].reshape(n_s, n_mem, X_HEADS * hd), rb0, dec, n_s, 1)
        x = _nmm(jnp.concatenate([ox_p, ox_s], axis=0), w_xo[l].astype(BF16), res=x, tm=tmj, tn=512)

        x = _moe(x, g_ffn[l], w_grp[l], b_grp[l], w_er[l], b_er[l], w_gate[l], w_up[l], w_down[l], tmj)

    y = _rmsnorm(x, g_final, tmj)
    y_prompt = y[:n_p].reshape(n_b, seq, d)
    y_sample = y[n_p:].reshape(n_s, dec, d)
    stack = lambda rows: [jnp.stack(a) for a in zip(*rows)]
    mem_k_p = jnp.stack([m[:, :X_HEADS * hd].reshape(n_b, n_mem, X_HEADS, hd) for m in mem_kv])
    mem_v_p = jnp.stack([m[:, X_HEADS * hd:].reshape(n_b, n_mem, X_HEADS, hd) for m in mem_kv])
    return (y_prompt, y_sample, *stack(even_p), *stack(odd_p), mem_k_p, mem_v_p, *stack(even_s), *stack(odd_s))
```

```python
import functools
import math

import numpy as np
import jax
import jax.numpy as jnp
from jax import lax
from jax.experimental import pallas as pl
from jax.experimental.pallas import tpu as pltpu

F32 = jnp.float32
BF16 = jnp.bfloat16

HEAD_DIM = 128
A_HEADS, A_KV = 8, 4
F_HEADS, F_KV = 8, 4
C_HEADS, C_KV = 16, 4
KV = 4
CMP_BLOCK = 64
SEL_BLOCK = 64
SEL_SHIFT = 6
TOP_N = 16
WINDOW = 512
X_HEADS = 4
N_GROUPS = 4
EXP_PER_GROUP = 8
N_EXPERTS = N_GROUPS * EXP_PER_GROUP
EPS = 1e-6
NEG = -1e30
FORCE = 1e4
E_MAIN = (A_HEADS + 2 * A_KV + F_HEADS + 2 * F_KV) * HEAD_DIM
O_MAIN = (C_HEADS + 6 * C_KV) * HEAD_DIM

LANE = 128
VMEM_LIMIT = 56 * 1024 * 1024
PAGES_PER_STEP = 8
MOE_TM = 256
KV_COLS = KV * HEAD_DIM


def _cparams(sem):
    return pltpu.CompilerParams(dimension_semantics=sem, vmem_limit_bytes=VMEM_LIMIT)


def _dot_nt(a, b):
    return lax.dot_general(a, b, (((1,), (1,)), ((), ())), preferred_element_type=F32)


def _dot(a, b):
    return jnp.dot(a, b, preferred_element_type=F32)


def _alibi_slopes(n_heads):
    return jnp.asarray(np.exp2(-8.0 * np.arange(1, n_heads + 1) / n_heads).astype(np.float32))


def _lambda_init(layer):
    return 0.8 - 0.6 * math.exp(-0.3 * layer)


def _head_rows(ref, h, n, row0=0):
    return ref[pl.ds(row0 * KV + h, n, stride=KV), :]


def _nmm_body(*refs, norm, epilogue, n_out, kv_cols):
    it = iter(refs)
    a_ref = next(it)
    g_ref = next(it) if norm else None
    b_ref = next(it)
    e_ref = next(it) if epilogue is not None else None
    outs = [next(it) for _ in range(n_out)]
    kv_refs = [next(it) for _ in kv_cols]
    as_ref = next(it)
    tm = a_ref.shape[0]

    @pl.when(pl.program_id(1) == 0)
    def _():
        a = a_ref[...].astype(F32)
        if norm:
            a = a * lax.rsqrt(jnp.mean(a * a, axis=-1, keepdims=True) + EPS) * g_ref[...]
        as_ref[...] = a.astype(BF16)

    acc = _dot(as_ref[...], b_ref[...])
    if epilogue == "res":
        acc = acc + e_ref[...]
    elif epilogue == "logsig":
        z = acc + e_ref[...]
        acc = jnp.minimum(z, 0.0) - jnp.log1p(jnp.exp(-jnp.abs(z)))
    for o in outs:
        o[...] = acc.astype(o.dtype)
    for ref, jcol in zip(kv_refs, kv_cols):
        @pl.when(pl.program_id(1) == jcol)
        def _(ref=ref):
            for h in range(KV):
                ref[pl.ds(h, tm, stride=KV), :] = acc[:, h * HEAD_DIM:(h + 1) * HEAD_DIM]


def _nmm(a, b, *, g=None, res=None, bias=None, out_dtypes=(F32,), kv_cols=(), m=None, row_off=0, tm, tn):
    k = a.shape[1]
    m = a.shape[0] if m is None else m
    n = b.shape[1]
    assert m % tm == 0 and n % tn == 0 and (not kv_cols or tn == KV_COLS)
    norm = g is not None
    epilogue = "res" if res is not None else ("logsig" if bias is not None else None)
    ins = [a]
    specs = [pl.BlockSpec((tm, k), lambda i, j: (i + row_off, 0))]
    if norm:
        ins.append(g.reshape(1, k).astype(F32))
        specs.append(pl.BlockSpec((1, k), lambda i, j: (0, 0)))
    ins.append(b)
    specs.append(pl.BlockSpec((k, tn), lambda i, j: (0, j)))
    if epilogue == "res":
        ins.append(res)
        specs.append(pl.BlockSpec((tm, tn), lambda i, j: (i + row_off, j)))
    elif epilogue == "logsig":
        ins.append(bias.reshape(1, n).astype(F32))
        specs.append(pl.BlockSpec((1, tn), lambda i, j: (0, j)))
    out_shape = [jax.ShapeDtypeStruct((m, n), dt) for dt in out_dtypes]
    out_specs = [pl.BlockSpec((tm, tn), lambda i, j: (i, j)) for _ in out_dtypes]
    out_shape += [jax.ShapeDtypeStruct((m * KV, HEAD_DIM), F32) for _ in kv_cols]
    out_specs += [pl.BlockSpec((tm * KV, HEAD_DIM), lambda i, j: (i, 0)) for _ in kv_cols]
    outs = pl.pallas_call(
        functools.partial(_nmm_body, norm=norm, epilogue=epilogue, n_out=len(out_dtypes), kv_cols=tuple(kv_cols)),
        grid=(m // tm, n // tn),
        in_specs=specs,
        out_specs=out_specs,
        out_shape=out_shape,
        scratch_shapes=[pltpu.VMEM((tm, k), BF16)],
        compiler_params=_cparams(("parallel", "arbitrary")),
    )(*ins)
    return outs[0] if len(outs) == 1 else outs


def _cumsum_body(x_ref, o_ref):
    n_chunks = x_ref.shape[-1] // LANE
    r = lax.broadcasted_iota(jnp.int32, (LANE, LANE), 0)
    c = lax.broadcasted_iota(jnp.int32, (LANE, LANE), 1)
    upper = (r <= c).astype(F32)
    carry = jnp.zeros((x_ref.shape[0], 1), F32)
    for j in range(n_chunks):
        blk = x_ref[:, j * LANE:(j + 1) * LANE]
        cs = jnp.dot(blk, upper, preferred_element_type=F32, precision=lax.Precision.HIGHEST)
        o_ref[:, j * LANE:(j + 1) * LANE] = cs + carry
        carry = carry + cs[:, LANE - 1:LANE]


def _cumsum(x):
    nb, h, l = x.shape
    return pl.pallas_call(
        _cumsum_body,
        grid=(nb,),
        in_specs=[pl.BlockSpec((None, h, l), lambda i: (i, 0, 0))],
        out_specs=pl.BlockSpec((None, h, l), lambda i: (i, 0, 0)),
        out_shape=jax.ShapeDtypeStruct((nb, h, l), F32),
        compiler_params=_cparams(("parallel",)),
    )(x)


def _with_ones(v):
    return jnp.concatenate([v, jnp.ones_like(v)], axis=1)


def _online_update(s, v_ext, m_ref, acc_ref, idx):
    m_prev = m_ref[idx]
    m_new = jnp.maximum(m_prev, jnp.max(s, axis=-1, keepdims=True))
    alpha = jnp.exp(m_prev - m_new)
    p = jnp.exp(s - jnp.tile(m_new, (1, s.shape[1] // LANE)))
    acc_ref[idx] = jnp.tile(alpha, (1, 2)) * acc_ref[idx] + _dot(p.astype(BF16), v_ext)
    m_ref[idx] = m_new


def _flash_out(acc):
    d = acc.shape[1] // 2
    return acc[:, :d] / acc[:, d:]


def _softmax_rows(s):
    e = jnp.exp(s - jnp.max(s, axis=-1, keepdims=True))
    return e / jnp.sum(e, axis=-1, keepdims=True)


def _softmax_apply(s, v_ext):
    e = jnp.exp(s - jnp.max(s, axis=-1, keepdims=True))
    return _flash_out(_dot(e.astype(BF16), v_ext))


def _lambda_value(lp_ref, lam_init):
    lp = lp_ref[...]
    a = jnp.sum(lp[0:1] * lp[1:2], axis=-1, keepdims=True)
    b = jnp.sum(lp[2:3] * lp[3:4], axis=-1, keepdims=True)
    return jnp.exp(a) - jnp.exp(b) + lam_init


def _dif_head_out(acc1, acc2, lam, gh, lam_init):
    o = _flash_out(acc1) - lam * _flash_out(acc2)
    o = o * lax.rsqrt(jnp.mean(o * o, axis=-1, keepdims=True) + EPS) * gh
    return o * (1.0 - lam_init)


def _row_select(row, step, vals):
    out = vals[-1]
    for g in range(len(vals) - 2, -1, -1):
        out = jnp.where(row < (g + 1) * step, vals[g], out)
    return out


EP_T = 512


def _even_prompt_body(sl_ref, qa_ref, ka_ref, va_ref, qf_ref, kf_ref, vf_ref, ck_ref, lp_ref, gh_ref,
                      o_ref, qs_ref, m_ref, acc_ref, *, lam_init):
    h = pl.program_id(1)
    qi = pl.program_id(2)
    t = EP_T
    d = HEAD_DIM
    lane = lax.broadcasted_iota(jnp.int32, (t, d), 1)
    zero = jnp.zeros((t, d), BF16)
    for g in range(2):
        q = qa_ref[:, g * d:(g + 1) * d] * jnp.asarray(0.125, BF16)
        qs_ref[g, 0] = jnp.where(lane < d // 2, q, zero)
        qs_ref[g, 1] = jnp.where(lane >= d // 2, q, zero)
        qs_ref[g, 2] = qf_ref[:, g * d:(g + 1) * d]
    m_ref[...] = jnp.full(m_ref.shape, NEG, F32)
    acc_ref[...] = jnp.zeros(acc_ref.shape, F32)
    scale_f = d ** -0.5

    def tile(ki, diag):
        k0 = pl.multiple_of(ki * t, t)
        ka = ka_ref[pl.ds(k0, t), :]
        va = _with_ones(va_ref[pl.ds(k0, t), :])
        kf = kf_ref[pl.ds(k0, t), :]
        vf = _with_ones(vf_ref[pl.ds(k0, t), :])
        rel = (k0 - qi * t) + lax.broadcasted_iota(jnp.int32, (1, t), 1)
        relf = rel.astype(F32)
        if diag:
            mask = rel <= lax.broadcasted_iota(jnp.int32, (t, 1), 0)
        for g in range(2):
            bias_a = sl_ref[2 * h + g] * relf
            bias_f = -ck_ref[g:g + 1, pl.ds(k0, t)]
            for kind in range(3):
                if kind < 2:
                    s = _dot_nt(qs_ref[g, kind], ka) + bias_a
                    v = va
                else:
                    s = _dot_nt(qs_ref[g, 2], kf) * scale_f + bias_f
                    v = vf
                if diag:
                    s = jnp.where(mask, s, NEG)
                _online_update(s, v, m_ref, acc_ref, (g, kind))

    def body(ki, carry):
        tile(ki, False)
        return carry

    lax.fori_loop(0, qi, body, 0)
    tile(qi, True)

    lam = _lambda_value(lp_ref, lam_init)
    gh = gh_ref[...]
    for g in range(2):
        od = _dif_head_out(acc_ref[g, 0], acc_ref[g, 1], lam, gh, lam_init)
        of = _flash_out(acc_ref[g, 2])
        o_ref[:, g * d:(g + 1) * d] = od.astype(o_ref.dtype)
        o_ref[:, (2 + g) * d:(3 + g) * d] = of.astype(o_ref.dtype)


def _even_prompt_attn(qkv16, ck, lam_p, g_head, n_batch, seq, lam_init):
    t = EP_T
    nq = seq // t
    d = HEAD_DIM
    kv_spec = lambda col0: pl.BlockSpec((seq, d), lambda b, h, qi: (b, col0 + h))
    q_spec = lambda col0: pl.BlockSpec((t, 2 * d), lambda b, h, qi: (b * nq + qi, col0 + h))
    return pl.pallas_call(
        functools.partial(_even_prompt_body, lam_init=lam_init),
        grid=(n_batch, A_KV, nq),
        in_specs=[
            pl.BlockSpec(memory_space=pltpu.SMEM),
            q_spec(0),
            kv_spec(8), kv_spec(12),
            q_spec(8),
            kv_spec(24), kv_spec(28),
            pl.BlockSpec((None, None, 2, seq), lambda b, h, qi: (b, h, 0, 0)),
            pl.BlockSpec((4, d // 2), lambda b, h, qi: (0, 0)),
            pl.BlockSpec((1, d), lambda b, h, qi: (0, 0)),
        ],
        out_specs=pl.BlockSpec((t, 4 * d), lambda b, h, qi: (b * nq + qi, h)),
        scratch_shapes=[
            pltpu.VMEM((2, 3, t, d), BF16),
            pltpu.VMEM((2, 3, t, LANE), F32),
            pltpu.VMEM((2, 3, t, 2 * d), F32),
        ],
        out_shape=jax.ShapeDtypeStruct((n_batch * seq, 4 * A_KV * d), BF16),
        compiler_params=_cparams(("parallel", "parallel", "arbitrary")),
    )(_alibi_slopes(A_HEADS), qkv16, qkv16, qkv16, qkv16, qkv16, qkv16, ck, lam_p, g_head.reshape(1, d))


def _even_sample_body(*refs, n_pages, past_len, dec, lam_init):
    p_cnt = PAGES_PER_STEP
    pt_ref, sl_ref, qkv_ref, ck_ref, ckn_ref, lp_ref, gh_ref = refs[:7]
    pages = refs[7:7 + 4 * p_cnt]
    dk, dv, fk, fv = (pages[i * p_cnt:(i + 1) * p_cnt] for i in range(4))
    o_ref, qs_ref, m_ref, acc_ref = refs[7 + 4 * p_cnt:]
    del pt_ref
    c = pl.program_id(1)
    d = HEAD_DIM
    rows = 2 * dec
    row = lax.broadcasted_iota(jnp.int32, (rows, 1), 0)
    first = row < dec
    scale_f = d ** -0.5
    lane = lax.broadcasted_iota(jnp.int32, (rows, d), 1)

    def stacked(col0):
        return jnp.concatenate([qkv_ref[:, col0:col0 + d], qkv_ref[:, col0 + d:col0 + 2 * d]], axis=0)

    @pl.when(c == 0)
    def _():
        for h in range(A_KV):
            q = (stacked(h * 2 * d) * 0.125).astype(BF16)
            zero = jnp.zeros_like(q)
            qs_ref[h, 0] = jnp.where(lane < d // 2, q, zero)
            qs_ref[h, 1] = jnp.where(lane >= d // 2, q, zero)
            qs_ref[h, 2] = stacked(2048 + h * 2 * d).astype(BF16)
        m_ref[...] = jnp.full(m_ref.shape, NEG, F32)
        acc_ref[...] = jnp.zeros(acc_ref.shape, F32)

    def head_keys(prefs, h):
        return jnp.concatenate([_head_rows(r, h, LANE) for r in prefs], axis=0).astype(BF16)

    n_keys = p_cnt * LANE
    relf = (c * n_keys - past_len + lax.broadcasted_iota(jnp.int32, (1, n_keys), 1)).astype(F32)

    def slope_col(h):
        return jnp.where(first, sl_ref[2 * h], sl_ref[2 * h + 1])

    def ck_rows(ref, h):
        return jnp.where(first, ref[2 * h:2 * h + 1, :], ref[2 * h + 1:2 * h + 2, :])

    for h in range(A_KV):
        ka, kf = head_keys(dk, h), head_keys(fk, h)
        va, vf = _with_ones(head_keys(dv, h)), _with_ones(head_keys(fv, h))
        bias_a = slope_col(h) * relf
        bias_f = -ck_rows(ck_ref, h)
        for kind in range(3):
            if kind < 2:
                s = _dot_nt(qs_ref[h, kind], ka) + bias_a
                v = va
            else:
                s = _dot_nt(qs_ref[h, 2], kf) * scale_f + bias_f
                v = vf
            _online_update(s, v, m_ref, acc_ref, (h, kind))

    @pl.when(c == n_pages // p_cnt - 1)
    def _():
        jn = lax.broadcasted_iota(jnp.int32, (1, LANE), 1)
        tok = jnp.where(first, row, row - dec)
        mask_n = (jn < dec) & (jn <= tok)
        reln = jn.astype(F32)
        pad = jnp.zeros((LANE - dec, d), F32)
        lam = _lambda_value(lp_ref, lam_init)
        gh = gh_ref[...]
        for h in range(A_KV):
            def new_rows(col0):
                return jnp.concatenate([qkv_ref[:, col0 + h * d:col0 + (h + 1) * d], pad], axis=0).astype(BF16)
            ka, kf = new_rows(1024), new_rows(3072)
            va, vf = _with_ones(new_rows(1536)), _with_ones(new_rows(3584))
            bias_a = slope_col(h) * reln
            bias_f = -ck_rows(ckn_ref, h)
            for kind in range(3):
                if kind < 2:
                    s = _dot_nt(qs_ref[h, kind], ka) + bias_a
                    v = va
                else:
                    s = _dot_nt(qs_ref[h, 2], kf) * scale_f + bias_f
                    v = vf
                s = jnp.where(mask_n, s, NEG)
                _online_update(s, v, m_ref, acc_ref, (h, kind))
            od = _dif_head_out(acc_ref[h, 0], acc_ref[h, 1], lam, gh, lam_init)
            of = _flash_out(acc_ref[h, 2])
            for g in range(2):
                o_ref[:, (4 * h + g) * d:(4 * h + g + 1) * d] = od[g * dec:(g + 1) * dec]
                o_ref[:, (4 * h + 2 + g) * d:(4 * h + 3 + g) * d] = of[g * dec:(g + 1) * dec]


def _page_spec(p_cnt, i):
    return pl.BlockSpec((None, LANE * KV, HEAD_DIM), lambda s, c, pt: (pt[s, c * p_cnt + i], 0, 0))


def _even_sample_attn(qkv32, caches, page_table, ck, lam_p, g_head, lam_init, dec):
    n_seq, n_pages = page_table.shape
    past_len = n_pages * LANE
    p_cnt = PAGES_PER_STEP
    d = HEAD_DIM
    n_chunks = n_pages // p_cnt
    in_specs = [
        pl.BlockSpec(memory_space=pltpu.SMEM),
        pl.BlockSpec((dec, qkv32.shape[1]), lambda s, c, pt: (s, 0)),
        pl.BlockSpec((None, 8, p_cnt * LANE), lambda s, c, pt: (s, 0, c)),
        pl.BlockSpec((None, 8, LANE), lambda s, c, pt: (s, 0, n_pages)),
        pl.BlockSpec((4, d // 2), lambda s, c, pt: (0, 0)),
        pl.BlockSpec((1, d), lambda s, c, pt: (0, 0)),
    ]
    ins = [_alibi_slopes(A_HEADS), qkv32, ck, ck, lam_p, g_head.reshape(1, d)]
    for arr in caches:
        for i in range(p_cnt):
            in_specs.append(_page_spec(p_cnt, i))
            ins.append(arr)
    rows = 2 * dec
    grid_spec = pltpu.PrefetchScalarGridSpec(
        num_scalar_prefetch=1,
        grid=(n_seq, n_chunks),
        in_specs=in_specs,
        out_specs=pl.BlockSpec((dec, 16 * d), lambda s, c, pt: (s, 0)),
        scratch_shapes=[
            pltpu.VMEM((A_KV, 3, rows, d), BF16),
            pltpu.VMEM((A_KV, 3, rows, LANE), F32),
            pltpu.VMEM((A_KV, 3, rows, 2 * d), F32),
        ],
    )
    return pl.pallas_call(
        functools.partial(_even_sample_body, n_pages=n_pages, past_len=past_len, dec=dec, lam_init=lam_init),
        grid_spec=grid_spec,
        out_shape=jax.ShapeDtypeStruct((n_seq * dec, 16 * d), F32),
        compiler_params=_cparams(("parallel", "arbitrary")),
    )(page_table, *ins)


def _even_out_perm():
    idx = []
    for h in range(A_KV):
        for base in (0, A_HEADS):
            for g in range(2):
                head = base + 2 * h + g
                idx.extend(range(head * HEAD_DIM, (head + 1) * HEAD_DIM))
    return np.asarray(idx, np.int32)


def _pos_softmax_body(pk_ref, pv_ref, wk_ref, wv_ref):
    for src, dst in ((pk_ref, wk_ref), (pv_ref, wv_ref)):
        x = src[...]
        e = jnp.exp(x - jnp.max(x, axis=1, keepdims=True))
        w = e / jnp.sum(e, axis=1, keepdims=True)
        dst[...] = jnp.concatenate([w, w], axis=1)


def _pos_softmax(pos_k, pos_v):
    expand = lambda p: jnp.broadcast_to(p.astype(F32).T[:, :, None], (C_KV, CMP_BLOCK, HEAD_DIM))
    shape = jax.ShapeDtypeStruct((C_KV, 2 * CMP_BLOCK, HEAD_DIM), F32)
    return pl.pallas_call(_pos_softmax_body, out_shape=(shape, shape))(expand(pos_k), expand(pos_v))


def _block_sums(x, w):
    r, width = x.shape
    xw = x.reshape(r // LANE, LANE, width) * w[None]
    return jnp.sum(xw.reshape(r // CMP_BLOCK, CMP_BLOCK, width), axis=1)


def _compress_rows_body(k_ref, v_ref, wk_ref, wv_ref, ok_ref, ov_ref):
    d = HEAD_DIM
    for h in range(C_KV):
        ok_ref[h] = _block_sums(k_ref[:, h * d:(h + 1) * d].astype(F32), wk_ref[h])
        ov_ref[h] = _block_sums(v_ref[:, h * d:(h + 1) * d].astype(F32), wv_ref[h])


def _compress_rows(qkv16, wk, wv, n_batch, seq, col_k, col_v):
    tr = 1024
    nc = seq // tr
    d = HEAD_DIM
    shape = jax.ShapeDtypeStruct((n_batch, C_KV, seq // CMP_BLOCK, d), F32)
    w_spec = pl.BlockSpec((C_KV, 2 * CMP_BLOCK, d), lambda b, i: (0, 0, 0))
    o_spec = pl.BlockSpec((None, C_KV, tr // CMP_BLOCK, d), lambda b, i: (b, 0, i, 0))
    return pl.pallas_call(
        _compress_rows_body,
        grid=(n_batch, nc),
        in_specs=[pl.BlockSpec((tr, KV_COLS), lambda b, i: (b * nc + i, col_k)),
                  pl.BlockSpec((tr, KV_COLS), lambda b, i: (b * nc + i, col_v)), w_spec, w_spec],
        out_specs=(o_spec, o_spec),
        out_shape=(shape, shape),
        compiler_params=_cparams(("parallel", "parallel")),
    )(qkv16, qkv16, wk, wv)


def _compress_paged_body(*refs):
    p_cnt = PAGES_PER_STEP
    kp = refs[1:1 + p_cnt]
    vp = refs[1 + p_cnt:1 + 2 * p_cnt]
    wk_ref, wv_ref, ok_ref, ov_ref = refs[1 + 2 * p_cnt:]
    for h in range(C_KV):
        wk, wv = wk_ref[h], wv_ref[h]
        ok_ref[h] = jnp.concatenate([_block_sums(_head_rows(r, h, LANE), wk) for r in kp], axis=0)
        ov_ref[h] = jnp.concatenate([_block_sums(_head_rows(r, h, LANE), wv) for r in vp], axis=0)


def _compress_paged(cache_k, cache_v, page_table, wk, wv):
    n_seq, n_pages = page_table.shape
    p_cnt = PAGES_PER_STEP
    d = HEAD_DIM
    per_step = p_cnt * LANE // CMP_BLOCK
    w_spec = pl.BlockSpec((C_KV, 2 * CMP_BLOCK, d), lambda s, c, pt: (0, 0, 0))
    o_spec = pl.BlockSpec((None, C_KV, per_step, d), lambda s, c, pt: (s, 0, c, 0))
    shape = jax.ShapeDtypeStruct((n_seq, C_KV, n_pages * LANE // CMP_BLOCK, d), F32)
    grid_spec = pltpu.PrefetchScalarGridSpec(
        num_scalar_prefetch=1,
        grid=(n_seq, n_pages // p_cnt),
        in_specs=[_page_spec(p_cnt, i) for i in range(p_cnt)] * 2 + [w_spec, w_spec],
        out_specs=(o_spec, o_spec),
    )
    return pl.pallas_call(
        _compress_paged_body,
        grid_spec=grid_spec,
        out_shape=(shape, shape),
        compiler_params=_cparams(("parallel", "arbitrary")),
    )(page_table, *([cache_k] * p_cnt), *([cache_v] * p_cnt), wk, wv)


def _top_n_mask(score, n_blocks, valid):
    lane = lax.broadcasted_iota(jnp.int32, score.shape, 1)
    rank = jnp.zeros(score.shape, F32)
    for i in range(n_blocks):
        col = score[:, i:i + 1]
        ahead = (col > score) | ((col == score) & (lane > i))
        rank = rank + jnp.where(ahead, 1.0, 0.0)
    keep = (rank < float(TOP_N)) & (lane < n_blocks) & valid
    return jnp.where(keep, 1.0, 0.0)


def _stack_heads(ref, col0, n):
    return jnp.concatenate([ref[:, col0 + g * HEAD_DIM:col0 + (g + 1) * HEAD_DIM] for g in range(n)], axis=0)


OP_T = 128
OP_TK = 512


def _odd_prompt_body(sl_ref, q_ref, ks_ref, vs_ref, kw_ref, vw_ref, kc_ref, vc_ref, gt_ref,
                     o_ref, sx_ref, m_ref, acc_ref, *, seq):
    h = pl.program_id(1)
    qi = pl.program_id(2)
    t = OP_T
    tk = OP_TK
    d = HEAD_DIM
    gc = C_HEADS // C_KV
    rows = gc * t
    nb = seq // CMP_BLOCK
    scale = d ** -0.5
    q = _stack_heads(q_ref, 0, gc)
    row = lax.broadcasted_iota(jnp.int32, (rows, 1), 0)
    slopes = [sl_ref[gc * h + g] for g in range(gc)]
    slope = _row_select(row, t, slopes)
    tok = _row_select(row, t, [row - g * t for g in range(gc)])
    tok1 = lax.broadcasted_iota(jnp.int32, (t, 1), 0)
    q0 = qi * t

    blk = lax.broadcasted_iota(jnp.int32, (1, nb), 1)
    bend_rel = (blk + 1) * CMP_BLOCK - 1 - q0
    mask_c = bend_rel <= tok
    s_c = _dot_nt(q, kc_ref[...].astype(BF16)) * scale + slope * bend_rel.astype(F32)
    p_c = jnp.where(mask_c, _softmax_rows(jnp.where(mask_c, s_c, NEG)), 0.0)
    o_c = _dot(p_c.astype(BF16), vc_ref[...].astype(BF16))

    imp = p_c[0:t]
    for g in range(1, gc):
        imp = imp + p_c[g * t:(g + 1) * t]
    cur = (q0 + tok1) >> SEL_SHIFT
    forced = (blk == 0) | (blk == cur) | (blk == cur - 1)
    valid = blk * SEL_BLOCK <= q0 + tok1
    score = jnp.where(valid, imp + jnp.where(forced, FORCE, 0.0), NEG)
    sel = _top_n_mask(score, nb, valid).astype(BF16)
    er = lax.broadcasted_iota(jnp.int32, (nb, seq), 0)
    ec = lax.broadcasted_iota(jnp.int32, (nb, seq), 1)
    expand = jnp.where(er == (ec >> SEL_SHIFT), 1.0, 0.0).astype(BF16)
    sel_keys = _dot(sel, expand)
    for kt in range(seq // tk):
        sx_ref[kt] = (sel_keys[:, kt * tk:(kt + 1) * tk] - 1.0) * (-NEG)

    m_ref[...] = jnp.full(m_ref.shape, NEG, F32)
    acc_ref[...] = jnp.zeros(acc_ref.shape, F32)

    def tile(ki, diag):
        k0 = pl.multiple_of(ki * tk, tk)
        k = ks_ref[pl.ds(k0, tk), :]
        v = _with_ones(vs_ref[pl.ds(k0, tk), :])
        rel = (k0 - q0) + lax.broadcasted_iota(jnp.int32, (1, tk), 1)
        relf = rel.astype(F32)
        madd = sx_ref[ki]
        if diag:
            madd = jnp.where(rel <= tok1, madd, NEG)
        s_all = _dot_nt(q, k) * scale
        for g in range(gc):
            s = s_all[g * t:(g + 1) * t] + (slopes[g] * relf + madd)
            _online_update(s, v, m_ref, acc_ref, g)

    def body(ki, carry):
        tile(ki, False)
        return carry

    n_full = (qi * t) // tk
    lax.fori_loop(0, n_full, body, 0)
    tile(n_full, True)

    n_slab = WINDOW + t
    start = pl.multiple_of(jnp.maximum(q0 - WINDOW, 0), t)
    kw = kw_ref[pl.ds(start, n_slab), :]
    vw = _with_ones(vw_ref[pl.ds(start, n_slab), :])
    rel_w = (start - q0) + lax.broadcasted_iota(jnp.int32, (1, n_slab), 1)
    delta = tok - rel_w
    mask_w = (delta >= 0) & (delta < WINDOW)
    s_w = _dot_nt(q, kw) * scale + slope * rel_w.astype(F32)
    o_w = _softmax_apply(jnp.where(mask_w, s_w, NEG), vw)

    gates = jax.nn.sigmoid(gt_ref[...])
    for g in range(gc):
        rs = slice(g * t, (g + 1) * t)
        o = (gates[:, 3 * g:3 * g + 1] * o_c[rs] + gates[:, 3 * g + 1:3 * g + 2] * _flash_out(acc_ref[g])
             + gates[:, 3 * g + 2:3 * g + 3] * o_w[rs])
        o_ref[:, g * d:(g + 1) * d] = o.astype(o_ref.dtype)


def _odd_prompt_attn(qkv16, k_cmp, v_cmp, gates, n_batch, seq):
    t = OP_T
    nq = seq // t
    d = HEAD_DIM
    gc = C_HEADS // C_KV
    assert seq % OP_TK == 0 and OP_TK % t == 0
    kv_spec = lambda col0: pl.BlockSpec((seq, d), lambda b, h, qi: (b, col0 + h))
    cmp_spec = pl.BlockSpec((None, None, seq // CMP_BLOCK, d), lambda b, h, qi: (b, h, 0, 0))
    return pl.pallas_call(
        functools.partial(_odd_prompt_body, seq=seq),
        grid=(n_batch, C_KV, nq),
        in_specs=[
            pl.BlockSpec(memory_space=pltpu.SMEM),
            pl.BlockSpec((t, gc * d), lambda b, h, qi: (b * nq + qi, h)),
            kv_spec(24), kv_spec(28), kv_spec(32), kv_spec(36),
            cmp_spec, cmp_spec,
            pl.BlockSpec((t, d), lambda b, h, qi: (b * nq + qi, h)),
        ],
        out_specs=pl.BlockSpec((t, gc * d), lambda b, h, qi: (b * nq + qi, h)),
        scratch_shapes=[
            pltpu.VMEM((seq // OP_TK, t, OP_TK), F32),
            pltpu.VMEM((gc, t, LANE), F32),
            pltpu.VMEM((gc, t, 2 * d), F32),
        ],
        out_shape=jax.ShapeDtypeStruct((n_batch * seq, C_HEADS * d), BF16),
        compiler_params=_cparams(("parallel", "parallel", "arbitrary")),
    )(_alibi_slopes(C_HEADS), qkv16, qkv16, qkv16, qkv16, qkv16, k_cmp, v_cmp, gates)


def _odd_sample_body(*refs, n_pages, past_len, dec):
    p_cnt = PAGES_PER_STEP
    pt_ref, sl_ref, qkv_ref, gt_ref, kc_ref, vc_ref, wk_ref, wv_ref = refs[:8]
    sk = refs[8:8 + p_cnt]
    sv = refs[8 + p_cnt:8 + 2 * p_cnt]
    o_ref, wko_ref, wvo_ref, qs_ref, sel_ref, oc_ref, m_ref, acc_ref = refs[8 + 2 * p_cnt:]
    del pt_ref
    c = pl.program_id(1)
    d = HEAD_DIM
    gc = C_HEADS // C_KV
    rows = gc * dec
    nb = past_len // CMP_BLOCK
    nbs = -(-(past_len + dec) // SEL_BLOCK)
    sel_w = 2 * LANE
    assert nb == LANE and nbs <= sel_w and (past_len + dec - 1) // SEL_BLOCK == past_len // SEL_BLOCK
    cur = past_len // SEL_BLOCK
    scale = d ** -0.5
    row = lax.broadcasted_iota(jnp.int32, (rows, 1), 0)
    tok = _row_select(row, dec, [row - g * dec for g in range(gc)])
    slopes = [_row_select(row, dec, [sl_ref[gc * h + g] for g in range(gc)]) for h in range(C_KV)]
    c_ks, c_vs, c_kw, c_vw = ((C_HEADS + i * C_KV) * d for i in (2, 3, 4, 5))

    @pl.when(c == 0)
    def _():
        blk = lax.broadcasted_iota(jnp.int32, (1, nb), 1)
        bend_rel = ((blk + 1) * CMP_BLOCK - 1 - past_len).astype(F32)
        lane = lax.broadcasted_iota(jnp.int32, (dec, sel_w), 1)
        forced = (lane == 0) | (lane == cur) | (lane == cur - 1)
        for h in range(C_KV):
            q = _stack_heads(qkv_ref, h * gc * d, gc).astype(BF16)
            qs_ref[h] = q
            p_c = _softmax_rows(_dot_nt(q, kc_ref[h].astype(BF16)) * scale + slopes[h] * bend_rel)
            oc_ref[h] = _dot(p_c.astype(BF16), vc_ref[h].astype(BF16))
            imp = p_c[0:dec]
            for g in range(1, gc):
                imp = imp + p_c[g * dec:(g + 1) * dec]
            imp = jnp.concatenate([imp, jnp.zeros((dec, sel_w - nb), F32)], axis=1)
            score = imp + jnp.where(forced, FORCE, 0.0)
            sel_ref[h] = _top_n_mask(score, nbs, lane < nbs)
        m_ref[...] = jnp.full(m_ref.shape, NEG, F32)
        acc_ref[...] = jnp.zeros(acc_ref.shape, F32)

    n_keys = p_cnt * LANE
    relf = (c * n_keys - past_len + lax.broadcasted_iota(jnp.int32, (1, n_keys), 1)).astype(F32)
    er = lax.broadcasted_iota(jnp.int32, (sel_w, n_keys), 0)
    ec = lax.broadcasted_iota(jnp.int32, (sel_w, n_keys), 1)
    expand = jnp.where(er - c * (n_keys // SEL_BLOCK) == (ec >> SEL_SHIFT), 1.0, 0.0).astype(BF16)
    for h in range(C_KV):
        k = jnp.concatenate([_head_rows(r, h, LANE) for r in sk], axis=0).astype(BF16)
        v = _with_ones(jnp.concatenate([_head_rows(r, h, LANE) for r in sv], axis=0).astype(BF16))
        keys = _dot(sel_ref[h].astype(BF16), expand)
        madd = jnp.concatenate([(keys - 1.0) * (-NEG)] * gc, axis=0)
        s = _dot_nt(qs_ref[h], k) * scale + (slopes[h] * relf + madd)
        _online_update(s, v, m_ref, acc_ref, h)

    @pl.when(c == n_pages // p_cnt - 1)
    def _():
        jn = lax.broadcasted_iota(jnp.int32, (1, LANE), 1)
        reln = jn.astype(F32)
        causal_n = (jn < dec) & (jn <= tok)
        pad = jnp.zeros((LANE - dec, d), F32)
        gates = jax.nn.sigmoid(gt_ref[...])
        n_win = WINDOW + LANE
        je = lax.broadcasted_iota(jnp.int32, (1, n_win), 1)
        rel_w = je - WINDOW
        delta = tok - rel_w
        mask_w = (delta >= 0) & (delta < WINDOW) & (je < WINDOW + dec)
        for h in range(C_KV):
            q = qs_ref[h]
            new = lambda col0: jnp.concatenate([qkv_ref[:, col0 + h * d:col0 + (h + 1) * d], pad], axis=0)
            picked = jnp.concatenate([sel_ref[h][:, cur:cur + 1]] * gc, axis=0) > 0.5
            s = _dot_nt(q, new(c_ks).astype(BF16)) * scale + slopes[h] * reln
            _online_update(jnp.where(causal_n & picked, s, NEG), _with_ones(new(c_vs).astype(BF16)), m_ref, acc_ref, h)
            o_s = _flash_out(acc_ref[h])
            kw = jnp.concatenate([_head_rows(wk_ref, h, WINDOW), new(c_kw)], axis=0).astype(BF16)
            vw = jnp.concatenate([_head_rows(wv_ref, h, WINDOW), new(c_vw)], axis=0).astype(BF16)
            s_w = _dot_nt(q, kw) * scale + slopes[h] * rel_w.astype(F32)
            o_w = _softmax_apply(jnp.where(mask_w, s_w, NEG), _with_ones(vw))
            o_c = oc_ref[h]
            for g in range(gc):
                rs = slice(g * dec, (g + 1) * dec)
                l0 = h * d + 3 * g
                o = (gates[:, l0:l0 + 1] * o_c[rs] + gates[:, l0 + 1:l0 + 2] * o_s[rs]
                     + gates[:, l0 + 2:l0 + 3] * o_w[rs])
                o_ref[:, (gc * h + g) * d:(gc * h + g + 1) * d] = o
            wko_ref[pl.ds((WINDOW - dec) * KV + h, dec, stride=KV), :] = qkv_ref[:, c_kw + h * d:c_kw + (h + 1) * d]
            wvo_ref[pl.ds((WINDOW - dec) * KV + h, dec, stride=KV), :] = qkv_ref[:, c_vw + h * d:c_vw + (h + 1) * d]
        wko_ref[0:(WINDOW - dec) * KV, :] = wk_ref[dec * KV:WINDOW * KV, :]
        wvo_ref[0:(WINDOW - dec) * KV, :] = wv_ref[dec * KV:WINDOW * KV, :]


def _odd_sample_attn(qkv32, gates, gate_block0, k_cmp, v_cmp, win_k, win_v, cache_k, cache_v, page_table, dec):
    n_seq, n_pages = page_table.shape
    past_len = n_pages * LANE
    p_cnt = PAGES_PER_STEP
    d = HEAD_DIM
    gc = C_HEADS // C_KV
    rows = gc * dec
    cmp_spec = pl.BlockSpec((None, C_KV, past_len // CMP_BLOCK, d), lambda s, c, pt: (s, 0, 0, 0))
    win_spec = pl.BlockSpec((None, WINDOW * KV, d), lambda s, c, pt: (s, 0, 0))
    in_specs = [
        pl.BlockSpec(memory_space=pltpu.SMEM),
        pl.BlockSpec((dec, qkv32.shape[1]), lambda s, c, pt: (s, 0)),
        pl.BlockSpec((dec, KV_COLS), lambda s, c, pt: (gate_block0 + s, 0)),
        cmp_spec, cmp_spec, win_spec, win_spec,
    ] + [_page_spec(p_cnt, i) for i in range(p_cnt)] * 2
    grid_spec = pltpu.PrefetchScalarGridSpec(
        num_scalar_prefetch=1,
        grid=(n_seq, n_pages // p_cnt),
        in_specs=in_specs,
        out_specs=(pl.BlockSpec((dec, C_HEADS * d), lambda s, c, pt: (s, 0)), win_spec, win_spec),
        scratch_shapes=[
            pltpu.VMEM((C_KV, rows, d), BF16),
            pltpu.VMEM((C_KV, dec, 2 * LANE), F32),
            pltpu.VMEM((C_KV, rows, d), F32),
            pltpu.VMEM((C_KV, rows, LANE), F32),
            pltpu.VMEM((C_KV, rows, 2 * d), F32),
        ],
    )
    win_shape = jax.ShapeDtypeStruct((n_seq, WINDOW * KV, d), F32)
    return pl.pallas_call(
        functools.partial(_odd_sample_body, n_pages=n_pages, past_len=past_len, dec=dec),
        grid_spec=grid_spec,
        out_shape=(jax.ShapeDtypeStruct((n_seq * dec, C_HEADS * d), F32), win_shape, win_shape),
        compiler_params=_cparams(("parallel", "arbitrary")),
    )(page_table, _alibi_slopes(C_HEADS), qkv32, gates, k_cmp, v_cmp, win_k, win_v,
      *([cache_k] * p_cnt), *([cache_v] * p_cnt))


def _xattn_body(q_ref, k_ref, v_ref, o_ref):
    d = HEAD_DIM
    scale = d ** -0.5
    tq = q_ref.shape[0]
    n_mem = k_ref.shape[0] // KV
    pad_rows = (-tq) % 16
    for h in range(X_HEADS):
        cols = slice(h * d, (h + 1) * d)
        q = q_ref[:, cols]
        if pad_rows:
            q = jnp.concatenate([q, jnp.zeros((pad_rows, d), F32)], axis=0)
        s = _dot_nt(q.astype(BF16), _head_rows(k_ref, h, n_mem).astype(BF16)) * scale
        o = _softmax_apply(s, _with_ones(_head_rows(v_ref, h, n_mem).astype(BF16)))
        o_ref[:, cols] = o[0:tq].astype(o_ref.dtype)


def _xattn(q, mem_k, mem_v, row_block0, tq, n_b, nq):
    width = X_HEADS * HEAD_DIM
    mem_spec = pl.BlockSpec((None, mem_k.shape[1], HEAD_DIM), lambda b, i: (b, 0, 0))
    return pl.pallas_call(
        _xattn_body,
        grid=(n_b, nq),
        in_specs=[pl.BlockSpec((tq, width), lambda b, i: (row_block0 + b * nq + i, 0)), mem_spec, mem_spec],
        out_specs=pl.BlockSpec((tq, width), lambda b, i: (b * nq + i, 0)),
        out_shape=jax.ShapeDtypeStruct((n_b * nq * tq, width), F32),
        compiler_params=_cparams(("parallel", "arbitrary")),
    )(q, mem_k, mem_v)


def _router_body(x_ref, g_ref, w_ref, b_ref, o_ref):
    x = x_ref[...]
    hn = x * lax.rsqrt(jnp.mean(x * x, axis=-1, keepdims=True) + EPS) * g_ref[...]
    logits = jnp.dot(hn, w_ref[...], preferred_element_type=F32, precision=lax.Precision.HIGHEST) + b_ref[...]
    lg = logits[:, 0:LANE]
    le = logits[:, LANE:2 * LANE]
    lane = lax.broadcasted_iota(jnp.int32, lg.shape, 1).astype(F32)
    big = float(2 * LANE)
    ninf = -jnp.inf

    def first_max(v):
        m = jnp.max(v, axis=-1, keepdims=True)
        return m, jnp.min(jnp.where(v == m, lane, big), axis=-1, keepdims=True)

    lgm = jnp.where(lane < N_GROUPS, lg, ninf)
    mg, gi = first_max(lgm)
    pg_top = 1.0 / jnp.sum(jnp.exp(lgm - mg), axis=-1, keepdims=True)
    lo = gi * EXP_PER_GROUP
    lem = jnp.where((lane >= lo) & (lane < lo + EXP_PER_GROUP), le, ninf)
    m1, i1 = first_max(lem)
    lem2 = jnp.where(lane == i1, ninf, lem)
    m2, i2 = first_max(lem2)
    z = jnp.sum(jnp.exp(lem - m1), axis=-1, keepdims=True)
    pe1 = 1.0 / z
    pe2 = jnp.exp(m2 - m1) / z
    w1 = pg_top * pe1 / (pe1 + pe2)
    w2 = pg_top * pe2 / (pe1 + pe2)
    o_ref[...] = jnp.where(lane == 0, i1, jnp.where(lane == 1, i2, jnp.where(lane == 2, w1, jnp.where(lane == 3, w2, 0.0))))


def _router(x, g, w_grp, b_grp, w_er, b_er, tm):
    n, d = x.shape
    w = jnp.zeros((d, 2 * LANE), F32).at[:, :N_GROUPS].set(w_grp).at[:, LANE:LANE + N_EXPERTS].set(w_er)
    b = jnp.zeros((1, 2 * LANE), F32).at[0, :N_GROUPS].set(b_grp).at[0, LANE:LANE + N_EXPERTS].set(b_er)
    return pl.pallas_call(
        _router_body,
        grid=(n // tm,),
        in_specs=[pl.BlockSpec((tm, d), lambda i: (i, 0)), pl.BlockSpec((1, d), lambda i: (0, 0)),
                  pl.BlockSpec((d, 2 * LANE), lambda i: (0, 0)), pl.BlockSpec((1, 2 * LANE), lambda i: (0, 0))],
        out_specs=pl.BlockSpec((tm, LANE), lambda i: (i, 0)),
        out_shape=jax.ShapeDtypeStruct((n, LANE), F32),
        compiler_params=_cparams(("parallel",)),
    )(x, g.reshape(1, d).astype(F32), w, b)


def _dispatch_tables(e1, e2, n_tiles):
    n = e1.shape[0]
    tm = MOE_TM
    e = jnp.concatenate([e1, e2])
    tok = jnp.concatenate([jnp.arange(n, dtype=jnp.int32)] * 2)
    onehot = (e[:, None] == jnp.arange(N_EXPERTS, dtype=jnp.int32)[None, :]).astype(jnp.int32)
    counts = jnp.sum(onehot, axis=0)
    rank = jnp.sum((jnp.cumsum(onehot, axis=0) - onehot) * onehot, axis=1)
    padded = ((counts + tm - 1) // tm) * tm
    ends = jnp.cumsum(padded)
    pos = jnp.sum((ends - padded)[None, :] * onehot, axis=1) + rank
    src = jnp.zeros((n_tiles * tm,), jnp.int32).at[pos].set(tok)
    n_used = (ends[-1] // tm).astype(jnp.int32)
    tile_start = jnp.arange(n_tiles, dtype=jnp.int32) * tm
    tile_e = jnp.sum((ends[None, :] <= tile_start[:, None]).astype(jnp.int32), axis=1)
    last_e = jnp.sum((ends <= (n_used - 1) * tm).astype(jnp.int32))
    tile_e = jnp.minimum(jnp.where(jnp.arange(n_tiles) < n_used, tile_e, last_e), N_EXPERTS - 1).astype(jnp.int32)
    return tile_e, n_used.reshape(1), src, pos[:n], pos[n:]


def _row_copy(idx_ref, i, r, src_hbm, dst, sem):
    return pltpu.make_async_copy(src_hbm.at[pl.ds(idx_ref[i], 1)], dst.at[pl.ds(r, 1)], sem)


def _start_rows(idx_ref, base, n, src_hbm, dst, sem):
    def start(r, carry):
        _row_copy(idx_ref, base + r, r, src_hbm, dst, sem).start()
        return carry
    lax.fori_loop(0, n, start, 0, unroll=8)


def _wait_rows(idx_ref, base, n, src_hbm, dst, sem):
    def wait(r, carry):
        _row_copy(idx_ref, base + r, r, src_hbm, dst, sem).wait()
        return carry
    lax.fori_loop(0, n, wait, 0, unroll=8)


def _experts_body(te_ref, nu_ref, src_ref, x_hbm, g_ref, wg_ref, wu_ref, wd_ref, y_ref,
                  xbuf, wgb, wub, wdb, sem):
    t = pl.program_id(0)
    n_used = nu_ref[0]
    slot = t % 2
    tm = MOE_TM

    @pl.when(t == 0)
    def _():
        _start_rows(src_ref, 0, tm, x_hbm, xbuf.at[0], sem.at[0])

    @pl.when(t + 1 < n_used)
    def _():
        _start_rows(src_ref, (t + 1) * tm, tm, x_hbm, xbuf.at[1 - slot], sem.at[1 - slot])

    @pl.when(t < n_used)
    def _():
        @pl.when((t == 0) | (te_ref[t] != te_ref[jnp.maximum(t - 1, 0)]))
        def _():
            wgb[...] = wg_ref[...].astype(BF16)
            wub[...] = wu_ref[...].astype(BF16)
            wdb[...] = wd_ref[...].astype(BF16)

        _wait_rows(src_ref, t * tm, tm, x_hbm, xbuf.at[slot], sem.at[slot])
        x = xbuf[slot]
        hn = (x * lax.rsqrt(jnp.mean(x * x, axis=-1, keepdims=True) + EPS) * g_ref[...]).astype(BF16)
        gate = _dot(hn, wgb[...])
        up = _dot(hn, wub[...])
        act = (gate * jax.nn.sigmoid(gate) * up).astype(BF16)
        y_ref[...] = _dot(act, wdb[...])

    @pl.when(t >= n_used)
    def _():
        y_ref[...] = jnp.zeros(y_ref.shape, F32)


def _experts(tile_e, n_used, src, x, g, w_gate, w_up, w_down):
    n_tiles = tile_e.shape[0]
    d = x.shape[1]
    f = w_gate.shape[2]
    tm = MOE_TM
    grid_spec = pltpu.PrefetchScalarGridSpec(
        num_scalar_prefetch=3,
        grid=(n_tiles,),
        in_specs=[
            pl.BlockSpec(memory_space=pl.ANY),
            pl.BlockSpec((1, d), lambda t, te, nu, sr: (0, 0)),
            pl.BlockSpec((None, d, f), lambda t, te, nu, sr: (te[t], 0, 0)),
            pl.BlockSpec((None, d, f), lambda t, te, nu, sr: (te[t], 0, 0)),
            pl.BlockSpec((None, f, d), lambda t, te, nu, sr: (te[t], 0, 0)),
        ],
        out_specs=pl.BlockSpec((tm, d), lambda t, te, nu, sr: (t, 0)),
        scratch_shapes=[pltpu.VMEM((2, tm, d), F32), pltpu.VMEM((d, f), BF16), pltpu.VMEM((d, f), BF16),
                        pltpu.VMEM((f, d), BF16), pltpu.SemaphoreType.DMA((2,))],
    )
    return pl.pallas_call(
        _experts_body,
        grid_spec=grid_spec,
        out_shape=jax.ShapeDtypeStruct((n_tiles * tm, d), F32),
        compiler_params=_cparams(("arbitrary",)),
    )(tile_e, n_used, src, x, g.reshape(1, d).astype(F32), w_gate, w_up, w_down)


def _combine_body(p1_ref, p2_ref, x_ref, r_ref, y_hbm, o_ref, ybuf, sem):
    i = pl.program_id(0)
    n = pl.num_programs(0)
    tm = x_ref.shape[0]
    slot = i % 2

    def start(step, sl):
        _start_rows(p1_ref, step * tm, tm, y_hbm, ybuf.at[sl, 0], sem.at[sl])
        _start_rows(p2_ref, step * tm, tm, y_hbm, ybuf.at[sl, 1], sem.at[sl])

    @pl.when(i == 0)
    def _():
        start(0, 0)

    @pl.when(i + 1 < n)
    def _():
        start(i + 1, 1 - slot)

    _wait_rows(p1_ref, i * tm, tm, y_hbm, ybuf.at[slot, 0], sem.at[slot])
    _wait_rows(p2_ref, i * tm, tm, y_hbm, ybuf.at[slot, 1], sem.at[slot])
    r = r_ref[...]
    o_ref[...] = x_ref[...] + r[:, 2:3] * ybuf[slot, 0] + r[:, 3:4] * ybuf[slot, 1]


def _combine(p1, p2, x, r, y, tm):
    n, d = x.shape
    grid_spec = pltpu.PrefetchScalarGridSpec(
        num_scalar_prefetch=2,
        grid=(n // tm,),
        in_specs=[
            pl.BlockSpec((tm, d), lambda i, a, b: (i, 0)),
            pl.BlockSpec((tm, LANE), lambda i, a, b: (i, 0)),
            pl.BlockSpec(memory_space=pl.ANY),
        ],
        out_specs=pl.BlockSpec((tm, d), lambda i, a, b: (i, 0)),
        scratch_shapes=[pltpu.VMEM((2, 2, tm, d), F32), pltpu.SemaphoreType.DMA((2,))],
    )
    return pl.pallas_call(
        _combine_body,
        grid_spec=grid_spec,
        out_shape=jax.ShapeDtypeStruct((n, d), F32),
        compiler_params=_cparams(("arbitrary",)),
    )(p1, p2, x, r, y)


def _moe(x, g, w_grp, b_grp, w_er, b_er, w_gate, w_up, w_down, tm):
    n = x.shape[0]
    r = _router(x, g, w_grp, b_grp, w_er, b_er, tm)
    e1 = r[:, 0].astype(jnp.int32)
    e2 = r[:, 1].astype(jnp.int32)
    n_tiles = -(-(2 * n + N_EXPERTS * (MOE_TM - 1)) // MOE_TM)
    tile_e, n_used, src, p1, p2 = _dispatch_tables(e1, e2, n_tiles)
    y = _experts(tile_e, n_used, src, x, g, w_gate, w_up, w_down)
    return _combine(p1, p2, x, r, y, MOE_TM)


def _rmsnorm_body(x_ref, g_ref, o_ref):
    x = x_ref[...]
    o_ref[...] = x * lax.rsqrt(jnp.mean(x * x, axis=-1, keepdims=True) + EPS) * g_ref[...]


def _rmsnorm(x, g, m, row_off, tm):
    d = x.shape[1]
    return pl.pallas_call(
        _rmsnorm_body,
        grid=(m // tm,),
        in_specs=[pl.BlockSpec((tm, d), lambda i: (i + row_off, 0)), pl.BlockSpec((1, d), lambda i: (0, 0))],
        out_specs=pl.BlockSpec((tm, d), lambda i: (i, 0)),
        out_shape=jax.ShapeDtypeStruct((m, d), F32),
        compiler_params=_cparams(("parallel",)),
    )(x, g.reshape(1, d).astype(F32))


def _row_tile(n, cands=(768, 512, 256, 128)):
    for tm in cands:
        if n % tm == 0:
            return tm
    raise ValueError(f"row count {n} has no supported row tile")


def kernel(x_prompt, x_sample, mem_prompt, cache_dif_k, cache_dif_v, cache_fox_k, cache_fox_v, cache_fox_logf, cache_cmp_k, cache_cmp_v, cache_slc_k, cache_slc_v, state_win_k, state_win_v, cache_mem_k, cache_mem_v, page_table, g_mix, w_in_e, b_forget, lam_q1, lam_k1, lam_q2, lam_k2, g_dif_head, w_out_e, w_in_o, cmp_pos_k, cmp_pos_v, w_out_o, g_xattn, g_mem, w_xq, w_xk, w_xv, w_xo, g_ffn, w_grp, b_grp, w_er, b_er, w_gate, w_up, w_down, g_final):
    n_b, seq, d = x_prompt.shape
    n_s, dec, _ = x_sample.shape
    n_mem = mem_prompt.shape[1]
    depth = g_mix.shape[0]
    n_pages = page_table.shape[1]
    past_len = n_pages * LANE
    n_p = n_b * seq
    n_smp = n_s * dec
    n_tok = n_p + n_smp
    hd = HEAD_DIM
    tmj = _row_tile(n_tok)
    tmp = _row_tile(n_p, (512, 256, 128))
    tms = _row_tile(n_smp, (256, 128))
    assert seq % EP_T == 0 and seq % 1024 == 0 and n_p % tms == 0 and n_pages % PAGES_PER_STEP == 0
    assert cache_dif_k.shape[2] == LANE and state_win_k.shape[2] == WINDOW and n_p % dec == 0

    x = jnp.concatenate([x_prompt.reshape(n_p, d), x_sample.reshape(n_smp, d)], axis=0)
    head_rows = lambda a: a.reshape(a.shape[0], a.shape[1] * KV, hd)
    kv5 = lambda a, lead: a.reshape(*lead, KV, hd)

    mem = mem_prompt.reshape(n_b * n_mem, d)
    mem_k_p, mem_v_p = [], []
    for l in range(depth):
        w_kv = jnp.concatenate([w_xk[l], w_xv[l]], axis=1).astype(BF16)
        mk, mv = _nmm(mem, w_kv, g=g_mem[l], out_dtypes=(), kv_cols=(0, 1),
                      tm=_row_tile(n_b * n_mem, (512, 256, 128)), tn=KV_COLS)
        mem_k_p.append(mk)
        mem_v_p.append(mv)

    even_p, even_s, odd_p, odd_s = [], [], [], []
    for l in range(depth):
        j = l // 2
        if l % 2 == 0:
            w_main = w_in_e[j][:, :E_MAIN].astype(BF16)
            w_fg = jnp.pad(w_in_e[j][:, E_MAIN:], ((0, 0), (0, LANE - F_HEADS))).astype(BF16)
            kvc = (2, 3, 6, 7)
            qkv16, *kv_p = _nmm(x, w_main, g=g_mix[l], out_dtypes=(BF16,), kv_cols=kvc, m=n_p, tm=tmp, tn=KV_COLS)
            qkv32, *kv_s = _nmm(x, w_main, g=g_mix[l], out_dtypes=(F32,), kv_cols=kvc, m=n_smp,
                                row_off=n_p // tms, tm=tms, tn=KV_COLS)
            logf = _nmm(x, w_fg, g=g_mix[l], bias=jnp.pad(b_forget[j], (0, LANE - F_HEADS)), tm=tmj, tn=LANE)
            logf = logf[:, :F_HEADS]
            lf_p = logf[:n_p].reshape(n_b, seq, F_HEADS)
            lf_s = logf[n_p:].reshape(n_s, dec, F_HEADS)
            ck_p = _cumsum(lf_p.transpose(0, 2, 1)).reshape(n_b, F_KV, F_HEADS // F_KV, seq)
            lf_past = cache_fox_logf[j].transpose(0, 2, 1)[page_table]
            lf_past = lf_past.transpose(0, 2, 1, 3).reshape(n_s, F_HEADS, past_len)
            lf_new = jnp.pad(lf_s.transpose(0, 2, 1), ((0, 0), (0, 0), (0, LANE - dec)))
            ck_s = _cumsum(jnp.concatenate([lf_past, lf_new], axis=2))
            lam_p = jnp.stack([lam_q1[j], lam_k1[j], lam_q2[j], lam_k2[j]]).astype(F32)
            lam_init = _lambda_init(l)
            o_p = _even_prompt_attn(qkv16, ck_p, lam_p, g_dif_head[j], n_b, seq, lam_init)
            caches = [head_rows(c[j]) for c in (cache_dif_k, cache_dif_v, cache_fox_k, cache_fox_v)]
            o_s = _even_sample_attn(qkv32, caches, page_table, ck_s, lam_p, g_dif_head[j], lam_init, dec)
            w_out = w_out_e[j][_even_out_perm()].astype(BF16)
            even_p.append([kv5(a, (n_b, seq)) for a in kv_p] + [lf_p])
            even_s.append([kv5(a, (n_s, dec)) for a in kv_s] + [lf_s])
        else:
            w_main = w_in_o[j][:, :O_MAIN].astype(BF16)
            w_g = w_in_o[j][:, O_MAIN:].reshape(d, C_KV, 3 * C_HEADS // C_KV)
            w_g = jnp.pad(w_g, ((0, 0), (0, 0), (0, LANE - w_g.shape[2]))).reshape(d, C_KV * LANE).astype(BF16)
            qkv16, *kv_p = _nmm(x, w_main, g=g_mix[l], out_dtypes=(BF16,), kv_cols=(4, 5, 6, 7, 8, 9),
                                m=n_p, tm=tmp, tn=KV_COLS)
            qkv32, *kv_s = _nmm(x, w_main, g=g_mix[l], out_dtypes=(F32,), kv_cols=(4, 5, 6, 7), m=n_smp,
                                row_off=n_p // tms, tm=tms, tn=KV_COLS)
            gates = _nmm(x, w_g, g=g_mix[l], tm=tmj, tn=KV_COLS)
            wk, wv = _pos_softmax(cmp_pos_k[j], cmp_pos_v[j])
            kc_p, vc_p = _compress_rows(qkv16, wk, wv, n_b, seq, 4, 5)
            kc_s, vc_s = _compress_paged(head_rows(cache_cmp_k[j]), head_rows(cache_cmp_v[j]), page_table, wk, wv)
            o_p = _odd_prompt_attn(qkv16, kc_p, vc_p, gates, n_b, seq)
            o_s, win_k, win_v = _odd_sample_attn(
                qkv32, gates, n_p // dec, kc_s, vc_s, head_rows(state_win_k[j]), head_rows(state_win_v[j]),
                head_rows(cache_slc_k[j]), head_rows(cache_slc_v[j]), page_table, dec)
            w_out = w_out_o[j].astype(BF16)
            n_keep = min(WINDOW, seq)
            rows_p = [kv5(a, (n_b, seq)) for a in kv_p]
            odd_p.append(rows_p[:4] + [a[:, seq - n_keep:] for a in rows_p[4:]])
            odd_s.append([kv5(a, (n_s, dec)) for a in kv_s] + [kv5(win_k, (n_s, WINDOW)), kv5(win_v, (n_s, WINDOW))])
        o = jnp.concatenate([o_p, o_s.astype(BF16)], axis=0)
        x = _nmm(o, w_out, res=x, tm=tmj, tn=512)

        q = _nmm(x, w_xq[l].astype(BF16), g=g_xattn[l], tm=tmj, tn=512)
        ox_p = _xattn(q, mem_k_p[l].reshape(n_b, n_mem * KV, hd), mem_v_p[l].reshape(n_b, n_mem * KV, hd),
                      0, 512, n_b, seq // 512)
        ox_s = _xattn(q, head_rows(cache_mem_k[l]), head_rows(cache_mem_v[l]), n_p // dec, dec, n_s, 1)
        x = _nmm(jnp.concatenate([ox_p, ox_s], axis=0), w_xo[l].astype(BF16), res=x, tm=tmj, tn=512)

        x = _moe(x, g_ffn[l], w_grp[l], b_grp[l], w_er[l], b_er[l], w_gate[l], w_up[l], w_down[l], tmj)

    y_prompt = _rmsnorm(x, g_final, n_p, 0, tmp).reshape(n_b, seq, d)
    y_sample = _rmsnorm(x, g_final, n_smp, n_p // tms, tms).reshape(n_s, dec, d)
    stack = lambda rows: [jnp.stack(a) for a in zip(*rows)]
    mem_k = jnp.stack([kv5(a, (n_b, n_mem)) for a in mem_k_p])
    mem_v = jnp.stack([kv5(a, (n_b, n_mem)) for a in mem_v_p])
    return (y_prompt, y_sample, *stack(even_p), *stack(odd_p), mem_k, mem_v, *stack(even_s), *stack(odd_s))
```

```python
import functools
import math

import numpy as np
import jax
import jax.numpy as jnp
from jax import lax
from jax.experimental import pallas as pl
from jax.experimental.pallas import tpu as pltpu

F32 = jnp.float32
BF16 = jnp.bfloat16

HEAD_DIM = 128
A_HEADS, A_KV = 8, 4
F_HEADS, F_KV = 8, 4
C_HEADS, C_KV = 16, 4
KV = 4
CMP_BLOCK = 64
SEL_BLOCK = 64
SEL_SHIFT = 6
TOP_N = 16
WINDOW = 512
X_HEADS = 4
N_GROUPS = 4
EXP_PER_GROUP = 8
N_EXPERTS = N_GROUPS * EXP_PER_GROUP
EPS = 1e-6
NEG = -1e30
FORCE = 1e4
E_MAIN = (A_HEADS + 2 * A_KV + F_HEADS + 2 * F_KV) * HEAD_DIM
O_MAIN = (C_HEADS + 6 * C_KV) * HEAD_DIM

LANE = 128
VMEM_LIMIT = 56 * 1024 * 1024
PAGES_PER_STEP = 16
MOE_TM = 256
KV_COLS = KV * HEAD_DIM


def _cparams(sem):
    return pltpu.CompilerParams(dimension_semantics=sem, vmem_limit_bytes=VMEM_LIMIT)


def _dot_nt(a, b):
    return lax.dot_general(a, b, (((1,), (1,)), ((), ())), preferred_element_type=F32)


def _dot(a, b):
    return jnp.dot(a, b, preferred_element_type=F32)


def _alibi_slopes(n_heads):
    return jnp.asarray(np.exp2(-8.0 * np.arange(1, n_heads + 1) / n_heads).astype(np.float32))


def _lambda_init(layer):
    return 0.8 - 0.6 * math.exp(-0.3 * layer)


def _head_rows(ref, h, n, row0=0):
    return ref[pl.ds(row0 * KV + h, n, stride=KV), :]


def _nmm_body(*refs, norm, epilogue, n_out, kv_cols):
    it = iter(refs)
    a_ref = next(it)
    g_ref = next(it) if norm else None
    b_ref = next(it)
    e_ref = next(it) if epilogue is not None else None
    outs = [next(it) for _ in range(n_out)]
    kv_refs = [next(it) for _ in kv_cols]
    as_ref = next(it)
    tm = a_ref.shape[0]

    @pl.when(pl.program_id(1) == 0)
    def _():
        a = a_ref[...].astype(F32)
        if norm:
            a = a * lax.rsqrt(jnp.mean(a * a, axis=-1, keepdims=True) + EPS) * g_ref[...]
        as_ref[...] = a.astype(BF16)

    acc = _dot(as_ref[...], b_ref[...])
    if epilogue == "res":
        acc = acc + e_ref[...]
    elif epilogue == "logsig":
        z = acc + e_ref[...]
        acc = jnp.minimum(z, 0.0) - jnp.log1p(jnp.exp(-jnp.abs(z)))
    for o in outs:
        o[...] = acc.astype(o.dtype)
    for ref, jcol in zip(kv_refs, kv_cols):
        @pl.when(pl.program_id(1) == jcol)
        def _(ref=ref):
            for h in range(KV):
                ref[pl.ds(h, tm, stride=KV), :] = acc[:, h * HEAD_DIM:(h + 1) * HEAD_DIM]


def _nmm(a, b, *, g=None, res=None, bias=None, out_dtypes=(F32,), kv_cols=(), m=None, row_off=0, tm, tn):
    k = a.shape[1]
    m = a.shape[0] if m is None else m
    n = b.shape[1]
    assert m % tm == 0 and n % tn == 0 and (not kv_cols or tn == KV_COLS)
    norm = g is not None
    epilogue = "res" if res is not None else ("logsig" if bias is not None else None)
    ins = [a]
    specs = [pl.BlockSpec((tm, k), lambda i, j: (i + row_off, 0))]
    if norm:
        ins.append(g.reshape(1, k).astype(F32))
        specs.append(pl.BlockSpec((1, k), lambda i, j: (0, 0)))
    ins.append(b)
    specs.append(pl.BlockSpec((k, tn), lambda i, j: (0, j)))
    if epilogue == "res":
        ins.append(res)
        specs.append(pl.BlockSpec((tm, tn), lambda i, j: (i + row_off, j)))
    elif epilogue == "logsig":
        ins.append(bias.reshape(1, n).astype(F32))
        specs.append(pl.BlockSpec((1, tn), lambda i, j: (0, j)))
    out_shape = [jax.ShapeDtypeStruct((m, n), dt) for dt in out_dtypes]
    out_specs = [pl.BlockSpec((tm, tn), lambda i, j: (i, j)) for _ in out_dtypes]
    out_shape += [jax.ShapeDtypeStruct((m * KV, HEAD_DIM), F32) for _ in kv_cols]
    out_specs += [pl.BlockSpec((tm * KV, HEAD_DIM), lambda i, j: (i, 0)) for _ in kv_cols]
    outs = pl.pallas_call(
        functools.partial(_nmm_body, norm=norm, epilogue=epilogue, n_out=len(out_dtypes), kv_cols=tuple(kv_cols)),
        grid=(m // tm, n // tn),
        in_specs=specs,
        out_specs=out_specs,
        out_shape=out_shape,
        scratch_shapes=[pltpu.VMEM((tm, k), BF16)],
        compiler_params=_cparams(("parallel", "arbitrary")),
    )(*ins)
    return outs[0] if len(outs) == 1 else outs


def _cumsum_body(x_ref, o_ref):
    nbk, h, l = x_ref.shape
    r = lax.broadcasted_iota(jnp.int32, (LANE, LANE), 0)
    c = lax.broadcasted_iota(jnp.int32, (LANE, LANE), 1)
    upper = (r <= c).astype(F32)
    carry = jnp.zeros((nbk * h, 1), F32)
    for j in range(l // LANE):
        blk = x_ref[:, :, j * LANE:(j + 1) * LANE].reshape(nbk * h, LANE)
        cs = jnp.dot(blk, upper, preferred_element_type=F32, precision=lax.Precision.HIGHEST)
        o_ref[:, :, j * LANE:(j + 1) * LANE] = (cs + carry).reshape(nbk, h, LANE)
        carry = carry + cs[:, LANE - 1:LANE]


def _cumsum(x):
    nb, h, l = x.shape
    nbk = math.gcd(nb, 8)
    return pl.pallas_call(
        _cumsum_body,
        grid=(nb // nbk,),
        in_specs=[pl.BlockSpec((nbk, h, l), lambda i: (i, 0, 0))],
        out_specs=pl.BlockSpec((nbk, h, l), lambda i: (i, 0, 0)),
        out_shape=jax.ShapeDtypeStruct((nb, h, l), F32),
        compiler_params=_cparams(("parallel",)),
    )(x)


def _with_ones(v):
    return jnp.concatenate([v, jnp.ones_like(v)], axis=1)


def _online_update(s, v_ext, m_ref, acc_ref, idx):
    m_prev = m_ref[idx]
    m_new = jnp.maximum(m_prev, jnp.max(s, axis=-1, keepdims=True))
    alpha = jnp.exp(m_prev - m_new)
    p = jnp.exp(s - jnp.tile(m_new, (1, s.shape[1] // LANE)))
    acc_ref[idx] = jnp.tile(alpha, (1, 2)) * acc_ref[idx] + _dot(p.astype(BF16), v_ext)
    m_ref[idx] = m_new


def _flash_out(acc):
    d = acc.shape[1] // 2
    return acc[:, :d] / acc[:, d:]


def _softmax_rows(s):
    e = jnp.exp(s - jnp.max(s, axis=-1, keepdims=True))
    return e / jnp.sum(e, axis=-1, keepdims=True)


def _softmax_apply(s, v_ext):
    e = jnp.exp(s - jnp.max(s, axis=-1, keepdims=True))
    return _flash_out(_dot(e.astype(BF16), v_ext))


def _lambda_value(lp_ref, lam_init):
    lp = lp_ref[...]
    a = jnp.sum(lp[0:1] * lp[1:2], axis=-1, keepdims=True)
    b = jnp.sum(lp[2:3] * lp[3:4], axis=-1, keepdims=True)
    return jnp.exp(a) - jnp.exp(b) + lam_init


def _dif_head_out(acc1, acc2, lam, gh, lam_init):
    o = _flash_out(acc1) - lam * _flash_out(acc2)
    o = o * lax.rsqrt(jnp.mean(o * o, axis=-1, keepdims=True) + EPS) * gh
    return o * (1.0 - lam_init)


def _row_select(row, step, vals):
    out = vals[-1]
    for g in range(len(vals) - 2, -1, -1):
        out = jnp.where(row < (g + 1) * step, vals[g], out)
    return out


EP_T = 512


def _even_prompt_body(sl_ref, qa_ref, ka_ref, va_ref, qf_ref, kf_ref, vf_ref, ck_ref, lp_ref, gh_ref,
                      o_ref, qs_ref, m_ref, acc_ref, *, lam_init):
    h = pl.program_id(1)
    qi = pl.program_id(2)
    t = EP_T
    d = HEAD_DIM
    lane = lax.broadcasted_iota(jnp.int32, (t, d), 1)
    zero = jnp.zeros((t, d), BF16)
    for g in range(2):
        q = qa_ref[:, g * d:(g + 1) * d] * jnp.asarray(0.125, BF16)
        qs_ref[g, 0] = jnp.where(lane < d // 2, q, zero)
        qs_ref[g, 1] = jnp.where(lane >= d // 2, q, zero)
        qs_ref[g, 2] = qf_ref[:, g * d:(g + 1) * d]
    m_ref[...] = jnp.full(m_ref.shape, NEG, F32)
    acc_ref[...] = jnp.zeros(acc_ref.shape, F32)
    scale_f = d ** -0.5

    def tile(ki, diag):
        k0 = pl.multiple_of(ki * t, t)
        ka = ka_ref[pl.ds(k0, t), :]
        va = _with_ones(va_ref[pl.ds(k0, t), :])
        kf = kf_ref[pl.ds(k0, t), :]
        vf = _with_ones(vf_ref[pl.ds(k0, t), :])
        rel = (k0 - qi * t) + lax.broadcasted_iota(jnp.int32, (1, t), 1)
        relf = rel.astype(F32)
        if diag:
            mask = rel <= lax.broadcasted_iota(jnp.int32, (t, 1), 0)
        for g in range(2):
            bias_a = sl_ref[2 * h + g] * relf
            bias_f = -ck_ref[g:g + 1, pl.ds(k0, t)]
            for kind in range(3):
                if kind < 2:
                    s = _dot_nt(qs_ref[g, kind], ka) + bias_a
                    v = va
                else:
                    s = _dot_nt(qs_ref[g, 2], kf) * scale_f + bias_f
                    v = vf
                if diag:
                    s = jnp.where(mask, s, NEG)
                _online_update(s, v, m_ref, acc_ref, (g, kind))

    def body(ki, carry):
        tile(ki, False)
        return carry

    lax.fori_loop(0, qi, body, 0)
    tile(qi, True)

    lam = _lambda_value(lp_ref, lam_init)
    gh = gh_ref[...]
    for g in range(2):
        od = _dif_head_out(acc_ref[g, 0], acc_ref[g, 1], lam, gh, lam_init)
        of = _flash_out(acc_ref[g, 2])
        o_ref[:, g * d:(g + 1) * d] = od.astype(o_ref.dtype)
        o_ref[:, (2 + g) * d:(3 + g) * d] = of.astype(o_ref.dtype)


def _even_prompt_attn(qkv16, ck, lam_p, g_head, n_batch, seq, lam_init):
    t = EP_T
    nq = seq // t
    d = HEAD_DIM
    kv_spec = lambda col0: pl.BlockSpec((seq, d), lambda b, h, qi: (b, col0 + h))
    q_spec = lambda col0: pl.BlockSpec((t, 2 * d), lambda b, h, qi: (b * nq + qi, col0 + h))
    return pl.pallas_call(
        functools.partial(_even_prompt_body, lam_init=lam_init),
        grid=(n_batch, A_KV, nq),
        in_specs=[
            pl.BlockSpec(memory_space=pltpu.SMEM),
            q_spec(0),
            kv_spec(8), kv_spec(12),
            q_spec(8),
            kv_spec(24), kv_spec(28),
            pl.BlockSpec((None, None, 2, seq), lambda b, h, qi: (b, h, 0, 0)),
            pl.BlockSpec((4, d // 2), lambda b, h, qi: (0, 0)),
            pl.BlockSpec((1, d), lambda b, h, qi: (0, 0)),
        ],
        out_specs=pl.BlockSpec((t, 4 * d), lambda b, h, qi: (b * nq + qi, h)),
        scratch_shapes=[
            pltpu.VMEM((2, 3, t, d), BF16),
            pltpu.VMEM((2, 3, t, LANE), F32),
            pltpu.VMEM((2, 3, t, 2 * d), F32),
        ],
        out_shape=jax.ShapeDtypeStruct((n_batch * seq, 4 * A_KV * d), BF16),
        compiler_params=_cparams(("parallel", "parallel", "arbitrary")),
    )(_alibi_slopes(A_HEADS), qkv16, qkv16, qkv16, qkv16, qkv16, qkv16, ck, lam_p, g_head.reshape(1, d))


def _even_sample_body(*refs, n_pages, past_len, dec, lam_init):
    p_cnt = PAGES_PER_STEP
    pt_ref, sl_ref, qkv_ref, ck_ref, ckn_ref, lp_ref, gh_ref = refs[:7]
    pages = refs[7:7 + 4 * p_cnt]
    dk, dv, fk, fv = (pages[i * p_cnt:(i + 1) * p_cnt] for i in range(4))
    o_ref, qs_ref, m_ref, acc_ref = refs[7 + 4 * p_cnt:]
    del pt_ref
    c = pl.program_id(1)
    d = HEAD_DIM
    rows = 2 * dec
    row = lax.broadcasted_iota(jnp.int32, (rows, 1), 0)
    first = row < dec
    scale_f = d ** -0.5
    lane = lax.broadcasted_iota(jnp.int32, (rows, d), 1)

    def stacked(col0):
        return jnp.concatenate([qkv_ref[:, col0:col0 + d], qkv_ref[:, col0 + d:col0 + 2 * d]], axis=0)

    @pl.when(c == 0)
    def _():
        for h in range(A_KV):
            q = (stacked(h * 2 * d) * 0.125).astype(BF16)
            zero = jnp.zeros_like(q)
            qs_ref[h, 0] = jnp.where(lane < d // 2, q, zero)
            qs_ref[h, 1] = jnp.where(lane >= d // 2, q, zero)
            qs_ref[h, 2] = stacked(2048 + h * 2 * d).astype(BF16)
        m_ref[...] = jnp.full(m_ref.shape, NEG, F32)
        acc_ref[...] = jnp.zeros(acc_ref.shape, F32)

    def head_keys(prefs, h):
        return jnp.concatenate([_head_rows(r, h, LANE) for r in prefs], axis=0).astype(BF16)

    n_keys = p_cnt * LANE
    relf = (c * n_keys - past_len + lax.broadcasted_iota(jnp.int32, (1, n_keys), 1)).astype(F32)

    def slope_col(h):
        return jnp.where(first, sl_ref[2 * h], sl_ref[2 * h + 1])

    def ck_rows(ref, h):
        return jnp.where(first, ref[2 * h:2 * h + 1, :], ref[2 * h + 1:2 * h + 2, :])

    for h in range(A_KV):
        ka, kf = head_keys(dk, h), head_keys(fk, h)
        va, vf = _with_ones(head_keys(dv, h)), _with_ones(head_keys(fv, h))
        bias_a = slope_col(h) * relf
        bias_f = -ck_rows(ck_ref, h)
        for kind in range(3):
            if kind < 2:
                s = _dot_nt(qs_ref[h, kind], ka) + bias_a
                v = va
            else:
                s = _dot_nt(qs_ref[h, 2], kf) * scale_f + bias_f
                v = vf
            _online_update(s, v, m_ref, acc_ref, (h, kind))

    @pl.when(c == n_pages // p_cnt - 1)
    def _():
        jn = lax.broadcasted_iota(jnp.int32, (1, LANE), 1)
        tok = jnp.where(first, row, row - dec)
        mask_n = (jn < dec) & (jn <= tok)
        reln = jn.astype(F32)
        pad = jnp.zeros((LANE - dec, d), F32)
        lam = _lambda_value(lp_ref, lam_init)
        gh = gh_ref[...]
        for h in range(A_KV):
            def new_rows(col0):
                return jnp.concatenate([qkv_ref[:, col0 + h * d:col0 + (h + 1) * d], pad], axis=0).astype(BF16)
            ka, kf = new_rows(1024), new_rows(3072)
            va, vf = _with_ones(new_rows(1536)), _with_ones(new_rows(3584))
            bias_a = slope_col(h) * reln
            bias_f = -ck_rows(ckn_ref, h)
            for kind in range(3):
                if kind < 2:
                    s = _dot_nt(qs_ref[h, kind], ka) + bias_a
                    v = va
                else:
                    s = _dot_nt(qs_ref[h, 2], kf) * scale_f + bias_f
                    v = vf
                s = jnp.where(mask_n, s, NEG)
                _online_update(s, v, m_ref, acc_ref, (h, kind))
            od = _dif_head_out(acc_ref[h, 0], acc_ref[h, 1], lam, gh, lam_init)
            of = _flash_out(acc_ref[h, 2])
            for g in range(2):
                o_ref[:, (4 * h + g) * d:(4 * h + g + 1) * d] = od[g * dec:(g + 1) * dec]
                o_ref[:, (4 * h + 2 + g) * d:(4 * h + 3 + g) * d] = of[g * dec:(g + 1) * dec]


def _page_spec(p_cnt, i):
    return pl.BlockSpec((None, LANE * KV, HEAD_DIM), lambda s, c, pt: (pt[s, c * p_cnt + i], 0, 0))


def _even_sample_attn(qkv32, caches, page_table, ck, lam_p, g_head, lam_init, dec):
    n_seq, n_pages = page_table.shape
    past_len = n_pages * LANE
    p_cnt = PAGES_PER_STEP
    d = HEAD_DIM
    n_chunks = n_pages // p_cnt
    in_specs = [
        pl.BlockSpec(memory_space=pltpu.SMEM),
        pl.BlockSpec((dec, qkv32.shape[1]), lambda s, c, pt: (s, 0)),
        pl.BlockSpec((None, 8, p_cnt * LANE), lambda s, c, pt: (s, 0, c)),
        pl.BlockSpec((None, 8, LANE), lambda s, c, pt: (s, 0, n_pages)),
        pl.BlockSpec((4, d // 2), lambda s, c, pt: (0, 0)),
        pl.BlockSpec((1, d), lambda s, c, pt: (0, 0)),
    ]
    ins = [_alibi_slopes(A_HEADS), qkv32, ck, ck, lam_p, g_head.reshape(1, d)]
    for arr in caches:
        for i in range(p_cnt):
            in_specs.append(_page_spec(p_cnt, i))
            ins.append(arr)
    rows = 2 * dec
    grid_spec = pltpu.PrefetchScalarGridSpec(
        num_scalar_prefetch=1,
        grid=(n_seq, n_chunks),
        in_specs=in_specs,
        out_specs=pl.BlockSpec((dec, 16 * d), lambda s, c, pt: (s, 0)),
        scratch_shapes=[
            pltpu.VMEM((A_KV, 3, rows, d), BF16),
            pltpu.VMEM((A_KV, 3, rows, LANE), F32),
            pltpu.VMEM((A_KV, 3, rows, 2 * d), F32),
        ],
    )
    return pl.pallas_call(
        functools.partial(_even_sample_body, n_pages=n_pages, past_len=past_len, dec=dec, lam_init=lam_init),
        grid_spec=grid_spec,
        out_shape=jax.ShapeDtypeStruct((n_seq * dec, 16 * d), F32),
        compiler_params=_cparams(("parallel", "arbitrary")),
    )(page_table, *ins)


def _even_out_perm():
    idx = []
    for h in range(A_KV):
        for base in (0, A_HEADS):
            for g in range(2):
                head = base + 2 * h + g
                idx.extend(range(head * HEAD_DIM, (head + 1) * HEAD_DIM))
    return np.asarray(idx, np.int32)


def _pos_softmax_body(pk_ref, pv_ref, wk_ref, wv_ref):
    for src, dst in ((pk_ref, wk_ref), (pv_ref, wv_ref)):
        x = src[...]
        e = jnp.exp(x - jnp.max(x, axis=1, keepdims=True))
        w = e / jnp.sum(e, axis=1, keepdims=True)
        dst[...] = jnp.concatenate([w, w], axis=1)


def _pos_softmax(pos_k, pos_v):
    expand = lambda p: jnp.broadcast_to(p.astype(F32).T[:, :, None], (C_KV, CMP_BLOCK, HEAD_DIM))
    shape = jax.ShapeDtypeStruct((C_KV, 2 * CMP_BLOCK, HEAD_DIM), F32)
    return pl.pallas_call(_pos_softmax_body, out_shape=(shape, shape))(expand(pos_k), expand(pos_v))


def _block_sums(x, w):
    r, width = x.shape
    xw = x.reshape(r // LANE, LANE, width) * w[None]
    return jnp.sum(xw.reshape(r // CMP_BLOCK, CMP_BLOCK, width), axis=1)


def _compress_rows_body(k_ref, v_ref, wk_ref, wv_ref, ok_ref, ov_ref):
    d = HEAD_DIM
    for h in range(C_KV):
        ok_ref[h] = _block_sums(k_ref[:, h * d:(h + 1) * d].astype(F32), wk_ref[h])
        ov_ref[h] = _block_sums(v_ref[:, h * d:(h + 1) * d].astype(F32), wv_ref[h])


def _compress_rows(qkv16, wk, wv, n_batch, seq, col_k, col_v):
    tr = 1024
    nc = seq // tr
    d = HEAD_DIM
    shape = jax.ShapeDtypeStruct((n_batch, C_KV, seq // CMP_BLOCK, d), F32)
    w_spec = pl.BlockSpec((C_KV, 2 * CMP_BLOCK, d), lambda b, i: (0, 0, 0))
    o_spec = pl.BlockSpec((None, C_KV, tr // CMP_BLOCK, d), lambda b, i: (b, 0, i, 0))
    return pl.pallas_call(
        _compress_rows_body,
        grid=(n_batch, nc),
        in_specs=[pl.BlockSpec((tr, KV_COLS), lambda b, i: (b * nc + i, col_k)),
                  pl.BlockSpec((tr, KV_COLS), lambda b, i: (b * nc + i, col_v)), w_spec, w_spec],
        out_specs=(o_spec, o_spec),
        out_shape=(shape, shape),
        compiler_params=_cparams(("parallel", "parallel")),
    )(qkv16, qkv16, wk, wv)


def _compress_paged_body(*refs):
    p_cnt = PAGES_PER_STEP
    kp = refs[1:1 + p_cnt]
    vp = refs[1 + p_cnt:1 + 2 * p_cnt]
    wk_ref, wv_ref, ok_ref, ov_ref = refs[1 + 2 * p_cnt:]
    for h in range(C_KV):
        wk, wv = wk_ref[h], wv_ref[h]
        ok_ref[h] = jnp.concatenate([_block_sums(_head_rows(r, h, LANE), wk) for r in kp], axis=0)
        ov_ref[h] = jnp.concatenate([_block_sums(_head_rows(r, h, LANE), wv) for r in vp], axis=0)


def _compress_paged(cache_k, cache_v, page_table, wk, wv):
    n_seq, n_pages = page_table.shape
    p_cnt = PAGES_PER_STEP
    d = HEAD_DIM
    per_step = p_cnt * LANE // CMP_BLOCK
    w_spec = pl.BlockSpec((C_KV, 2 * CMP_BLOCK, d), lambda s, c, pt: (0, 0, 0))
    o_spec = pl.BlockSpec((None, C_KV, per_step, d), lambda s, c, pt: (s, 0, c, 0))
    shape = jax.ShapeDtypeStruct((n_seq, C_KV, n_pages * LANE // CMP_BLOCK, d), F32)
    grid_spec = pltpu.PrefetchScalarGridSpec(
        num_scalar_prefetch=1,
        grid=(n_seq, n_pages // p_cnt),
        in_specs=[_page_spec(p_cnt, i) for i in range(p_cnt)] * 2 + [w_spec, w_spec],
        out_specs=(o_spec, o_spec),
    )
    return pl.pallas_call(
        _compress_paged_body,
        grid_spec=grid_spec,
        out_shape=(shape, shape),
        compiler_params=_cparams(("parallel", "arbitrary")),
    )(page_table, *([cache_k] * p_cnt), *([cache_v] * p_cnt), wk, wv)


def _top_n_mask(score, n_blocks, valid):
    lane = lax.broadcasted_iota(jnp.int32, score.shape, 1)
    rank = jnp.zeros(score.shape, F32)
    for i in range(n_blocks):
        col = score[:, i:i + 1]
        ahead = (col > score) | ((col == score) & (lane > i))
        rank = rank + jnp.where(ahead, 1.0, 0.0)
    keep = (rank < float(TOP_N)) & (lane < n_blocks) & valid
    return jnp.where(keep, 1.0, 0.0)


def _top_n_mask_cols(score, valid):
    blk = lax.broadcasted_iota(jnp.int32, score.shape, 0)
    rank = jnp.zeros(score.shape, F32)
    for i in range(score.shape[0]):
        row = score[i:i + 1, :]
        ahead = (row > score) | ((row == score) & (blk > i))
        rank = rank + jnp.where(ahead, 1.0, 0.0)
    return jnp.where((rank < float(TOP_N)) & valid, 1.0, 0.0)


def _eye(n, dtype):
    r = lax.broadcasted_iota(jnp.int32, (n, n), 0)
    c = lax.broadcasted_iota(jnp.int32, (n, n), 1)
    return jnp.where(r == c, 1.0, 0.0).astype(dtype)


def _stack_heads(ref, col0, n):
    return jnp.concatenate([ref[:, col0 + g * HEAD_DIM:col0 + (g + 1) * HEAD_DIM] for g in range(n)], axis=0)


OP_T = 128
OP_TK = 512


def _odd_prompt_body(sl_ref, q_ref, ks_ref, vs_ref, kw_ref, vw_ref, kc_ref, vc_ref, gt_ref,
                     o_ref, sx_ref, m_ref, acc_ref, *, seq):
    h = pl.program_id(1)
    qi = pl.program_id(2)
    t = OP_T
    tk = OP_TK
    d = HEAD_DIM
    gc = C_HEADS // C_KV
    rows = gc * t
    nb = seq // CMP_BLOCK
    scale = d ** -0.5
    q = _stack_heads(q_ref, 0, gc)
    row = lax.broadcasted_iota(jnp.int32, (rows, 1), 0)
    slopes = [sl_ref[gc * h + g] for g in range(gc)]
    slope = _row_select(row, t, slopes)
    tok = _row_select(row, t, [row - g * t for g in range(gc)])
    tok1 = lax.broadcasted_iota(jnp.int32, (t, 1), 0)
    q0 = qi * t

    blk = lax.broadcasted_iota(jnp.int32, (1, nb), 1)
    bend_rel = (blk + 1) * CMP_BLOCK - 1 - q0
    mask_c = bend_rel <= tok
    s_c = _dot_nt(q, kc_ref[...].astype(BF16)) * scale + slope * bend_rel.astype(F32)
    p_c = jnp.where(mask_c, _softmax_rows(jnp.where(mask_c, s_c, NEG)), 0.0)
    o_c = _dot(p_c.astype(BF16), vc_ref[...].astype(BF16))

    imp = p_c[0:t]
    for g in range(1, gc):
        imp = imp + p_c[g * t:(g + 1) * t]
    eye_nb = _eye(nb, F32)
    imp_t = lax.dot_general(eye_nb, imp, (((1,), (1,)), ((), ())), preferred_element_type=F32,
                            precision=lax.Precision.HIGHEST)
    blk_c = lax.broadcasted_iota(jnp.int32, (nb, 1), 0)
    pos_r = q0 + lax.broadcasted_iota(jnp.int32, (1, t), 1)
    cur = pos_r >> SEL_SHIFT
    forced = (blk_c == 0) | (blk_c == cur) | (blk_c == cur - 1)
    valid = blk_c * SEL_BLOCK <= pos_r
    score_t = jnp.where(valid, imp_t + jnp.where(forced, FORCE, 0.0), NEG)
    sel_t = _top_n_mask_cols(score_t, valid).astype(BF16)
    sel = _dot_nt(_eye(t, BF16), sel_t).astype(BF16)
    er = lax.broadcasted_iota(jnp.int32, (nb, seq), 0)
    ec = lax.broadcasted_iota(jnp.int32, (nb, seq), 1)
    expand = jnp.where(er == (ec >> SEL_SHIFT), 1.0, 0.0).astype(BF16)
    sel_keys = _dot(sel, expand)
    for kt in range(seq // tk):
        sx_ref[kt] = (sel_keys[:, kt * tk:(kt + 1) * tk] - 1.0) * (-NEG)

    m_ref[...] = jnp.full(m_ref.shape, NEG, F32)
    acc_ref[...] = jnp.zeros(acc_ref.shape, F32)

    def tile(ki, diag):
        k0 = pl.multiple_of(ki * tk, tk)
        k = ks_ref[pl.ds(k0, tk), :]
        v = _with_ones(vs_ref[pl.ds(k0, tk), :])
        rel = (k0 - q0) + lax.broadcasted_iota(jnp.int32, (1, tk), 1)
        relf = rel.astype(F32)
        madd = sx_ref[ki]
        if diag:
            madd = jnp.where(rel <= tok1, madd, NEG)
        s_all = _dot_nt(q, k) * scale
        for g in range(gc):
            s = s_all[g * t:(g + 1) * t] + (slopes[g] * relf + madd)
            _online_update(s, v, m_ref, acc_ref, g)

    def body(ki, carry):
        tile(ki, False)
        return carry

    n_full = (qi * t) // tk
    lax.fori_loop(0, n_full, body, 0)
    tile(n_full, True)

    n_slab = WINDOW + t
    start = pl.multiple_of(jnp.maximum(q0 - WINDOW, 0), t)
    kw = kw_ref[pl.ds(start, n_slab), :]
    vw = _with_ones(vw_ref[pl.ds(start, n_slab), :])
    rel_w = (start - q0) + lax.broadcasted_iota(jnp.int32, (1, n_slab), 1)
    delta = tok - rel_w
    mask_w = (delta >= 0) & (delta < WINDOW)
    s_w = _dot_nt(q, kw) * scale + slope * rel_w.astype(F32)
    o_w = _softmax_apply(jnp.where(mask_w, s_w, NEG), vw)

    gates = jax.nn.sigmoid(gt_ref[...])
    for g in range(gc):
        rs = slice(g * t, (g + 1) * t)
        o = (gates[:, 3 * g:3 * g + 1] * o_c[rs] + gates[:, 3 * g + 1:3 * g + 2] * _flash_out(acc_ref[g])
             + gates[:, 3 * g + 2:3 * g + 3] * o_w[rs])
        o_ref[:, g * d:(g + 1) * d] = o.astype(o_ref.dtype)


def _odd_prompt_attn(qkv16, k_cmp, v_cmp, gates, n_batch, seq):
    t = OP_T
    nq = seq // t
    d = HEAD_DIM
    gc = C_HEADS // C_KV
    assert seq % OP_TK == 0 and OP_TK % t == 0
    kv_spec = lambda col0: pl.BlockSpec((seq, d), lambda b, h, qi: (b, col0 + h))
    cmp_spec = pl.BlockSpec((None, None, seq // CMP_BLOCK, d), lambda b, h, qi: (b, h, 0, 0))
    return pl.pallas_call(
        functools.partial(_odd_prompt_body, seq=seq),
        grid=(n_batch, C_KV, nq),
        in_specs=[
            pl.BlockSpec(memory_space=pltpu.SMEM),
            pl.BlockSpec((t, gc * d), lambda b, h, qi: (b * nq + qi, h)),
            kv_spec(24), kv_spec(28), kv_spec(32), kv_spec(36),
            cmp_spec, cmp_spec,
            pl.BlockSpec((t, d), lambda b, h, qi: (b * nq + qi, h)),
        ],
        out_specs=pl.BlockSpec((t, gc * d), lambda b, h, qi: (b * nq + qi, h)),
        scratch_shapes=[
            pltpu.VMEM((seq // OP_TK, t, OP_TK), F32),
            pltpu.VMEM((gc, t, LANE), F32),
            pltpu.VMEM((gc, t, 2 * d), F32),
        ],
        out_shape=jax.ShapeDtypeStruct((n_batch * seq, C_HEADS * d), BF16),
        compiler_params=_cparams(("parallel", "parallel", "arbitrary")),
    )(_alibi_slopes(C_HEADS), qkv16, qkv16, qkv16, qkv16, qkv16, k_cmp, v_cmp, gates)


def _odd_sample_body(*refs, n_pages, past_len, dec):
    p_cnt = PAGES_PER_STEP
    pt_ref, sl_ref, qkv_ref, gt_ref, kc_ref, vc_ref, wk_ref, wv_ref = refs[:8]
    sk = refs[8:8 + p_cnt]
    sv = refs[8 + p_cnt:8 + 2 * p_cnt]
    o_ref, wko_ref, wvo_ref, qs_ref, sel_ref, oc_ref, m_ref, acc_ref = refs[8 + 2 * p_cnt:]
    del pt_ref
    c = pl.program_id(1)
    d = HEAD_DIM
    gc = C_HEADS // C_KV
    rows = gc * dec
    nb = past_len // CMP_BLOCK
    nbs = -(-(past_len + dec) // SEL_BLOCK)
    sel_w = 2 * LANE
    assert nb == LANE and nbs <= sel_w and (past_len + dec - 1) // SEL_BLOCK == past_len // SEL_BLOCK
    cur = past_len // SEL_BLOCK
    scale = d ** -0.5
    row = lax.broadcasted_iota(jnp.int32, (rows, 1), 0)
    tok = _row_select(row, dec, [row - g * dec for g in range(gc)])
    slopes = [_row_select(row, dec, [sl_ref[gc * h + g] for g in range(gc)]) for h in range(C_KV)]
    c_ks, c_vs, c_kw, c_vw = ((C_HEADS + i * C_KV) * d for i in (2, 3, 4, 5))

    @pl.when(c == 0)
    def _():
        blk = lax.broadcasted_iota(jnp.int32, (1, nb), 1)
        bend_rel = ((blk + 1) * CMP_BLOCK - 1 - past_len).astype(F32)
        lane = lax.broadcasted_iota(jnp.int32, (dec, sel_w), 1)
        forced = (lane == 0) | (lane == cur) | (lane == cur - 1)
        for h in range(C_KV):
            q = _stack_heads(qkv_ref, h * gc * d, gc).astype(BF16)
            qs_ref[h] = q
            p_c = _softmax_rows(_dot_nt(q, kc_ref[h].astype(BF16)) * scale + slopes[h] * bend_rel)
            oc_ref[h] = _dot(p_c.astype(BF16), vc_ref[h].astype(BF16))
            imp = p_c[0:dec]
            for g in range(1, gc):
                imp = imp + p_c[g * dec:(g + 1) * dec]
            imp = jnp.concatenate([imp, jnp.zeros((dec, sel_w - nb), F32)], axis=1)
            score = imp + jnp.where(forced, FORCE, 0.0)
            sel_ref[h] = _top_n_mask(score, nbs, lane < nbs)
        m_ref[...] = jnp.full(m_ref.shape, NEG, F32)
        acc_ref[...] = jnp.zeros(acc_ref.shape, F32)

    n_keys = p_cnt * LANE
    relf = (c * n_keys - past_len + lax.broadcasted_iota(jnp.int32, (1, n_keys), 1)).astype(F32)
    er = lax.broadcasted_iota(jnp.int32, (sel_w, n_keys), 0)
    ec = lax.broadcasted_iota(jnp.int32, (sel_w, n_keys), 1)
    expand = jnp.where(er - c * (n_keys // SEL_BLOCK) == (ec >> SEL_SHIFT), 1.0, 0.0).astype(BF16)
    for h in range(C_KV):
        k = jnp.concatenate([_head_rows(r, h, LANE) for r in sk], axis=0).astype(BF16)
        v = _with_ones(jnp.concatenate([_head_rows(r, h, LANE) for r in sv], axis=0).astype(BF16))
        keys = _dot(sel_ref[h].astype(BF16), expand)
        madd = jnp.concatenate([(keys - 1.0) * (-NEG)] * gc, axis=0)
        s = _dot_nt(qs_ref[h], k) * scale + (slopes[h] * relf + madd)
        _online_update(s, v, m_ref, acc_ref, h)

    @pl.when(c == n_pages // p_cnt - 1)
    def _():
        jn = lax.broadcasted_iota(jnp.int32, (1, LANE), 1)
        reln = jn.astype(F32)
        causal_n = (jn < dec) & (jn <= tok)
        pad = jnp.zeros((LANE - dec, d), F32)
        gates = jax.nn.sigmoid(gt_ref[...])
        n_win = WINDOW + LANE
        je = lax.broadcasted_iota(jnp.int32, (1, n_win), 1)
        rel_w = je - WINDOW
        delta = tok - rel_w
        mask_w = (delta >= 0) & (delta < WINDOW) & (je < WINDOW + dec)
        for h in range(C_KV):
            q = qs_ref[h]
            new = lambda col0: jnp.concatenate([qkv_ref[:, col0 + h * d:col0 + (h + 1) * d], pad], axis=0)
            picked = jnp.concatenate([sel_ref[h][:, cur:cur + 1]] * gc, axis=0) > 0.5
            s = _dot_nt(q, new(c_ks).astype(BF16)) * scale + slopes[h] * reln
            _online_update(jnp.where(causal_n & picked, s, NEG), _with_ones(new(c_vs).astype(BF16)), m_ref, acc_ref, h)
            o_s = _flash_out(acc_ref[h])
            kw = jnp.concatenate([_head_rows(wk_ref, h, WINDOW), new(c_kw)], axis=0).astype(BF16)
            vw = jnp.concatenate([_head_rows(wv_ref, h, WINDOW), new(c_vw)], axis=0).astype(BF16)
            s_w = _dot_nt(q, kw) * scale + slopes[h] * rel_w.astype(F32)
            o_w = _softmax_apply(jnp.where(mask_w, s_w, NEG), _with_ones(vw))
            o_c = oc_ref[h]
            for g in range(gc):
                rs = slice(g * dec, (g + 1) * dec)
                l0 = h * d + 3 * g
                o = (gates[:, l0:l0 + 1] * o_c[rs] + gates[:, l0 + 1:l0 + 2] * o_s[rs]
                     + gates[:, l0 + 2:l0 + 3] * o_w[rs])
                o_ref[:, (gc * h + g) * d:(gc * h + g + 1) * d] = o
            wko_ref[pl.ds((WINDOW - dec) * KV + h, dec, stride=KV), :] = qkv_ref[:, c_kw + h * d:c_kw + (h + 1) * d]
            wvo_ref[pl.ds((WINDOW - dec) * KV + h, dec, stride=KV), :] = qkv_ref[:, c_vw + h * d:c_vw + (h + 1) * d]
        wko_ref[0:(WINDOW - dec) * KV, :] = wk_ref[dec * KV:WINDOW * KV, :]
        wvo_ref[0:(WINDOW - dec) * KV, :] = wv_ref[dec * KV:WINDOW * KV, :]


def _odd_sample_attn(qkv32, gates, gate_block0, k_cmp, v_cmp, win_k, win_v, cache_k, cache_v, page_table, dec):
    n_seq, n_pages = page_table.shape
    past_len = n_pages * LANE
    p_cnt = PAGES_PER_STEP
    d = HEAD_DIM
    gc = C_HEADS // C_KV
    rows = gc * dec
    cmp_spec = pl.BlockSpec((None, C_KV, past_len // CMP_BLOCK, d), lambda s, c, pt: (s, 0, 0, 0))
    win_spec = pl.BlockSpec((None, WINDOW * KV, d), lambda s, c, pt: (s, 0, 0))
    in_specs = [
        pl.BlockSpec(memory_space=pltpu.SMEM),
        pl.BlockSpec((dec, qkv32.shape[1]), lambda s, c, pt: (s, 0)),
        pl.BlockSpec((dec, KV_COLS), lambda s, c, pt: (gate_block0 + s, 0)),
        cmp_spec, cmp_spec, win_spec, win_spec,
    ] + [_page_spec(p_cnt, i) for i in range(p_cnt)] * 2
    grid_spec = pltpu.PrefetchScalarGridSpec(
        num_scalar_prefetch=1,
        grid=(n_seq, n_pages // p_cnt),
        in_specs=in_specs,
        out_specs=(pl.BlockSpec((dec, C_HEADS * d), lambda s, c, pt: (s, 0)), win_spec, win_spec),
        scratch_shapes=[
            pltpu.VMEM((C_KV, rows, d), BF16),
            pltpu.VMEM((C_KV, dec, 2 * LANE), F32),
            pltpu.VMEM((C_KV, rows, d), F32),
            pltpu.VMEM((C_KV, rows, LANE), F32),
            pltpu.VMEM((C_KV, rows, 2 * d), F32),
        ],
    )
    win_shape = jax.ShapeDtypeStruct((n_seq, WINDOW * KV, d), F32)
    return pl.pallas_call(
        functools.partial(_odd_sample_body, n_pages=n_pages, past_len=past_len, dec=dec),
        grid_spec=grid_spec,
        out_shape=(jax.ShapeDtypeStruct((n_seq * dec, C_HEADS * d), F32), win_shape, win_shape),
        compiler_params=_cparams(("parallel", "arbitrary")),
    )(page_table, _alibi_slopes(C_HEADS), qkv32, gates, k_cmp, v_cmp, win_k, win_v,
      *([cache_k] * p_cnt), *([cache_v] * p_cnt))


def _xattn_body(q_ref, k_ref, v_ref, o_ref):
    d = HEAD_DIM
    scale = d ** -0.5
    tq = q_ref.shape[0]
    n_mem = k_ref.shape[0] // KV
    pad_rows = (-tq) % 16
    for h in range(X_HEADS):
        cols = slice(h * d, (h + 1) * d)
        q = q_ref[:, cols]
        if pad_rows:
            q = jnp.concatenate([q, jnp.zeros((pad_rows, d), F32)], axis=0)
        s = _dot_nt(q.astype(BF16), _head_rows(k_ref, h, n_mem).astype(BF16)) * scale
        o = _softmax_apply(s, _with_ones(_head_rows(v_ref, h, n_mem).astype(BF16)))
        o_ref[:, cols] = o[0:tq].astype(o_ref.dtype)


def _xattn(q, mem_k, mem_v, layer, row_block0, tq, n_b, nq):
    width = X_HEADS * HEAD_DIM
    mem_spec = pl.BlockSpec((None, None, mem_k.shape[2], HEAD_DIM), lambda b, i: (layer, b, 0, 0))
    return pl.pallas_call(
        _xattn_body,
        grid=(n_b, nq),
        in_specs=[pl.BlockSpec((tq, width), lambda b, i: (row_block0 + b * nq + i, 0)), mem_spec, mem_spec],
        out_specs=pl.BlockSpec((tq, width), lambda b, i: (b * nq + i, 0)),
        out_shape=jax.ShapeDtypeStruct((n_b * nq * tq, width), F32),
        compiler_params=_cparams(("parallel", "arbitrary")),
    )(q, mem_k, mem_v)


def _router_body(x_ref, g_ref, w_ref, b_ref, o_ref):
    x = x_ref[...]
    hn = x * lax.rsqrt(jnp.mean(x * x, axis=-1, keepdims=True) + EPS) * g_ref[...]
    logits = jnp.dot(hn, w_ref[...], preferred_element_type=F32, precision=lax.Precision.HIGHEST) + b_ref[...]
    lg = logits[:, 0:LANE]
    le = logits[:, LANE:2 * LANE]
    lane = lax.broadcasted_iota(jnp.int32, lg.shape, 1).astype(F32)
    big = float(2 * LANE)
    ninf = -jnp.inf

    def first_max(v):
        m = jnp.max(v, axis=-1, keepdims=True)
        return m, jnp.min(jnp.where(v == m, lane, big), axis=-1, keepdims=True)

    lgm = jnp.where(lane < N_GROUPS, lg, ninf)
    mg, gi = first_max(lgm)
    pg_top = 1.0 / jnp.sum(jnp.exp(lgm - mg), axis=-1, keepdims=True)
    lo = gi * EXP_PER_GROUP
    lem = jnp.where((lane >= lo) & (lane < lo + EXP_PER_GROUP), le, ninf)
    m1, i1 = first_max(lem)
    lem2 = jnp.where(lane == i1, ninf, lem)
    m2, i2 = first_max(lem2)
    z = jnp.sum(jnp.exp(lem - m1), axis=-1, keepdims=True)
    pe1 = 1.0 / z
    pe2 = jnp.exp(m2 - m1) / z
    w1 = pg_top * pe1 / (pe1 + pe2)
    w2 = pg_top * pe2 / (pe1 + pe2)
    o_ref[...] = jnp.where(lane == 0, i1, jnp.where(lane == 1, i2, jnp.where(lane == 2, w1, jnp.where(lane == 3, w2, 0.0))))


def _router(x, g, w_grp, b_grp, w_er, b_er, tm):
    n, d = x.shape
    w = jnp.zeros((d, 2 * LANE), F32).at[:, :N_GROUPS].set(w_grp).at[:, LANE:LANE + N_EXPERTS].set(w_er)
    b = jnp.zeros((1, 2 * LANE), F32).at[0, :N_GROUPS].set(b_grp).at[0, LANE:LANE + N_EXPERTS].set(b_er)
    return pl.pallas_call(
        _router_body,
        grid=(n // tm,),
        in_specs=[pl.BlockSpec((tm, d), lambda i: (i, 0)), pl.BlockSpec((1, d), lambda i: (0, 0)),
                  pl.BlockSpec((d, 2 * LANE), lambda i: (0, 0)), pl.BlockSpec((1, 2 * LANE), lambda i: (0, 0))],
        out_specs=pl.BlockSpec((tm, LANE), lambda i: (i, 0)),
        out_shape=jax.ShapeDtypeStruct((n, LANE), F32),
        compiler_params=_cparams(("parallel",)),
    )(x, g.reshape(1, d).astype(F32), w, b)


def _dispatch_tables(e1, e2, n_tiles):
    n = e1.shape[0]
    tm = MOE_TM
    e = jnp.concatenate([e1, e2])
    tok = jnp.concatenate([jnp.arange(n, dtype=jnp.int32)] * 2)
    onehot = (e[:, None] == jnp.arange(N_EXPERTS, dtype=jnp.int32)[None, :]).astype(jnp.int32)
    counts = jnp.sum(onehot, axis=0)
    rank = jnp.sum((jnp.cumsum(onehot, axis=0) - onehot) * onehot, axis=1)
    padded = ((counts + tm - 1) // tm) * tm
    ends = jnp.cumsum(padded)
    pos = jnp.sum((ends - padded)[None, :] * onehot, axis=1) + rank
    src = jnp.zeros((n_tiles * tm,), jnp.int32).at[pos].set(tok)
    return (padded // tm).astype(jnp.int32), ((ends - padded) // tm).astype(jnp.int32), src, pos[:n], pos[n:]


def _row_copy(idx_ref, i, r, src_hbm, dst, sem):
    return pltpu.make_async_copy(src_hbm.at[pl.ds(idx_ref[i], 1)], dst.at[pl.ds(r, 1)], sem)


def _start_rows(idx_ref, base, n, src_hbm, dst, sem):
    def start(r, carry):
        _row_copy(idx_ref, base + r, r, src_hbm, dst, sem).start()
        return carry
    lax.fori_loop(0, n, start, 0, unroll=8)


def _wait_rows(idx_ref, base, n, src_hbm, dst, sem):
    def wait(r, carry):
        _row_copy(idx_ref, base + r, r, src_hbm, dst, sem).wait()
        return carry
    lax.fori_loop(0, n, wait, 0, unroll=8)


def _experts_body(cnt_ref, off_ref, src_ref, x_hbm, g_ref, wg_ref, wu_ref, wd_ref, y_hbm,
                  xbuf, ybuf, wgb, wub, wdb, gsem, osem, *, n_tiles):
    e = pl.program_id(0)
    n = cnt_ref[e]
    base = off_ref[e]
    tm = MOE_TM

    def gather_start(i, slot):
        _start_rows(src_ref, (base + i) * tm, tm, x_hbm, xbuf.at[slot], gsem.at[slot])

    def out_copy(i, slot):
        return pltpu.make_async_copy(ybuf.at[slot], y_hbm.at[pl.ds((base + i) * tm, tm)], osem.at[slot])

    @pl.when(n > 0)
    def _():
        gather_start(0, 0)
        wgb[...] = wg_ref[...].astype(BF16)
        wub[...] = wu_ref[...].astype(BF16)
        wdb[...] = wd_ref[...].astype(BF16)

        def tile(i, carry):
            slot = i % 2

            @pl.when(i + 1 < n)
            def _():
                gather_start(i + 1, 1 - slot)

            _wait_rows(src_ref, (base + i) * tm, tm, x_hbm, xbuf.at[slot], gsem.at[slot])

            @pl.when(i >= 2)
            def _():
                out_copy(i - 2, slot).wait()

            x = xbuf[slot]
            hn = (x * lax.rsqrt(jnp.mean(x * x, axis=-1, keepdims=True) + EPS) * g_ref[...]).astype(BF16)
            gate = _dot(hn, wgb[...])
            up = _dot(hn, wub[...])
            act = (gate * jax.nn.sigmoid(gate) * up).astype(BF16)
            ybuf[slot] = _dot(act, wdb[...])
            out_copy(i, slot).start()
            return carry

        lax.fori_loop(0, n, tile, 0)

        @pl.when(n >= 2)
        def _():
            out_copy(n - 2, n % 2).wait()

        out_copy(n - 1, (n - 1) % 2).wait()

    @pl.when(e == pl.num_programs(0) - 1)
    def _():
        ybuf[0] = jnp.zeros(ybuf.shape[1:], F32)

        def fill(t, carry):
            cp = pltpu.make_async_copy(ybuf.at[0], y_hbm.at[pl.ds(t * tm, tm)], osem.at[0])
            cp.start()
            cp.wait()
            return carry

        lax.fori_loop(base + n, n_tiles, fill, 0)


def _experts(cnt, off, src, x, g, w_gate, w_up, w_down, layer):
    n_rows = src.shape[0]
    d = x.shape[1]
    f = w_gate.shape[3]
    tm = MOE_TM
    w_in_spec = pl.BlockSpec((None, None, d, f), lambda e, c, o, sr: (layer, e, 0, 0))
    grid_spec = pltpu.PrefetchScalarGridSpec(
        num_scalar_prefetch=3,
        grid=(N_EXPERTS,),
        in_specs=[
            pl.BlockSpec(memory_space=pl.ANY),
            pl.BlockSpec((1, d), lambda e, c, o, sr: (0, 0)),
            w_in_spec, w_in_spec,
            pl.BlockSpec((None, None, f, d), lambda e, c, o, sr: (layer, e, 0, 0)),
        ],
        out_specs=pl.BlockSpec(memory_space=pl.ANY),
        scratch_shapes=[pltpu.VMEM((2, tm, d), F32), pltpu.VMEM((2, tm, d), F32),
                        pltpu.VMEM((d, f), BF16), pltpu.VMEM((d, f), BF16), pltpu.VMEM((f, d), BF16),
                        pltpu.SemaphoreType.DMA((2,)), pltpu.SemaphoreType.DMA((2,))],
    )
    return pl.pallas_call(
        functools.partial(_experts_body, n_tiles=n_rows // tm),
        grid_spec=grid_spec,
        out_shape=jax.ShapeDtypeStruct((n_rows, d), F32),
        compiler_params=_cparams(("arbitrary",)),
    )(cnt, off, src, x, g.reshape(1, d).astype(F32), w_gate, w_up, w_down)


def _combine_body(p1_ref, p2_ref, x_ref, r_ref, y_hbm, o_ref, ybuf, sem):
    i = pl.program_id(0)
    n = pl.num_programs(0)
    tm = x_ref.shape[0]
    slot = i % 2

    def start(step, sl):
        _start_rows(p1_ref, step * tm, tm, y_hbm, ybuf.at[sl, 0], sem.at[sl])
        _start_rows(p2_ref, step * tm, tm, y_hbm, ybuf.at[sl, 1], sem.at[sl])

    @pl.when(i == 0)
    def _():
        start(0, 0)

    @pl.when(i + 1 < n)
    def _():
        start(i + 1, 1 - slot)

    _wait_rows(p1_ref, i * tm, tm, y_hbm, ybuf.at[slot, 0], sem.at[slot])
    _wait_rows(p2_ref, i * tm, tm, y_hbm, ybuf.at[slot, 1], sem.at[slot])
    r = r_ref[...]
    o_ref[...] = x_ref[...] + r[:, 2:3] * ybuf[slot, 0] + r[:, 3:4] * ybuf[slot, 1]


def _combine(p1, p2, x, r, y, tm):
    n, d = x.shape
    grid_spec = pltpu.PrefetchScalarGridSpec(
        num_scalar_prefetch=2,
        grid=(n // tm,),
        in_specs=[
            pl.BlockSpec((tm, d), lambda i, a, b: (i, 0)),
            pl.BlockSpec((tm, LANE), lambda i, a, b: (i, 0)),
            pl.BlockSpec(memory_space=pl.ANY),
        ],
        out_specs=pl.BlockSpec((tm, d), lambda i, a, b: (i, 0)),
        scratch_shapes=[pltpu.VMEM((2, 2, tm, d), F32), pltpu.SemaphoreType.DMA((2,))],
    )
    return pl.pallas_call(
        _combine_body,
        grid_spec=grid_spec,
        out_shape=jax.ShapeDtypeStruct((n, d), F32),
        compiler_params=_cparams(("arbitrary",)),
    )(p1, p2, x, r, y)


def _moe(x, g, w_grp, b_grp, w_er, b_er, w_gate, w_up, w_down, layer, tm):
    n = x.shape[0]
    r = _router(x, g, w_grp, b_grp, w_er, b_er, tm)
    e1 = r[:, 0].astype(jnp.int32)
    e2 = r[:, 1].astype(jnp.int32)
    n_tiles = -(-(2 * n + N_EXPERTS * (MOE_TM - 1)) // MOE_TM)
    cnt, off, src, p1, p2 = _dispatch_tables(e1, e2, n_tiles)
    y = _experts(cnt, off, src, x, g, w_gate, w_up, w_down, layer)
    return _combine(p1, p2, x, r, y, MOE_TM)


def _rmsnorm_body(x_ref, g_ref, o_ref):
    x = x_ref[...]
    o_ref[...] = x * lax.rsqrt(jnp.mean(x * x, axis=-1, keepdims=True) + EPS) * g_ref[...]


def _rmsnorm(x, g, m, row_off, tm):
    d = x.shape[1]
    return pl.pallas_call(
        _rmsnorm_body,
        grid=(m // tm,),
        in_specs=[pl.BlockSpec((tm, d), lambda i: (i + row_off, 0)), pl.BlockSpec((1, d), lambda i: (0, 0))],
        out_specs=pl.BlockSpec((tm, d), lambda i: (i, 0)),
        out_shape=jax.ShapeDtypeStruct((m, d), F32),
        compiler_params=_cparams(("parallel",)),
    )(x, g.reshape(1, d).astype(F32))


def _row_tile(n, cands=(768, 512, 256, 128)):
    for tm in cands:
        if n % tm == 0:
            return tm
    raise ValueError(f"row count {n} has no supported row tile")


def kernel(x_prompt, x_sample, mem_prompt, cache_dif_k, cache_dif_v, cache_fox_k, cache_fox_v, cache_fox_logf, cache_cmp_k, cache_cmp_v, cache_slc_k, cache_slc_v, state_win_k, state_win_v, cache_mem_k, cache_mem_v, page_table, g_mix, w_in_e, b_forget, lam_q1, lam_k1, lam_q2, lam_k2, g_dif_head, w_out_e, w_in_o, cmp_pos_k, cmp_pos_v, w_out_o, g_xattn, g_mem, w_xq, w_xk, w_xv, w_xo, g_ffn, w_grp, b_grp, w_er, b_er, w_gate, w_up, w_down, g_final):
    n_b, seq, d = x_prompt.shape
    n_s, dec, _ = x_sample.shape
    n_mem = mem_prompt.shape[1]
    depth = g_mix.shape[0]
    n_pages = page_table.shape[1]
    past_len = n_pages * LANE
    n_p = n_b * seq
    n_smp = n_s * dec
    n_tok = n_p + n_smp
    hd = HEAD_DIM
    tmj = _row_tile(n_tok)
    tmp = _row_tile(n_p, (512, 256, 128))
    tms = _row_tile(n_smp, (256, 128))
    assert seq % EP_T == 0 and seq % 1024 == 0 and n_p % tms == 0 and n_pages % PAGES_PER_STEP == 0
    assert cache_dif_k.shape[2] == LANE and state_win_k.shape[2] == WINDOW and n_p % dec == 0

    x = jnp.concatenate([x_prompt.reshape(n_p, d), x_sample.reshape(n_smp, d)], axis=0)
    head_rows = lambda a: a.reshape(a.shape[0], a.shape[1] * KV, hd)
    kv5 = lambda a, lead: a.reshape(*lead, KV, hd)

    mem_k_s = cache_mem_k.reshape(depth, n_s, n_mem * KV, hd)
    mem_v_s = cache_mem_v.reshape(depth, n_s, n_mem * KV, hd)

    mem = mem_prompt.reshape(n_b * n_mem, d)
    mem_k_p, mem_v_p = [], []
    for l in range(depth):
        w_kv = jnp.concatenate([w_xk[l], w_xv[l]], axis=1).astype(BF16)
        mk, mv = _nmm(mem, w_kv, g=g_mem[l], out_dtypes=(), kv_cols=(0, 1),
                      tm=_row_tile(n_b * n_mem, (512, 256, 128)), tn=KV_COLS)
        mem_k_p.append(mk)
        mem_v_p.append(mv)

    even_p, even_s, odd_p, odd_s = [], [], [], []
    for l in range(depth):
        j = l // 2
        if l % 2 == 0:
            w_main = w_in_e[j][:, :E_MAIN].astype(BF16)
            w_fg = jnp.pad(w_in_e[j][:, E_MAIN:], ((0, 0), (0, LANE - F_HEADS))).astype(BF16)
            kvc = (2, 3, 6, 7)
            qkv16, *kv_p = _nmm(x, w_main, g=g_mix[l], out_dtypes=(BF16,), kv_cols=kvc, m=n_p, tm=tmp, tn=KV_COLS)
            qkv32, *kv_s = _nmm(x, w_main, g=g_mix[l], out_dtypes=(F32,), kv_cols=kvc, m=n_smp,
                                row_off=n_p // tms, tm=tms, tn=KV_COLS)
            logf = _nmm(x, w_fg, g=g_mix[l], bias=jnp.pad(b_forget[j], (0, LANE - F_HEADS)), tm=tmj, tn=LANE)
            logf = logf[:, :F_HEADS]
            lf_p = logf[:n_p].reshape(n_b, seq, F_HEADS)
            lf_s = logf[n_p:].reshape(n_s, dec, F_HEADS)
            ck_p = _cumsum(lf_p.transpose(0, 2, 1)).reshape(n_b, F_KV, F_HEADS // F_KV, seq)
            lf_past = cache_fox_logf[j].transpose(0, 2, 1)[page_table]
            lf_past = lf_past.transpose(0, 2, 1, 3).reshape(n_s, F_HEADS, past_len)
            lf_new = jnp.pad(lf_s.transpose(0, 2, 1), ((0, 0), (0, 0), (0, LANE - dec)))
            ck_s = _cumsum(jnp.concatenate([lf_past, lf_new], axis=2))
            lam_p = jnp.stack([lam_q1[j], lam_k1[j], lam_q2[j], lam_k2[j]]).astype(F32)
            lam_init = _lambda_init(l)
            o_p = _even_prompt_attn(qkv16, ck_p, lam_p, g_dif_head[j], n_b, seq, lam_init)
            caches = [head_rows(c[j]) for c in (cache_dif_k, cache_dif_v, cache_fox_k, cache_fox_v)]
            o_s = _even_sample_attn(qkv32, caches, page_table, ck_s, lam_p, g_dif_head[j], lam_init, dec)
            w_out = w_out_e[j][_even_out_perm()].astype(BF16)
            even_p.append([kv5(a, (n_b, seq)) for a in kv_p] + [lf_p])
            even_s.append([kv5(a, (n_s, dec)) for a in kv_s] + [lf_s])
        else:
            w_main = w_in_o[j][:, :O_MAIN].astype(BF16)
            w_g = w_in_o[j][:, O_MAIN:].reshape(d, C_KV, 3 * C_HEADS // C_KV)
            w_g = jnp.pad(w_g, ((0, 0), (0, 0), (0, LANE - w_g.shape[2]))).reshape(d, C_KV * LANE).astype(BF16)
            qkv16, *kv_p = _nmm(x, w_main, g=g_mix[l], out_dtypes=(BF16,), kv_cols=(4, 5, 6, 7, 8, 9),
                                m=n_p, tm=tmp, tn=KV_COLS)
            qkv32, *kv_s = _nmm(x, w_main, g=g_mix[l], out_dtypes=(F32,), kv_cols=(4, 5, 6, 7), m=n_smp,
                                row_off=n_p // tms, tm=tms, tn=KV_COLS)
            gates = _nmm(x, w_g, g=g_mix[l], tm=tmj, tn=KV_COLS)
            wk, wv = _pos_softmax(cmp_pos_k[j], cmp_pos_v[j])
            kc_p, vc_p = _compress_rows(qkv16, wk, wv, n_b, seq, 4, 5)
            kc_s, vc_s = _compress_paged(head_rows(cache_cmp_k[j]), head_rows(cache_cmp_v[j]), page_table, wk, wv)
            o_p = _odd_prompt_attn(qkv16, kc_p, vc_p, gates, n_b, seq)
            o_s, win_k, win_v = _odd_sample_attn(
                qkv32, gates, n_p // dec, kc_s, vc_s, head_rows(state_win_k[j]), head_rows(state_win_v[j]),
                head_rows(cache_slc_k[j]), head_rows(cache_slc_v[j]), page_table, dec)
            w_out = w_out_o[j].astype(BF16)
            n_keep = min(WINDOW, seq)
            rows_p = [kv5(a, (n_b, seq)) for a in kv_p]
            odd_p.append(rows_p[:4] + [a[:, seq - n_keep:] for a in rows_p[4:]])
            odd_s.append([kv5(a, (n_s, dec)) for a in kv_s] + [kv5(win_k, (n_s, WINDOW)), kv5(win_v, (n_s, WINDOW))])
        o = jnp.concatenate([o_p, o_s.astype(BF16)], axis=0)
        x = _nmm(o, w_out, res=x, tm=tmj, tn=512)

        q = _nmm(x, w_xq[l].astype(BF16), g=g_xattn[l], tm=tmj, tn=512)
        ox_p = _xattn(q, mem_k_p[l].reshape(1, n_b, n_mem * KV, hd), mem_v_p[l].reshape(1, n_b, n_mem * KV, hd),
                      0, 0, 512, n_b, seq // 512)
        ox_s = _xattn(q, mem_k_s, mem_v_s, l, n_p // dec, dec, n_s, 1)
        x = _nmm(jnp.concatenate([ox_p, ox_s], axis=0), w_xo[l].astype(BF16), res=x, tm=tmj, tn=512)

        x = _moe(x, g_ffn[l], w_grp[l], b_grp[l], w_er[l], b_er[l], w_gate, w_up, w_down, l, tmj)

    y_prompt = _rmsnorm(x, g_final, n_p, 0, tmp).reshape(n_b, seq, d)
    y_sample = _rmsnorm(x, g_final, n_smp, n_p // tms, tms).reshape(n_s, dec, d)
    stack = lambda rows: [jnp.stack(a) for a in zip(*rows)]
    mem_k = jnp.stack([kv5(a, (n_b, n_mem)) for a in mem_k_p])
    mem_v = jnp.stack([kv5(a, (n_b, n_mem)) for a in mem_v_p])
    return (y_prompt, y_sample, *stack(even_p), *stack(odd_p), mem_k, mem_v, *stack(even_s), *stack(odd_s))
```

```python
import functools
import math

import numpy as np
import jax
import jax.numpy as jnp
from jax import lax
from jax.experimental import pallas as pl
from jax.experimental.pallas import tpu as pltpu

F32 = jnp.float32
BF16 = jnp.bfloat16

HEAD_DIM = 128
A_HEADS, A_KV = 8, 4
F_HEADS, F_KV = 8, 4
C_HEADS, C_KV = 16, 4
KV = 4
CMP_BLOCK = 64
SEL_BLOCK = 64
SEL_SHIFT = 6
TOP_N = 16
WINDOW = 512
X_HEADS = 4
N_GROUPS = 4
EXP_PER_GROUP = 8
N_EXPERTS = N_GROUPS * EXP_PER_GROUP
EPS = 1e-6
NEG = -1e30
FORCE = 1e4
E_MAIN = (A_HEADS + 2 * A_KV + F_HEADS + 2 * F_KV) * HEAD_DIM
O_MAIN = (C_HEADS + 6 * C_KV) * HEAD_DIM

LANE = 128
VMEM_LIMIT = 56 * 1024 * 1024
PAGES_PER_STEP = 16
MOE_TM = 768
COMBINE_TM = 256
KV_COLS = KV * HEAD_DIM


def _cparams(sem):
    return pltpu.CompilerParams(dimension_semantics=sem, vmem_limit_bytes=VMEM_LIMIT)


def _dot_nt(a, b):
    return lax.dot_general(a, b, (((1,), (1,)), ((), ())), preferred_element_type=F32)


def _dot(a, b):
    return jnp.dot(a, b, preferred_element_type=F32)


def _alibi_slopes(n_heads):
    return jnp.asarray(np.exp2(-8.0 * np.arange(1, n_heads + 1) / n_heads).astype(np.float32))


def _lambda_init(layer):
    return 0.8 - 0.6 * math.exp(-0.3 * layer)


def _head_rows(ref, h, n, row0=0):
    return ref[pl.ds(row0 * KV + h, n, stride=KV), :]


def _nmm_body(*refs, norm, epilogue, n_out, kv_cols):
    it = iter(refs)
    a_ref = next(it)
    g_ref = next(it) if norm else None
    b_ref = next(it)
    e_ref = next(it) if epilogue is not None else None
    outs = [next(it) for _ in range(n_out)]
    kv_refs = [next(it) for _ in kv_cols]
    as_ref = next(it)
    tm = a_ref.shape[0]

    @pl.when(pl.program_id(1) == 0)
    def _():
        a = a_ref[...].astype(F32)
        if norm:
            a = a * lax.rsqrt(jnp.mean(a * a, axis=-1, keepdims=True) + EPS) * g_ref[...]
        as_ref[...] = a.astype(BF16)

    acc = _dot(as_ref[...], b_ref[...])
    if epilogue == "res":
        acc = acc + e_ref[...]
    elif epilogue == "logsig":
        z = acc + e_ref[...]
        acc = jnp.minimum(z, 0.0) - jnp.log1p(jnp.exp(-jnp.abs(z)))
    for o in outs:
        o[...] = acc.astype(o.dtype)
    for ref, jcol in zip(kv_refs, kv_cols):
        @pl.when(pl.program_id(1) == jcol)
        def _(ref=ref):
            for h in range(KV):
                ref[pl.ds(h, tm, stride=KV), :] = acc[:, h * HEAD_DIM:(h + 1) * HEAD_DIM]


def _nmm(a, b, *, g=None, res=None, bias=None, out_dtypes=(F32,), kv_cols=(), m=None, row_off=0, tm, tn):
    k = a.shape[1]
    m = a.shape[0] if m is None else m
    n = b.shape[1]
    assert m % tm == 0 and n % tn == 0 and (not kv_cols or tn == KV_COLS)
    norm = g is not None
    epilogue = "res" if res is not None else ("logsig" if bias is not None else None)
    ins = [a]
    specs = [pl.BlockSpec((tm, k), lambda i, j: (i + row_off, 0))]
    if norm:
        ins.append(g.reshape(1, k).astype(F32))
        specs.append(pl.BlockSpec((1, k), lambda i, j: (0, 0)))
    ins.append(b)
    specs.append(pl.BlockSpec((k, tn), lambda i, j: (0, j)))
    if epilogue == "res":
        ins.append(res)
        specs.append(pl.BlockSpec((tm, tn), lambda i, j: (i + row_off, j)))
    elif epilogue == "logsig":
        ins.append(bias.reshape(1, n).astype(F32))
        specs.append(pl.BlockSpec((1, tn), lambda i, j: (0, j)))
    out_shape = [jax.ShapeDtypeStruct((m, n), dt) for dt in out_dtypes]
    out_specs = [pl.BlockSpec((tm, tn), lambda i, j: (i, j)) for _ in out_dtypes]
    out_shape += [jax.ShapeDtypeStruct((m * KV, HEAD_DIM), F32) for _ in kv_cols]
    out_specs += [pl.BlockSpec((tm * KV, HEAD_DIM), lambda i, j: (i, 0)) for _ in kv_cols]
    outs = pl.pallas_call(
        functools.partial(_nmm_body, norm=norm, epilogue=epilogue, n_out=len(out_dtypes), kv_cols=tuple(kv_cols)),
        grid=(m // tm, n // tn),
        in_specs=specs,
        out_specs=out_specs,
        out_shape=out_shape,
        scratch_shapes=[pltpu.VMEM((tm, k), BF16)],
        compiler_params=_cparams(("parallel", "arbitrary")),
    )(*ins)
    return outs[0] if len(outs) == 1 else outs


def _cumsum_body(x_ref, o_ref):
    nbk, h, l = x_ref.shape
    r = lax.broadcasted_iota(jnp.int32, (LANE, LANE), 0)
    c = lax.broadcasted_iota(jnp.int32, (LANE, LANE), 1)
    upper = (r <= c).astype(F32)
    carry = jnp.zeros((nbk * h, 1), F32)
    for j in range(l // LANE):
        blk = x_ref[:, :, j * LANE:(j + 1) * LANE].reshape(nbk * h, LANE)
        cs = jnp.dot(blk, upper, preferred_element_type=F32, precision=lax.Precision.HIGHEST)
        o_ref[:, :, j * LANE:(j + 1) * LANE] = (cs + carry).reshape(nbk, h, LANE)
        carry = carry + cs[:, LANE - 1:LANE]


def _cumsum(x):
    nb, h, l = x.shape
    nbk = math.gcd(nb, 8)
    return pl.pallas_call(
        _cumsum_body,
        grid=(nb // nbk,),
        in_specs=[pl.BlockSpec((nbk, h, l), lambda i: (i, 0, 0))],
        out_specs=pl.BlockSpec((nbk, h, l), lambda i: (i, 0, 0)),
        out_shape=jax.ShapeDtypeStruct((nb, h, l), F32),
        compiler_params=_cparams(("parallel",)),
    )(x)


def _with_ones(v):
    return jnp.concatenate([v, jnp.ones_like(v)], axis=1)


def _online_update(s, v_ext, m_ref, acc_ref, idx):
    m_prev = m_ref[idx]
    m_new = jnp.maximum(m_prev, jnp.max(s, axis=-1, keepdims=True))
    alpha = jnp.exp(m_prev - m_new)
    p = jnp.exp(s - jnp.tile(m_new, (1, s.shape[1] // LANE)))
    acc_ref[idx] = jnp.tile(alpha, (1, 2)) * acc_ref[idx] + _dot(p.astype(BF16), v_ext)
    m_ref[idx] = m_new


def _flash_out(acc):
    d = acc.shape[1] // 2
    return acc[:, :d] / acc[:, d:]


def _softmax_rows(s):
    e = jnp.exp(s - jnp.max(s, axis=-1, keepdims=True))
    return e / jnp.sum(e, axis=-1, keepdims=True)


def _softmax_apply(s, v_ext):
    e = jnp.exp(s - jnp.max(s, axis=-1, keepdims=True))
    return _flash_out(_dot(e.astype(BF16), v_ext))


def _lambda_value(lp_ref, lam_init):
    lp = lp_ref[...]
    a = jnp.sum(lp[0:1] * lp[1:2], axis=-1, keepdims=True)
    b = jnp.sum(lp[2:3] * lp[3:4], axis=-1, keepdims=True)
    return jnp.exp(a) - jnp.exp(b) + lam_init


def _dif_head_out(acc1, acc2, lam, gh, lam_init):
    o = _flash_out(acc1) - lam * _flash_out(acc2)
    o = o * lax.rsqrt(jnp.mean(o * o, axis=-1, keepdims=True) + EPS) * gh
    return o * (1.0 - lam_init)


def _row_select(row, step, vals):
    out = vals[-1]
    for g in range(len(vals) - 2, -1, -1):
        out = jnp.where(row < (g + 1) * step, vals[g], out)
    return out


EP_T = 512


def _even_prompt_body(sl_ref, qa_ref, ka_ref, va_ref, qf_ref, kf_ref, vf_ref, ck_ref, lp_ref, gh_ref,
                      o_ref, qs_ref, m_ref, acc_ref, *, lam_init):
    h = pl.program_id(1)
    qi = pl.program_id(2)
    t = EP_T
    d = HEAD_DIM
    lane = lax.broadcasted_iota(jnp.int32, (t, d), 1)
    zero = jnp.zeros((t, d), BF16)
    for g in range(2):
        q = qa_ref[:, g * d:(g + 1) * d] * jnp.asarray(0.125, BF16)
        qs_ref[g, 0] = jnp.where(lane < d // 2, q, zero)
        qs_ref[g, 1] = jnp.where(lane >= d // 2, q, zero)
        qs_ref[g, 2] = qf_ref[:, g * d:(g + 1) * d]
    m_ref[...] = jnp.full(m_ref.shape, NEG, F32)
    acc_ref[...] = jnp.zeros(acc_ref.shape, F32)
    scale_f = d ** -0.5

    def tile(ki, diag):
        k0 = pl.multiple_of(ki * t, t)
        ka = ka_ref[pl.ds(k0, t), :]
        va = _with_ones(va_ref[pl.ds(k0, t), :])
        kf = kf_ref[pl.ds(k0, t), :]
        vf = _with_ones(vf_ref[pl.ds(k0, t), :])
        rel = (k0 - qi * t) + lax.broadcasted_iota(jnp.int32, (1, t), 1)
        relf = rel.astype(F32)
        if diag:
            mask = rel <= lax.broadcasted_iota(jnp.int32, (t, 1), 0)
        for g in range(2):
            bias_a = sl_ref[2 * h + g] * relf
            bias_f = -ck_ref[g:g + 1, pl.ds(k0, t)]
            for kind in range(3):
                if kind < 2:
                    s = _dot_nt(qs_ref[g, kind], ka) + bias_a
                    v = va
                else:
                    s = _dot_nt(qs_ref[g, 2], kf) * scale_f + bias_f
                    v = vf
                if diag:
                    s = jnp.where(mask, s, NEG)
                _online_update(s, v, m_ref, acc_ref, (g, kind))

    def body(ki, carry):
        tile(ki, False)
        return carry

    lax.fori_loop(0, qi, body, 0)
    tile(qi, True)

    lam = _lambda_value(lp_ref, lam_init)
    gh = gh_ref[...]
    for g in range(2):
        od = _dif_head_out(acc_ref[g, 0], acc_ref[g, 1], lam, gh, lam_init)
        of = _flash_out(acc_ref[g, 2])
        o_ref[:, g * d:(g + 1) * d] = od.astype(o_ref.dtype)
        o_ref[:, (2 + g) * d:(3 + g) * d] = of.astype(o_ref.dtype)


def _even_prompt_attn(qkv16, ck, lam_p, g_head, n_batch, seq, lam_init):
    t = EP_T
    nq = seq // t
    d = HEAD_DIM
    kv_spec = lambda col0: pl.BlockSpec((seq, d), lambda b, h, qi: (b, col0 + h))
    q_spec = lambda col0: pl.BlockSpec((t, 2 * d), lambda b, h, qi: (b * nq + qi, col0 + h))
    return pl.pallas_call(
        functools.partial(_even_prompt_body, lam_init=lam_init),
        grid=(n_batch, A_KV, nq),
        in_specs=[
            pl.BlockSpec(memory_space=pltpu.SMEM),
            q_spec(0),
            kv_spec(8), kv_spec(12),
            q_spec(8),
            kv_spec(24), kv_spec(28),
            pl.BlockSpec((None, None, 2, seq), lambda b, h, qi: (b, h, 0, 0)),
            pl.BlockSpec((4, d // 2), lambda b, h, qi: (0, 0)),
            pl.BlockSpec((1, d), lambda b, h, qi: (0, 0)),
        ],
        out_specs=pl.BlockSpec((t, 4 * d), lambda b, h, qi: (b * nq + qi, h)),
        scratch_shapes=[
            pltpu.VMEM((2, 3, t, d), BF16),
            pltpu.VMEM((2, 3, t, LANE), F32),
            pltpu.VMEM((2, 3, t, 2 * d), F32),
        ],
        out_shape=jax.ShapeDtypeStruct((n_batch * seq, 4 * A_KV * d), BF16),
        compiler_params=_cparams(("parallel", "parallel", "arbitrary")),
    )(_alibi_slopes(A_HEADS), qkv16, qkv16, qkv16, qkv16, qkv16, qkv16, ck, lam_p, g_head.reshape(1, d))


def _even_sample_body(*refs, n_pages, past_len, dec, lam_init):
    p_cnt = PAGES_PER_STEP
    pt_ref, sl_ref, qkv_ref, ck_ref, ckn_ref, lp_ref, gh_ref = refs[:7]
    pages = refs[7:7 + 4 * p_cnt]
    dk, dv, fk, fv = (pages[i * p_cnt:(i + 1) * p_cnt] for i in range(4))
    o_ref, qs_ref, m_ref, acc_ref = refs[7 + 4 * p_cnt:]
    del pt_ref
    c = pl.program_id(1)
    d = HEAD_DIM
    rows = 2 * dec
    row = lax.broadcasted_iota(jnp.int32, (rows, 1), 0)
    first = row < dec
    scale_f = d ** -0.5
    lane = lax.broadcasted_iota(jnp.int32, (rows, d), 1)

    def stacked(col0):
        return jnp.concatenate([qkv_ref[:, col0:col0 + d], qkv_ref[:, col0 + d:col0 + 2 * d]], axis=0)

    @pl.when(c == 0)
    def _():
        for h in range(A_KV):
            q = (stacked(h * 2 * d) * 0.125).astype(BF16)
            zero = jnp.zeros_like(q)
            qs_ref[h, 0] = jnp.where(lane < d // 2, q, zero)
            qs_ref[h, 1] = jnp.where(lane >= d // 2, q, zero)
            qs_ref[h, 2] = stacked(2048 + h * 2 * d).astype(BF16)
        m_ref[...] = jnp.full(m_ref.shape, NEG, F32)
        acc_ref[...] = jnp.zeros(acc_ref.shape, F32)

    def head_keys(prefs, h):
        return jnp.concatenate([_head_rows(r, h, LANE) for r in prefs], axis=0).astype(BF16)

    n_keys = p_cnt * LANE
    relf = (c * n_keys - past_len + lax.broadcasted_iota(jnp.int32, (1, n_keys), 1)).astype(F32)

    def slope_col(h):
        return jnp.where(first, sl_ref[2 * h], sl_ref[2 * h + 1])

    def ck_rows(ref, h):
        return jnp.where(first, ref[2 * h:2 * h + 1, :], ref[2 * h + 1:2 * h + 2, :])

    for h in range(A_KV):
        ka, kf = head_keys(dk, h), head_keys(fk, h)
        va, vf = _with_ones(head_keys(dv, h)), _with_ones(head_keys(fv, h))
        bias_a = slope_col(h) * relf
        bias_f = -ck_rows(ck_ref, h)
        for kind in range(3):
            if kind < 2:
                s = _dot_nt(qs_ref[h, kind], ka) + bias_a
                v = va
            else:
                s = _dot_nt(qs_ref[h, 2], kf) * scale_f + bias_f
                v = vf
            _online_update(s, v, m_ref, acc_ref, (h, kind))

    @pl.when(c == n_pages // p_cnt - 1)
    def _():
        jn = lax.broadcasted_iota(jnp.int32, (1, LANE), 1)
        tok = jnp.where(first, row, row - dec)
        mask_n = (jn < dec) & (jn <= tok)
        reln = jn.astype(F32)
        pad = jnp.zeros((LANE - dec, d), F32)
        lam = _lambda_value(lp_ref, lam_init)
        gh = gh_ref[...]
        for h in range(A_KV):
            def new_rows(col0):
                return jnp.concatenate([qkv_ref[:, col0 + h * d:col0 + (h + 1) * d], pad], axis=0).astype(BF16)
            ka, kf = new_rows(1024), new_rows(3072)
            va, vf = _with_ones(new_rows(1536)), _with_ones(new_rows(3584))
            bias_a = slope_col(h) * reln
            bias_f = -ck_rows(ckn_ref, h)
            for kind in range(3):
                if kind < 2:
                    s = _dot_nt(qs_ref[h, kind], ka) + bias_a
                    v = va
                else:
                    s = _dot_nt(qs_ref[h, 2], kf) * scale_f + bias_f
                    v = vf
                s = jnp.where(mask_n, s, NEG)
                _online_update(s, v, m_ref, acc_ref, (h, kind))
            od = _dif_head_out(acc_ref[h, 0], acc_ref[h, 1], lam, gh, lam_init)
            of = _flash_out(acc_ref[h, 2])
            for g in range(2):
                o_ref[:, (4 * h + g) * d:(4 * h + g + 1) * d] = od[g * dec:(g + 1) * dec]
                o_ref[:, (4 * h + 2 + g) * d:(4 * h + 3 + g) * d] = of[g * dec:(g + 1) * dec]


def _page_spec(p_cnt, i):
    return pl.BlockSpec((None, LANE * KV, HEAD_DIM), lambda s, c, pt: (pt[s, c * p_cnt + i], 0, 0))


def _even_sample_attn(qkv32, caches, page_table, ck, lam_p, g_head, lam_init, dec):
    n_seq, n_pages = page_table.shape
    past_len = n_pages * LANE
    p_cnt = PAGES_PER_STEP
    d = HEAD_DIM
    n_chunks = n_pages // p_cnt
    in_specs = [
        pl.BlockSpec(memory_space=pltpu.SMEM),
        pl.BlockSpec((dec, qkv32.shape[1]), lambda s, c, pt: (s, 0)),
        pl.BlockSpec((None, 8, p_cnt * LANE), lambda s, c, pt: (s, 0, c)),
        pl.BlockSpec((None, 8, LANE), lambda s, c, pt: (s, 0, n_pages)),
        pl.BlockSpec((4, d // 2), lambda s, c, pt: (0, 0)),
        pl.BlockSpec((1, d), lambda s, c, pt: (0, 0)),
    ]
    ins = [_alibi_slopes(A_HEADS), qkv32, ck, ck, lam_p, g_head.reshape(1, d)]
    for arr in caches:
        for i in range(p_cnt):
            in_specs.append(_page_spec(p_cnt, i))
            ins.append(arr)
    rows = 2 * dec
    grid_spec = pltpu.PrefetchScalarGridSpec(
        num_scalar_prefetch=1,
        grid=(n_seq, n_chunks),
        in_specs=in_specs,
        out_specs=pl.BlockSpec((dec, 16 * d), lambda s, c, pt: (s, 0)),
        scratch_shapes=[
            pltpu.VMEM((A_KV, 3, rows, d), BF16),
            pltpu.VMEM((A_KV, 3, rows, LANE), F32),
            pltpu.VMEM((A_KV, 3, rows, 2 * d), F32),
        ],
    )
    return pl.pallas_call(
        functools.partial(_even_sample_body, n_pages=n_pages, past_len=past_len, dec=dec, lam_init=lam_init),
        grid_spec=grid_spec,
        out_shape=jax.ShapeDtypeStruct((n_seq * dec, 16 * d), F32),
        compiler_params=_cparams(("parallel", "arbitrary")),
    )(page_table, *ins)


def _even_out_perm():
    idx = []
    for h in range(A_KV):
        for base in (0, A_HEADS):
            for g in range(2):
                head = base + 2 * h + g
                idx.extend(range(head * HEAD_DIM, (head + 1) * HEAD_DIM))
    return np.asarray(idx, np.int32)


def _pos_softmax_body(pk_ref, pv_ref, wk_ref, wv_ref):
    for src, dst in ((pk_ref, wk_ref), (pv_ref, wv_ref)):
        x = src[...]
        e = jnp.exp(x - jnp.max(x, axis=1, keepdims=True))
        w = e / jnp.sum(e, axis=1, keepdims=True)
        dst[...] = jnp.concatenate([w, w], axis=1)


def _pos_softmax(pos_k, pos_v):
    expand = lambda p: jnp.broadcast_to(p.astype(F32).T[:, :, None], (C_KV, CMP_BLOCK, HEAD_DIM))
    shape = jax.ShapeDtypeStruct((C_KV, 2 * CMP_BLOCK, HEAD_DIM), F32)
    return pl.pallas_call(_pos_softmax_body, out_shape=(shape, shape))(expand(pos_k), expand(pos_v))


def _block_sums(x, w):
    r, width = x.shape
    xw = x.reshape(r // LANE, LANE, width) * w[None]
    return jnp.sum(xw.reshape(r // CMP_BLOCK, CMP_BLOCK, width), axis=1)


def _compress_rows_body(k_ref, v_ref, wk_ref, wv_ref, ok_ref, ov_ref):
    d = HEAD_DIM
    for h in range(C_KV):
        ok_ref[h] = _block_sums(k_ref[:, h * d:(h + 1) * d].astype(F32), wk_ref[h])
        ov_ref[h] = _block_sums(v_ref[:, h * d:(h + 1) * d].astype(F32), wv_ref[h])


def _compress_rows(qkv16, wk, wv, n_batch, seq, col_k, col_v):
    tr = 1024
    nc = seq // tr
    d = HEAD_DIM
    shape = jax.ShapeDtypeStruct((n_batch, C_KV, seq // CMP_BLOCK, d), F32)
    w_spec = pl.BlockSpec((C_KV, 2 * CMP_BLOCK, d), lambda b, i: (0, 0, 0))
    o_spec = pl.BlockSpec((None, C_KV, tr // CMP_BLOCK, d), lambda b, i: (b, 0, i, 0))
    return pl.pallas_call(
        _compress_rows_body,
        grid=(n_batch, nc),
        in_specs=[pl.BlockSpec((tr, KV_COLS), lambda b, i: (b * nc + i, col_k)),
                  pl.BlockSpec((tr, KV_COLS), lambda b, i: (b * nc + i, col_v)), w_spec, w_spec],
        out_specs=(o_spec, o_spec),
        out_shape=(shape, shape),
        compiler_params=_cparams(("parallel", "parallel")),
    )(qkv16, qkv16, wk, wv)


def _compress_paged_body(*refs):
    p_cnt = PAGES_PER_STEP
    kp = refs[1:1 + p_cnt]
    vp = refs[1 + p_cnt:1 + 2 * p_cnt]
    wk_ref, wv_ref, ok_ref, ov_ref = refs[1 + 2 * p_cnt:]
    for h in range(C_KV):
        wk, wv = wk_ref[h], wv_ref[h]
        ok_ref[h] = jnp.concatenate([_block_sums(_head_rows(r, h, LANE), wk) for r in kp], axis=0)
        ov_ref[h] = jnp.concatenate([_block_sums(_head_rows(r, h, LANE), wv) for r in vp], axis=0)


def _compress_paged(cache_k, cache_v, page_table, wk, wv):
    n_seq, n_pages = page_table.shape
    p_cnt = PAGES_PER_STEP
    d = HEAD_DIM
    per_step = p_cnt * LANE // CMP_BLOCK
    w_spec = pl.BlockSpec((C_KV, 2 * CMP_BLOCK, d), lambda s, c, pt: (0, 0, 0))
    o_spec = pl.BlockSpec((None, C_KV, per_step, d), lambda s, c, pt: (s, 0, c, 0))
    shape = jax.ShapeDtypeStruct((n_seq, C_KV, n_pages * LANE // CMP_BLOCK, d), F32)
    grid_spec = pltpu.PrefetchScalarGridSpec(
        num_scalar_prefetch=1,
        grid=(n_seq, n_pages // p_cnt),
        in_specs=[_page_spec(p_cnt, i) for i in range(p_cnt)] * 2 + [w_spec, w_spec],
        out_specs=(o_spec, o_spec),
    )
    return pl.pallas_call(
        _compress_paged_body,
        grid_spec=grid_spec,
        out_shape=(shape, shape),
        compiler_params=_cparams(("parallel", "arbitrary")),
    )(page_table, *([cache_k] * p_cnt), *([cache_v] * p_cnt), wk, wv)


def _top_n_mask(score, n_blocks, valid):
    lane = lax.broadcasted_iota(jnp.int32, score.shape, 1)
    rank = jnp.zeros(score.shape, F32)
    for i in range(n_blocks):
        col = score[:, i:i + 1]
        ahead = (col > score) | ((col == score) & (lane > i))
        rank = rank + jnp.where(ahead, 1.0, 0.0)
    keep = (rank < float(TOP_N)) & (lane < n_blocks) & valid
    return jnp.where(keep, 1.0, 0.0)


def _top_n_mask_cols(score, valid):
    blk = lax.broadcasted_iota(jnp.int32, score.shape, 0)
    rank = jnp.zeros(score.shape, F32)
    for i in range(score.shape[0]):
        row = score[i:i + 1, :]
        ahead = (row > score) | ((row == score) & (blk > i))
        rank = rank + jnp.where(ahead, 1.0, 0.0)
    return jnp.where((rank < float(TOP_N)) & valid, 1.0, 0.0)


def _eye(n, dtype):
    r = lax.broadcasted_iota(jnp.int32, (n, n), 0)
    c = lax.broadcasted_iota(jnp.int32, (n, n), 1)
    return jnp.where(r == c, 1.0, 0.0).astype(dtype)


def _stack_heads(ref, col0, n):
    return jnp.concatenate([ref[:, col0 + g * HEAD_DIM:col0 + (g + 1) * HEAD_DIM] for g in range(n)], axis=0)


OP_T = 256
OP_TK = 512


def _odd_prompt_body(sl_ref, q_ref, ks_ref, vs_ref, kw_ref, vw_ref, kc_ref, vc_ref, gt_ref,
                     o_ref, sx_ref, m_ref, acc_ref, *, seq):
    h = pl.program_id(1)
    qi = pl.program_id(2)
    t = OP_T
    tk = OP_TK
    d = HEAD_DIM
    gc = C_HEADS // C_KV
    rows = gc * t
    nb = seq // CMP_BLOCK
    scale = d ** -0.5
    q = _stack_heads(q_ref, 0, gc)
    row = lax.broadcasted_iota(jnp.int32, (rows, 1), 0)
    slopes = [sl_ref[gc * h + g] for g in range(gc)]
    slope = _row_select(row, t, slopes)
    tok = _row_select(row, t, [row - g * t for g in range(gc)])
    tok1 = lax.broadcasted_iota(jnp.int32, (t, 1), 0)
    q0 = qi * t

    blk = lax.broadcasted_iota(jnp.int32, (1, nb), 1)
    bend_rel = (blk + 1) * CMP_BLOCK - 1 - q0
    mask_c = bend_rel <= tok
    s_c = _dot_nt(q, kc_ref[...].astype(BF16)) * scale + slope * bend_rel.astype(F32)
    p_c = jnp.where(mask_c, _softmax_rows(jnp.where(mask_c, s_c, NEG)), 0.0)
    o_c = _dot(p_c.astype(BF16), vc_ref[...].astype(BF16))

    imp = p_c[0:t]
    for g in range(1, gc):
        imp = imp + p_c[g * t:(g + 1) * t]
    eye_nb = _eye(nb, F32)
    imp_t = lax.dot_general(eye_nb, imp, (((1,), (1,)), ((), ())), preferred_element_type=F32,
                            precision=lax.Precision.HIGHEST)
    blk_c = lax.broadcasted_iota(jnp.int32, (nb, 1), 0)
    pos_r = q0 + lax.broadcasted_iota(jnp.int32, (1, t), 1)
    cur = pos_r >> SEL_SHIFT
    forced = (blk_c == 0) | (blk_c == cur) | (blk_c == cur - 1)
    valid = blk_c * SEL_BLOCK <= pos_r
    score_t = jnp.where(valid, imp_t + jnp.where(forced, FORCE, 0.0), NEG)
    sel_t = _top_n_mask_cols(score_t, valid).astype(BF16)
    sel = _dot_nt(_eye(t, BF16), sel_t).astype(BF16)
    er = lax.broadcasted_iota(jnp.int32, (nb, seq), 0)
    ec = lax.broadcasted_iota(jnp.int32, (nb, seq), 1)
    expand = jnp.where(er == (ec >> SEL_SHIFT), 1.0, 0.0).astype(BF16)
    sel_keys = _dot(sel, expand)
    for kt in range(seq // tk):
        sx_ref[kt] = (sel_keys[:, kt * tk:(kt + 1) * tk] - 1.0) * (-NEG)

    m_ref[...] = jnp.full(m_ref.shape, NEG, F32)
    acc_ref[...] = jnp.zeros(acc_ref.shape, F32)

    def tile(ki, diag):
        k0 = pl.multiple_of(ki * tk, tk)
        k = ks_ref[pl.ds(k0, tk), :]
        v = _with_ones(vs_ref[pl.ds(k0, tk), :])
        rel = (k0 - q0) + lax.broadcasted_iota(jnp.int32, (1, tk), 1)
        relf = rel.astype(F32)
        madd = sx_ref[ki]
        if diag:
            madd = jnp.where(rel <= tok1, madd, NEG)
        s_all = _dot_nt(q, k) * scale
        for g in range(gc):
            s = s_all[g * t:(g + 1) * t] + (slopes[g] * relf + madd)
            _online_update(s, v, m_ref, acc_ref, g)

    def body(ki, carry):
        tile(ki, False)
        return carry

    n_full = (qi * t) // tk
    lax.fori_loop(0, n_full, body, 0)
    tile(n_full, True)

    n_slab = WINDOW + t
    start = pl.multiple_of(jnp.maximum(q0 - WINDOW, 0), t)
    kw = kw_ref[pl.ds(start, n_slab), :]
    vw = _with_ones(vw_ref[pl.ds(start, n_slab), :])
    rel_w = (start - q0) + lax.broadcasted_iota(jnp.int32, (1, n_slab), 1)
    delta = tok - rel_w
    mask_w = (delta >= 0) & (delta < WINDOW)
    s_w = _dot_nt(q, kw) * scale + slope * rel_w.astype(F32)
    o_w = _softmax_apply(jnp.where(mask_w, s_w, NEG), vw)

    gates = jax.nn.sigmoid(gt_ref[...])
    for g in range(gc):
        rs = slice(g * t, (g + 1) * t)
        o = (gates[:, 3 * g:3 * g + 1] * o_c[rs] + gates[:, 3 * g + 1:3 * g + 2] * _flash_out(acc_ref[g])
             + gates[:, 3 * g + 2:3 * g + 3] * o_w[rs])
        o_ref[:, g * d:(g + 1) * d] = o.astype(o_ref.dtype)


def _odd_prompt_attn(qkv16, k_cmp, v_cmp, gates, n_batch, seq):
    t = OP_T
    nq = seq // t
    d = HEAD_DIM
    gc = C_HEADS // C_KV
    assert seq % OP_TK == 0 and OP_TK % t == 0
    kv_spec = lambda col0: pl.BlockSpec((seq, d), lambda b, h, qi: (b, col0 + h))
    cmp_spec = pl.BlockSpec((None, None, seq // CMP_BLOCK, d), lambda b, h, qi: (b, h, 0, 0))
    return pl.pallas_call(
        functools.partial(_odd_prompt_body, seq=seq),
        grid=(n_batch, C_KV, nq),
        in_specs=[
            pl.BlockSpec(memory_space=pltpu.SMEM),
            pl.BlockSpec((t, gc * d), lambda b, h, qi: (b * nq + qi, h)),
            kv_spec(24), kv_spec(28), kv_spec(32), kv_spec(36),
            cmp_spec, cmp_spec,
            pl.BlockSpec((t, d), lambda b, h, qi: (b * nq + qi, h)),
        ],
        out_specs=pl.BlockSpec((t, gc * d), lambda b, h, qi: (b * nq + qi, h)),
        scratch_shapes=[
            pltpu.VMEM((seq // OP_TK, t, OP_TK), F32),
            pltpu.VMEM((gc, t, LANE), F32),
            pltpu.VMEM((gc, t, 2 * d), F32),
        ],
        out_shape=jax.ShapeDtypeStruct((n_batch * seq, C_HEADS * d), BF16),
        compiler_params=_cparams(("parallel", "parallel", "arbitrary")),
    )(_alibi_slopes(C_HEADS), qkv16, qkv16, qkv16, qkv16, qkv16, k_cmp, v_cmp, gates)


def _odd_sample_body(*refs, n_pages, past_len, dec):
    p_cnt = PAGES_PER_STEP
    pt_ref, sl_ref, qkv_ref, gt_ref, kc_ref, vc_ref, wk_ref, wv_ref = refs[:8]
    sk = refs[8:8 + p_cnt]
    sv = refs[8 + p_cnt:8 + 2 * p_cnt]
    o_ref, wko_ref, wvo_ref, qs_ref, sel_ref, oc_ref, m_ref, acc_ref = refs[8 + 2 * p_cnt:]
    del pt_ref
    c = pl.program_id(1)
    d = HEAD_DIM
    gc = C_HEADS // C_KV
    rows = gc * dec
    nb = past_len // CMP_BLOCK
    nbs = -(-(past_len + dec) // SEL_BLOCK)
    sel_w = 2 * LANE
    assert nb == LANE and nbs <= sel_w and (past_len + dec - 1) // SEL_BLOCK == past_len // SEL_BLOCK
    cur = past_len // SEL_BLOCK
    scale = d ** -0.5
    row = lax.broadcasted_iota(jnp.int32, (rows, 1), 0)
    tok = _row_select(row, dec, [row - g * dec for g in range(gc)])
    slopes = [_row_select(row, dec, [sl_ref[gc * h + g] for g in range(gc)]) for h in range(C_KV)]
    c_ks, c_vs, c_kw, c_vw = ((C_HEADS + i * C_KV) * d for i in (2, 3, 4, 5))

    @pl.when(c == 0)
    def _():
        blk = lax.broadcasted_iota(jnp.int32, (1, nb), 1)
        bend_rel = ((blk + 1) * CMP_BLOCK - 1 - past_len).astype(F32)
        lane = lax.broadcasted_iota(jnp.int32, (dec, sel_w), 1)
        forced = (lane == 0) | (lane == cur) | (lane == cur - 1)
        for h in range(C_KV):
            q = _stack_heads(qkv_ref, h * gc * d, gc).astype(BF16)
            qs_ref[h] = q
            p_c = _softmax_rows(_dot_nt(q, kc_ref[h].astype(BF16)) * scale + slopes[h] * bend_rel)
            oc_ref[h] = _dot(p_c.astype(BF16), vc_ref[h].astype(BF16))
            imp = p_c[0:dec]
            for g in range(1, gc):
                imp = imp + p_c[g * dec:(g + 1) * dec]
            imp = jnp.concatenate([imp, jnp.zeros((dec, sel_w - nb), F32)], axis=1)
            score = imp + jnp.where(forced, FORCE, 0.0)
            sel_ref[h] = _top_n_mask(score, nbs, lane < nbs)
        m_ref[...] = jnp.full(m_ref.shape, NEG, F32)
        acc_ref[...] = jnp.zeros(acc_ref.shape, F32)

    n_keys = p_cnt * LANE
    relf = (c * n_keys - past_len + lax.broadcasted_iota(jnp.int32, (1, n_keys), 1)).astype(F32)
    er = lax.broadcasted_iota(jnp.int32, (sel_w, n_keys), 0)
    ec = lax.broadcasted_iota(jnp.int32, (sel_w, n_keys), 1)
    expand = jnp.where(er - c * (n_keys // SEL_BLOCK) == (ec >> SEL_SHIFT), 1.0, 0.0).astype(BF16)
    for h in range(C_KV):
        k = jnp.concatenate([_head_rows(r, h, LANE) for r in sk], axis=0).astype(BF16)
        v = _with_ones(jnp.concatenate([_head_rows(r, h, LANE) for r in sv], axis=0).astype(BF16))
        keys = _dot(sel_ref[h].astype(BF16), expand)
        madd = jnp.concatenate([(keys - 1.0) * (-NEG)] * gc, axis=0)
        s = _dot_nt(qs_ref[h], k) * scale + (slopes[h] * relf + madd)
        _online_update(s, v, m_ref, acc_ref, h)

    @pl.when(c == n_pages // p_cnt - 1)
    def _():
        jn = lax.broadcasted_iota(jnp.int32, (1, LANE), 1)
        reln = jn.astype(F32)
        causal_n = (jn < dec) & (jn <= tok)
        pad = jnp.zeros((LANE - dec, d), F32)
        gates = jax.nn.sigmoid(gt_ref[...])
        n_win = WINDOW + LANE
        je = lax.broadcasted_iota(jnp.int32, (1, n_win), 1)
        rel_w = je - WINDOW
        delta = tok - rel_w
        mask_w = (delta >= 0) & (delta < WINDOW) & (je < WINDOW + dec)
        for h in range(C_KV):
            q = qs_ref[h]
            new = lambda col0: jnp.concatenate([qkv_ref[:, col0 + h * d:col0 + (h + 1) * d], pad], axis=0)
            picked = jnp.concatenate([sel_ref[h][:, cur:cur + 1]] * gc, axis=0) > 0.5
            s = _dot_nt(q, new(c_ks).astype(BF16)) * scale + slopes[h] * reln
            _online_update(jnp.where(causal_n & picked, s, NEG), _with_ones(new(c_vs).astype(BF16)), m_ref, acc_ref, h)
            o_s = _flash_out(acc_ref[h])
            kw = jnp.concatenate([_head_rows(wk_ref, h, WINDOW), new(c_kw)], axis=0).astype(BF16)
            vw = jnp.concatenate([_head_rows(wv_ref, h, WINDOW), new(c_vw)], axis=0).astype(BF16)
            s_w = _dot_nt(q, kw) * scale + slopes[h] * rel_w.astype(F32)
            o_w = _softmax_apply(jnp.where(mask_w, s_w, NEG), _with_ones(vw))
            o_c = oc_ref[h]
            for g in range(gc):
                rs = slice(g * dec, (g + 1) * dec)
                l0 = h * d + 3 * g
                o = (gates[:, l0:l0 + 1] * o_c[rs] + gates[:, l0 + 1:l0 + 2] * o_s[rs]
                     + gates[:, l0 + 2:l0 + 3] * o_w[rs])
                o_ref[:, (gc * h + g) * d:(gc * h + g + 1) * d] = o
            wko_ref[pl.ds((WINDOW - dec) * KV + h, dec, stride=KV), :] = qkv_ref[:, c_kw + h * d:c_kw + (h + 1) * d]
            wvo_ref[pl.ds((WINDOW - dec) * KV + h, dec, stride=KV), :] = qkv_ref[:, c_vw + h * d:c_vw + (h + 1) * d]
        wko_ref[0:(WINDOW - dec) * KV, :] = wk_ref[dec * KV:WINDOW * KV, :]
        wvo_ref[0:(WINDOW - dec) * KV, :] = wv_ref[dec * KV:WINDOW * KV, :]


def _odd_sample_attn(qkv32, gates, gate_block0, k_cmp, v_cmp, win_k, win_v, cache_k, cache_v, page_table, dec):
    n_seq, n_pages = page_table.shape
    past_len = n_pages * LANE
    p_cnt = PAGES_PER_STEP
    d = HEAD_DIM
    gc = C_HEADS // C_KV
    rows = gc * dec
    cmp_spec = pl.BlockSpec((None, C_KV, past_len // CMP_BLOCK, d), lambda s, c, pt: (s, 0, 0, 0))
    win_spec = pl.BlockSpec((None, WINDOW * KV, d), lambda s, c, pt: (s, 0, 0))
    in_specs = [
        pl.BlockSpec(memory_space=pltpu.SMEM),
        pl.BlockSpec((dec, qkv32.shape[1]), lambda s, c, pt: (s, 0)),
        pl.BlockSpec((dec, KV_COLS), lambda s, c, pt: (gate_block0 + s, 0)),
        cmp_spec, cmp_spec, win_spec, win_spec,
    ] + [_page_spec(p_cnt, i) for i in range(p_cnt)] * 2
    grid_spec = pltpu.PrefetchScalarGridSpec(
        num_scalar_prefetch=1,
        grid=(n_seq, n_pages // p_cnt),
        in_specs=in_specs,
        out_specs=(pl.BlockSpec((dec, C_HEADS * d), lambda s, c, pt: (s, 0)), win_spec, win_spec),
        scratch_shapes=[
            pltpu.VMEM((C_KV, rows, d), BF16),
            pltpu.VMEM((C_KV, dec, 2 * LANE), F32),
            pltpu.VMEM((C_KV, rows, d), F32),
            pltpu.VMEM((C_KV, rows, LANE), F32),
            pltpu.VMEM((C_KV, rows, 2 * d), F32),
        ],
    )
    win_shape = jax.ShapeDtypeStruct((n_seq, WINDOW * KV, d), F32)
    return pl.pallas_call(
        functools.partial(_odd_sample_body, n_pages=n_pages, past_len=past_len, dec=dec),
        grid_spec=grid_spec,
        out_shape=(jax.ShapeDtypeStruct((n_seq * dec, C_HEADS * d), F32), win_shape, win_shape),
        compiler_params=_cparams(("parallel", "arbitrary")),
    )(page_table, _alibi_slopes(C_HEADS), qkv32, gates, k_cmp, v_cmp, win_k, win_v,
      *([cache_k] * p_cnt), *([cache_v] * p_cnt))


def _xattn_body(q_ref, k_ref, v_ref, o_ref):
    d = HEAD_DIM
    scale = d ** -0.5
    tq = q_ref.shape[0]
    n_mem = k_ref.shape[0] // KV
    pad_rows = (-tq) % 16
    for h in range(X_HEADS):
        cols = slice(h * d, (h + 1) * d)
        q = q_ref[:, cols]
        if pad_rows:
            q = jnp.concatenate([q, jnp.zeros((pad_rows, d), F32)], axis=0)
        s = _dot_nt(q.astype(BF16), _head_rows(k_ref, h, n_mem).astype(BF16)) * scale
        o = _softmax_apply(s, _with_ones(_head_rows(v_ref, h, n_mem).astype(BF16)))
        o_ref[:, cols] = o[0:tq].astype(o_ref.dtype)


def _xattn(q, mem_k, mem_v, layer, row_block0, tq, n_b, nq):
    width = X_HEADS * HEAD_DIM
    mem_spec = pl.BlockSpec((None, None, mem_k.shape[2], HEAD_DIM), lambda b, i: (layer, b, 0, 0))
    return pl.pallas_call(
        _xattn_body,
        grid=(n_b, nq),
        in_specs=[pl.BlockSpec((tq, width), lambda b, i: (row_block0 + b * nq + i, 0)), mem_spec, mem_spec],
        out_specs=pl.BlockSpec((tq, width), lambda b, i: (b * nq + i, 0)),
        out_shape=jax.ShapeDtypeStruct((n_b * nq * tq, width), F32),
        compiler_params=_cparams(("parallel", "arbitrary")),
    )(q, mem_k, mem_v)


def _router_body(x_ref, g_ref, w_ref, b_ref, o_ref):
    x = x_ref[...]
    hn = x * lax.rsqrt(jnp.mean(x * x, axis=-1, keepdims=True) + EPS) * g_ref[...]
    logits = jnp.dot(hn, w_ref[...], preferred_element_type=F32, precision=lax.Precision.HIGHEST) + b_ref[...]
    lane = lax.broadcasted_iota(jnp.int32, logits.shape, 1).astype(F32)
    big = float(2 * LANE)
    ninf = -jnp.inf

    def first_max(v):
        m = jnp.max(v, axis=-1, keepdims=True)
        return m, jnp.min(jnp.where(v == m, lane, big), axis=-1, keepdims=True)

    lgm = jnp.where(lane < N_GROUPS, logits, ninf)
    mg, gi = first_max(lgm)
    pg_top = 1.0 / jnp.sum(jnp.exp(lgm - mg), axis=-1, keepdims=True)
    lo = N_GROUPS + gi * EXP_PER_GROUP
    lem = jnp.where((lane >= lo) & (lane < lo + EXP_PER_GROUP), logits, ninf)
    m1, i1 = first_max(lem)
    lem2 = jnp.where(lane == i1, ninf, lem)
    m2, i2 = first_max(lem2)
    z = jnp.sum(jnp.exp(lem - m1), axis=-1, keepdims=True)
    pe1 = 1.0 / z
    pe2 = jnp.exp(m2 - m1) / z
    w1 = pg_top * pe1 / (pe1 + pe2)
    w2 = pg_top * pe2 / (pe1 + pe2)
    e1 = i1 - N_GROUPS
    e2 = i2 - N_GROUPS
    o_ref[...] = jnp.where(lane == 0, e1, jnp.where(lane == 1, e2, jnp.where(lane == 2, w1, jnp.where(lane == 3, w2, 0.0))))


def _router(x, g, w_grp, b_grp, w_er, b_er, tm):
    n, d = x.shape
    pad = LANE - N_GROUPS - N_EXPERTS
    w = jnp.pad(jnp.concatenate([w_grp, w_er], axis=1).astype(F32), ((0, 0), (0, pad)))
    b = jnp.pad(jnp.concatenate([b_grp, b_er]).astype(F32), (0, pad)).reshape(1, LANE)
    return pl.pallas_call(
        _router_body,
        grid=(n // tm,),
        in_specs=[pl.BlockSpec((tm, d), lambda i: (i, 0)), pl.BlockSpec((1, d), lambda i: (0, 0)),
                  pl.BlockSpec((d, LANE), lambda i: (0, 0)), pl.BlockSpec((1, LANE), lambda i: (0, 0))],
        out_specs=pl.BlockSpec((tm, LANE), lambda i: (i, 0)),
        out_shape=jax.ShapeDtypeStruct((n, LANE), F32),
        compiler_params=_cparams(("parallel",)),
    )(x, g.reshape(1, d).astype(F32), w, b)


def _moe_max_tiles(n_rows):
    return n_rows // MOE_TM + N_EXPERTS


ROW_ALIGN = 8


def _moe_rows(n):
    return 2 * n + ROW_ALIGN * N_EXPERTS


def _dispatch_tables(e1, e2):
    n = e1.shape[0]
    tm = MOE_TM
    max_tiles = _moe_max_tiles(2 * n)
    e = jnp.concatenate([e1, e2])
    tok = jnp.concatenate([jnp.arange(n, dtype=jnp.int32)] * 2)
    onehot = (e[:, None] == jnp.arange(N_EXPERTS, dtype=jnp.int32)[None, :]).astype(jnp.int32)
    counts = jnp.sum(onehot, axis=0)
    rank = jnp.sum((jnp.cumsum(onehot, axis=0) - onehot) * onehot, axis=1)
    aligned = ((counts + ROW_ALIGN - 1) // ROW_ALIGN) * ROW_ALIGN
    starts = jnp.cumsum(aligned) - aligned
    pos = jnp.sum(starts[None, :] * onehot, axis=1) + rank
    src = jnp.zeros((_moe_rows(n) + tm,), jnp.int32).at[pos].set(tok)
    tiles_per = (counts + tm - 1) // tm
    tile_ends = jnp.cumsum(tiles_per)
    n_used = tile_ends[-1].astype(jnp.int32)
    t = jnp.arange(max_tiles, dtype=jnp.int32)
    tile_e = jnp.minimum(jnp.sum((tile_ends[None, :] <= t[:, None]).astype(jnp.int32), axis=1), N_EXPERTS - 1)
    eh = (tile_e[:, None] == jnp.arange(N_EXPERTS, dtype=jnp.int32)[None, :]).astype(jnp.int32)
    pick = lambda v: jnp.sum(eh * v[None, :], axis=1)
    in_e = t - pick(tile_ends - tiles_per)
    used = t < n_used
    tile_start = jnp.where(used, pick(starts) + in_e * tm, 0).astype(jnp.int32)
    tile_rows = jnp.where(used, jnp.clip(pick(counts) - in_e * tm, 0, tm), 0).astype(jnp.int32)
    used_rows = jnp.stack([n_used, jnp.sum(aligned).astype(jnp.int32)])
    return tile_e.astype(jnp.int32), tile_start, tile_rows, used_rows, src, pos[:n], pos[n:]


ROW_GROUP = 8


def _row_copy(idx_ref, i, r, src_hbm, dst, sem):
    return pltpu.make_async_copy(src_hbm.at[pl.ds(idx_ref[i], 1)], dst.at[pl.ds(r, 1)], sem)


def _start_rows(idx_ref, base, n, src_hbm, dst, sem):
    def start(j, carry):
        for u in range(ROW_GROUP):
            r = j * ROW_GROUP + u
            _row_copy(idx_ref, base + r, r, src_hbm, dst, sem).start(priority=u % 2)
        return carry
    lax.fori_loop(0, (n + ROW_GROUP - 1) // ROW_GROUP, start, 0)


def _wait_rows(idx_ref, base, n, src_hbm, dst, sem):
    def wait(j, carry):
        for u in range(ROW_GROUP):
            r = j * ROW_GROUP + u
            _row_copy(idx_ref, base + r, r, src_hbm, dst, sem).wait()
        return carry
    lax.fori_loop(0, (n + ROW_GROUP - 1) // ROW_GROUP, wait, 0)


def _experts_body(te_ref, ts_ref, tr_ref, nu_ref, src_ref, x_hbm, g_ref, wg_hbm, wu_hbm, wd_hbm, y_hbm,
                  xbuf, ybuf, hn_ref, wgb, wub, wdb, gsem, osem, wsem, *, layer, n_rows):
    t = pl.program_id(0)
    n_used = nu_ref[0]
    slot = t % 2
    tm = MOE_TM
    n_chunks = wgb.shape[0]
    fc = wgb.shape[2]

    def gather(fn, tile, sl):
        fn(src_ref, ts_ref[tile], tr_ref[tile], x_hbm, xbuf.at[sl], gsem.at[sl])

    def w_copies(tile, c):
        e = te_ref[tile]
        cols = pl.ds(c * fc, fc)
        return (pltpu.make_async_copy(wg_hbm.at[layer, e, :, cols], wgb.at[c], wsem.at[c]),
                pltpu.make_async_copy(wu_hbm.at[layer, e, :, cols], wub.at[c], wsem.at[c]),
                pltpu.make_async_copy(wd_hbm.at[layer, e, cols, :], wdb.at[c], wsem.at[c]))

    def out_rows(row0, sl):
        if not isinstance(row0, int):
            row0 = pl.multiple_of(row0, ROW_ALIGN)
        return pltpu.make_async_copy(ybuf.at[sl], y_hbm.at[pl.ds(row0, tm)], osem.at[sl])

    def out_copy(tile, sl):
        return out_rows(ts_ref[tile], sl)

    @pl.when(t == 0)
    def _():
        xbuf[...] = jnp.zeros(xbuf.shape, F32)
        gather(_start_rows, 0, 0)
        for cp in w_copies(0, 0):
            cp.start()

    @pl.when(t < n_used)
    def _():
        @pl.when(t + 1 < n_used)
        def _():
            gather(_start_rows, t + 1, 1 - slot)

        gather(_wait_rows, t, slot)
        x = xbuf[slot]
        hn_ref[...] = (x * lax.rsqrt(jnp.mean(x * x, axis=-1, keepdims=True) + EPS) * g_ref[...]).astype(BF16)
        for c in range(n_chunks):
            if c + 1 < n_chunks:
                for cp in w_copies(t, c + 1):
                    cp.start()
            else:
                @pl.when(t + 1 < n_used)
                def _():
                    for cp in w_copies(t + 1, 0):
                        cp.start()
            for cp in w_copies(t, c):
                cp.wait()
            hn = hn_ref[...]
            gate = _dot(hn, wgb[c].astype(BF16))
            up = _dot(hn, wub[c].astype(BF16))
            act = (gate * jax.nn.sigmoid(gate) * up).astype(BF16)
            part = _dot(act, wdb[c].astype(BF16))
            if c == 0:
                ybuf[slot] = part
            else:
                ybuf[slot] += part

        @pl.when(t >= 1)
        def _():
            out_copy(t - 1, 1 - slot).wait()

        out_copy(t, slot).start()

        @pl.when(t == n_used - 1)
        def _():
            out_copy(t, slot).wait()
            ybuf[1 - slot] = jnp.zeros(ybuf.shape[1:], F32)
            for row0 in (nu_ref[1], n_rows):
                tail = out_rows(row0, 1 - slot)
                tail.start()
                tail.wait()


def _experts(tile_e, tile_start, tile_rows, n_used, src, x, g, w_gate, w_up, w_down, layer):
    n_rows = src.shape[0] - MOE_TM
    assert MOE_TM >= ROW_ALIGN * N_EXPERTS
    d = x.shape[1]
    f = w_gate.shape[3]
    tm = MOE_TM
    fc = f // 2
    assert f % (2 * LANE) == 0
    grid_spec = pltpu.PrefetchScalarGridSpec(
        num_scalar_prefetch=5,
        grid=(tile_e.shape[0],),
        in_specs=[
            pl.BlockSpec(memory_space=pl.ANY),
            pl.BlockSpec((1, d), lambda t, *_: (0, 0)),
            pl.BlockSpec(memory_space=pl.ANY), pl.BlockSpec(memory_space=pl.ANY), pl.BlockSpec(memory_space=pl.ANY),
        ],
        out_specs=pl.BlockSpec(memory_space=pl.ANY),
        scratch_shapes=[pltpu.VMEM((2, tm, d), F32), pltpu.VMEM((2, tm, d), F32), pltpu.VMEM((tm, d), BF16),
                        pltpu.VMEM((2, d, fc), F32), pltpu.VMEM((2, d, fc), F32), pltpu.VMEM((2, fc, d), F32),
                        pltpu.SemaphoreType.DMA((2,)), pltpu.SemaphoreType.DMA((2,)), pltpu.SemaphoreType.DMA((2,))],
    )
    return pl.pallas_call(
        functools.partial(_experts_body, layer=layer, n_rows=n_rows),
        grid_spec=grid_spec,
        out_shape=jax.ShapeDtypeStruct((n_rows + tm, d), F32),
        compiler_params=_cparams(("arbitrary",)),
    )(tile_e, tile_start, tile_rows, n_used, src, x, g.reshape(1, d).astype(F32), w_gate, w_up, w_down)


def _combine_body(p1_ref, p2_ref, x_ref, r_ref, y_hbm, o_ref, ybuf, sem):
    i = pl.program_id(0)
    n = pl.num_programs(0)
    tm = x_ref.shape[0]
    slot = i % 2

    def start(step, sl):
        _start_rows(p1_ref, step * tm, tm, y_hbm, ybuf.at[sl, 0], sem.at[sl])
        _start_rows(p2_ref, step * tm, tm, y_hbm, ybuf.at[sl, 1], sem.at[sl])

    @pl.when(i == 0)
    def _():
        start(0, 0)

    @pl.when(i + 1 < n)
    def _():
        start(i + 1, 1 - slot)

    _wait_rows(p1_ref, i * tm, tm, y_hbm, ybuf.at[slot, 0], sem.at[slot])
    _wait_rows(p2_ref, i * tm, tm, y_hbm, ybuf.at[slot, 1], sem.at[slot])
    r = r_ref[...]
    o_ref[...] = x_ref[...] + r[:, 2:3] * ybuf[slot, 0] + r[:, 3:4] * ybuf[slot, 1]


def _combine(p1, p2, x, r, y, tm):
    n, d = x.shape
    grid_spec = pltpu.PrefetchScalarGridSpec(
        num_scalar_prefetch=2,
        grid=(n // tm,),
        in_specs=[
            pl.BlockSpec((tm, d), lambda i, a, b: (i, 0)),
            pl.BlockSpec((tm, LANE), lambda i, a, b: (i, 0)),
            pl.BlockSpec(memory_space=pl.ANY),
        ],
        out_specs=pl.BlockSpec((tm, d), lambda i, a, b: (i, 0)),
        scratch_shapes=[pltpu.VMEM((2, 2, tm, d), F32), pltpu.SemaphoreType.DMA((2,))],
    )
    return pl.pallas_call(
        _combine_body,
        grid_spec=grid_spec,
        out_shape=jax.ShapeDtypeStruct((n, d), F32),
        compiler_params=_cparams(("arbitrary",)),
    )(p1, p2, x, r, y)


def _moe(x, g, w_grp, b_grp, w_er, b_er, w_gate, w_up, w_down, layer, tm):
    n = x.shape[0]
    r = _router(x, g, w_grp, b_grp, w_er, b_er, tm)
    e1 = r[:, 0].astype(jnp.int32)
    e2 = r[:, 1].astype(jnp.int32)
    tile_e, tile_start, tile_rows, n_used, src, p1, p2 = _dispatch_tables(e1, e2)
    y = _experts(tile_e, tile_start, tile_rows, n_used, src, x, g, w_gate, w_up, w_down, layer)
    return _combine(p1, p2, x, r, y, COMBINE_TM)


def _rmsnorm_body(x_ref, g_ref, o_ref):
    x = x_ref[...]
    o_ref[...] = x * lax.rsqrt(jnp.mean(x * x, axis=-1, keepdims=True) + EPS) * g_ref[...]


def _rmsnorm(x, g, m, row_off, tm):
    d = x.shape[1]
    return pl.pallas_call(
        _rmsnorm_body,
        grid=(m // tm,),
        in_specs=[pl.BlockSpec((tm, d), lambda i: (i + row_off, 0)), pl.BlockSpec((1, d), lambda i: (0, 0))],
        out_specs=pl.BlockSpec((tm, d), lambda i: (i, 0)),
        out_shape=jax.ShapeDtypeStruct((m, d), F32),
        compiler_params=_cparams(("parallel",)),
    )(x, g.reshape(1, d).astype(F32))


def _row_tile(n, cands=(768, 512, 256, 128)):
    for tm in cands:
        if n % tm == 0:
            return tm
    raise ValueError(f"row count {n} has no supported row tile")


def kernel(x_prompt, x_sample, mem_prompt, cache_dif_k, cache_dif_v, cache_fox_k, cache_fox_v, cache_fox_logf, cache_cmp_k, cache_cmp_v, cache_slc_k, cache_slc_v, state_win_k, state_win_v, cache_mem_k, cache_mem_v, page_table, g_mix, w_in_e, b_forget, lam_q1, lam_k1, lam_q2, lam_k2, g_dif_head, w_out_e, w_in_o, cmp_pos_k, cmp_pos_v, w_out_o, g_xattn, g_mem, w_xq, w_xk, w_xv, w_xo, g_ffn, w_grp, b_grp, w_er, b_er, w_gate, w_up, w_down, g_final):
    n_b, seq, d = x_prompt.shape
    n_s, dec, _ = x_sample.shape
    n_mem = mem_prompt.shape[1]
    depth = g_mix.shape[0]
    n_pages = page_table.shape[1]
    past_len = n_pages * LANE
    n_p = n_b * seq
    n_smp = n_s * dec
    n_tok = n_p + n_smp
    hd = HEAD_DIM
    tmj = _row_tile(n_tok)
    tmp = _row_tile(n_p, (512, 256, 128))
    tms = _row_tile(n_smp, (256, 128))
    assert seq % EP_T == 0 and seq % 1024 == 0 and n_p % tms == 0 and n_pages % PAGES_PER_STEP == 0
    assert cache_dif_k.shape[2] == LANE and state_win_k.shape[2] == WINDOW and n_p % dec == 0

    x = jnp.concatenate([x_prompt.reshape(n_p, d), x_sample.reshape(n_smp, d)], axis=0)
    head_rows = lambda a: a.reshape(a.shape[0], a.shape[1] * KV, hd)
    kv5 = lambda a, lead: a.reshape(*lead, KV, hd)

    mem_k_s = cache_mem_k.reshape(depth, n_s, n_mem * KV, hd)
    mem_v_s = cache_mem_v.reshape(depth, n_s, n_mem * KV, hd)

    mem = mem_prompt.reshape(n_b * n_mem, d)
    mem_k_p, mem_v_p = [], []
    for l in range(depth):
        w_kv = jnp.concatenate([w_xk[l], w_xv[l]], axis=1).astype(BF16)
        mk, mv = _nmm(mem, w_kv, g=g_mem[l], out_dtypes=(), kv_cols=(0, 1),
                      tm=_row_tile(n_b * n_mem, (512, 256, 128)), tn=KV_COLS)
        mem_k_p.append(mk)
        mem_v_p.append(mv)

    even_p, even_s, odd_p, odd_s = [], [], [], []
    for l in range(depth):
        j = l // 2
        if l % 2 == 0:
            w_main = w_in_e[j][:, :E_MAIN].astype(BF16)
            w_fg = jnp.pad(w_in_e[j][:, E_MAIN:], ((0, 0), (0, LANE - F_HEADS))).astype(BF16)
            kvc = (2, 3, 6, 7)
            qkv16, *kv_p = _nmm(x, w_main, g=g_mix[l], out_dtypes=(BF16,), kv_cols=kvc, m=n_p, tm=tmp, tn=KV_COLS)
            qkv32, *kv_s = _nmm(x, w_main, g=g_mix[l], out_dtypes=(F32,), kv_cols=kvc, m=n_smp,
                                row_off=n_p // tms, tm=tms, tn=KV_COLS)
            logf = _nmm(x, w_fg, g=g_mix[l], bias=jnp.pad(b_forget[j], (0, LANE - F_HEADS)), tm=tmj, tn=LANE)
            logf = logf[:, :F_HEADS]
            lf_p = logf[:n_p].reshape(n_b, seq, F_HEADS)
            lf_s = logf[n_p:].reshape(n_s, dec, F_HEADS)
            ck_p = _cumsum(lf_p.transpose(0, 2, 1)).reshape(n_b, F_KV, F_HEADS // F_KV, seq)
            lf_past = cache_fox_logf[j].transpose(0, 2, 1)[page_table]
            lf_past = lf_past.transpose(0, 2, 1, 3).reshape(n_s, F_HEADS, past_len)
            lf_new = jnp.pad(lf_s.transpose(0, 2, 1), ((0, 0), (0, 0), (0, LANE - dec)))
            ck_s = _cumsum(jnp.concatenate([lf_past, lf_new], axis=2))
            lam_p = jnp.stack([lam_q1[j], lam_k1[j], lam_q2[j], lam_k2[j]]).astype(F32)
            lam_init = _lambda_init(l)
            o_p = _even_prompt_attn(qkv16, ck_p, lam_p, g_dif_head[j], n_b, seq, lam_init)
            caches = [head_rows(c[j]) for c in (cache_dif_k, cache_dif_v, cache_fox_k, cache_fox_v)]
            o_s = _even_sample_attn(qkv32, caches, page_table, ck_s, lam_p, g_dif_head[j], lam_init, dec)
            w_out = w_out_e[j][_even_out_perm()].astype(BF16)
            even_p.append([kv5(a, (n_b, seq)) for a in kv_p] + [lf_p])
            even_s.append([kv5(a, (n_s, dec)) for a in kv_s] + [lf_s])
        else:
            w_main = w_in_o[j][:, :O_MAIN].astype(BF16)
            w_g = w_in_o[j][:, O_MAIN:].reshape(d, C_KV, 3 * C_HEADS // C_KV)
            w_g = jnp.pad(w_g, ((0, 0), (0, 0), (0, LANE - w_g.shape[2]))).reshape(d, C_KV * LANE).astype(BF16)
            qkv16, *kv_p = _nmm(x, w_main, g=g_mix[l], out_dtypes=(BF16,), kv_cols=(4, 5, 6, 7, 8, 9),
                                m=n_p, tm=tmp, tn=KV_COLS)
            qkv32, *kv_s = _nmm(x, w_main, g=g_mix[l], out_dtypes=(F32,), kv_cols=(4, 5, 6, 7), m=n_smp,
                                row_off=n_p // tms, tm=tms, tn=KV_COLS)
            gates = _nmm(x, w_g, g=g_mix[l], tm=tmj, tn=KV_COLS)
            wk, wv = _pos_softmax(cmp_pos_k[j], cmp_pos_v[j])
            kc_p, vc_p = _compress_rows(qkv16, wk, wv, n_b, seq, 4, 5)
            kc_s, vc_s = _compress_paged(head_rows(cache_cmp_k[j]), head_rows(cache_cmp_v[j]), page_table, wk, wv)
            o_p = _odd_prompt_attn(qkv16, kc_p, vc_p, gates, n_b, seq)
            o_s, win_k, win_v = _odd_sample_attn(
                qkv32, gates, n_p // dec, kc_s, vc_s, head_rows(state_win_k[j]), head_rows(state_win_v[j]),
                head_rows(cache_slc_k[j]), head_rows(cache_slc_v[j]), page_table, dec)
            w_out = w_out_o[j].astype(BF16)
            n_keep = min(WINDOW, seq)
            rows_p = [kv5(a, (n_b, seq)) for a in kv_p]
            odd_p.append(rows_p[:4] + [a[:, seq - n_keep:] for a in rows_p[4:]])
            odd_s.append([kv5(a, (n_s, dec)) for a in kv_s] + [kv5(win_k, (n_s, WINDOW)), kv5(win_v, (n_s, WINDOW))])
        o = jnp.concatenate([o_p, o_s.astype(BF16)], axis=0)
        x = _nmm(o, w_out, res=x, tm=tmj, tn=512)

        q = _nmm(x, w_xq[l].astype(BF16), g=g_xattn[l], tm=tmj, tn=512)
        ox_p = _xattn(q, mem_k_p[l].reshape(1, n_b, n_mem * KV, hd), mem_v_p[l].reshape(1, n_b, n_mem * KV, hd),
                      0, 0, 512, n_b, seq // 512)
        ox_s = _xattn(q, mem_k_s, mem_v_s, l, n_p // dec, dec, n_s, 1)
        x = _nmm(jnp.concatenate([ox_p, ox_s], axis=0), w_xo[l].astype(BF16), res=x, tm=tmj, tn=512)

        x = _moe(x, g_ffn[l], w_grp[l], b_grp[l], w_er[l], b_er[l], w_gate, w_up, w_down, l, tmj)

    y_prompt = _rmsnorm(x, g_final, n_p, 0, tmp).reshape(n_b, seq, d)
    y_sample = _rmsnorm(x, g_final, n_smp, n_p // tms, tms).reshape(n_s, dec, d)
    stack = lambda rows: [jnp.stack(a) for a in zip(*rows)]
    mem_k = jnp.stack([kv5(a, (n_b, n_mem)) for a in mem_k_p])
    mem_v = jnp.stack([kv5(a, (n_b, n_mem)) for a in mem_v_p])
    return (y_prompt, y_sample, *stack(even_p), *stack(odd_p), mem_k, mem_v, *stack(even_s), *stack(odd_s))
```

```python
import functools
import math

import numpy as np
import jax
import jax.numpy as jnp
from jax import lax
from jax.experimental import pallas as pl
from jax.experimental.pallas import tpu as pltpu

F32 = jnp.float32
BF16 = jnp.bfloat16

HEAD_DIM = 128
A_HEADS, A_KV = 8, 4
F_HEADS, F_KV = 8, 4
C_HEADS, C_KV = 16, 4
KV = 4
CMP_BLOCK = 64
SEL_BLOCK = 64
SEL_SHIFT = 6
TOP_N = 16
WINDOW = 512
X_HEADS = 4
N_GROUPS = 4
EXP_PER_GROUP = 8
N_EXPERTS = N_GROUPS * EXP_PER_GROUP
EPS = 1e-6
NEG = -1e30
FORCE = 1e4
E_MAIN = (A_HEADS + 2 * A_KV + F_HEADS + 2 * F_KV) * HEAD_DIM
O_MAIN = (C_HEADS + 6 * C_KV) * HEAD_DIM

LANE = 128
VMEM_LIMIT = 56 * 1024 * 1024
PAGES_PER_STEP = 16
MOE_TM = 768
COMBINE_TM = 256
KV_COLS = KV * HEAD_DIM


def _cparams(sem):
    return pltpu.CompilerParams(dimension_semantics=sem, vmem_limit_bytes=VMEM_LIMIT)


def _dot_nt(a, b):
    return lax.dot_general(a, b, (((1,), (1,)), ((), ())), preferred_element_type=F32)


def _dot(a, b):
    return jnp.dot(a, b, preferred_element_type=F32)


def _alibi_slopes(n_heads):
    return jnp.asarray(np.exp2(-8.0 * np.arange(1, n_heads + 1) / n_heads).astype(np.float32))


def _lambda_init(layer):
    return 0.8 - 0.6 * math.exp(-0.3 * layer)


def _head_rows(ref, h, n, row0=0):
    return ref[pl.ds(row0 * KV + h, n, stride=KV), :]


def _nmm_body(*refs, norm, epilogue, n_out, kv_cols):
    it = iter(refs)
    a_ref = next(it)
    g_ref = next(it) if norm else None
    b_ref = next(it)
    e_ref = next(it) if epilogue is not None else None
    outs = [next(it) for _ in range(n_out)]
    kv_refs = [next(it) for _ in kv_cols]
    as_ref = next(it)
    tm = a_ref.shape[0]

    @pl.when(pl.program_id(1) == 0)
    def _():
        a = a_ref[...].astype(F32)
        if norm:
            a = a * lax.rsqrt(jnp.mean(a * a, axis=-1, keepdims=True) + EPS) * g_ref[...]
        as_ref[...] = a.astype(BF16)

    acc = _dot(as_ref[...], b_ref[...])
    if epilogue == "res":
        acc = acc + e_ref[...]
    elif epilogue == "logsig":
        z = acc + e_ref[...]
        acc = jnp.minimum(z, 0.0) - jnp.log1p(jnp.exp(-jnp.abs(z)))
    for o in outs:
        o[...] = acc.astype(o.dtype)
    for ref, jcol in zip(kv_refs, kv_cols):
        @pl.when(pl.program_id(1) == jcol)
        def _(ref=ref):
            for h in range(KV):
                ref[pl.ds(h, tm, stride=KV), :] = acc[:, h * HEAD_DIM:(h + 1) * HEAD_DIM]


def _nmm(a, b, *, g=None, res=None, bias=None, out_dtypes=(F32,), kv_cols=(), m=None, row_off=0, tm, tn):
    k = a.shape[1]
    m = a.shape[0] if m is None else m
    n = b.shape[1]
    assert m % tm == 0 and n % tn == 0 and (not kv_cols or tn == KV_COLS)
    norm = g is not None
    epilogue = "res" if res is not None else ("logsig" if bias is not None else None)
    ins = [a]
    specs = [pl.BlockSpec((tm, k), lambda i, j: (i + row_off, 0))]
    if norm:
        ins.append(g.reshape(1, k).astype(F32))
        specs.append(pl.BlockSpec((1, k), lambda i, j: (0, 0)))
    ins.append(b)
    specs.append(pl.BlockSpec((k, tn), lambda i, j: (0, j)))
    if epilogue == "res":
        ins.append(res)
        specs.append(pl.BlockSpec((tm, tn), lambda i, j: (i + row_off, j)))
    elif epilogue == "logsig":
        ins.append(bias.reshape(1, n).astype(F32))
        specs.append(pl.BlockSpec((1, tn), lambda i, j: (0, j)))
    out_shape = [jax.ShapeDtypeStruct((m, n), dt) for dt in out_dtypes]
    out_specs = [pl.BlockSpec((tm, tn), lambda i, j: (i, j)) for _ in out_dtypes]
    out_shape += [jax.ShapeDtypeStruct((m * KV, HEAD_DIM), F32) for _ in kv_cols]
    out_specs += [pl.BlockSpec((tm * KV, HEAD_DIM), lambda i, j: (i, 0)) for _ in kv_cols]
    outs = pl.pallas_call(
        functools.partial(_nmm_body, norm=norm, epilogue=epilogue, n_out=len(out_dtypes), kv_cols=tuple(kv_cols)),
        grid=(m // tm, n // tn),
        in_specs=specs,
        out_specs=out_specs,
        out_shape=out_shape,
        scratch_shapes=[pltpu.VMEM((tm, k), BF16)],
        compiler_params=_cparams(("parallel", "arbitrary")),
    )(*ins)
    return outs[0] if len(outs) == 1 else outs


def _cumsum_body(x_ref, o_ref):
    nbk, h, l = x_ref.shape
    r = lax.broadcasted_iota(jnp.int32, (LANE, LANE), 0)
    c = lax.broadcasted_iota(jnp.int32, (LANE, LANE), 1)
    upper = (r <= c).astype(F32)
    carry = jnp.zeros((nbk * h, 1), F32)
    for j in range(l // LANE):
        blk = x_ref[:, :, j * LANE:(j + 1) * LANE].reshape(nbk * h, LANE)
        cs = jnp.dot(blk, upper, preferred_element_type=F32, precision=lax.Precision.HIGHEST)
        o_ref[:, :, j * LANE:(j + 1) * LANE] = (cs + carry).reshape(nbk, h, LANE)
        carry = carry + cs[:, LANE - 1:LANE]


def _cumsum(x):
    nb, h, l = x.shape
    nbk = math.gcd(nb, 8)
    return pl.pallas_call(
        _cumsum_body,
        grid=(nb // nbk,),
        in_specs=[pl.BlockSpec((nbk, h, l), lambda i: (i, 0, 0))],
        out_specs=pl.BlockSpec((nbk, h, l), lambda i: (i, 0, 0)),
        out_shape=jax.ShapeDtypeStruct((nb, h, l), F32),
        compiler_params=_cparams(("parallel",)),
    )(x)


def _with_ones(v):
    return jnp.concatenate([v, jnp.ones_like(v)], axis=1)


def _online_update(s, v_ext, m_ref, acc_ref, idx):
    m_prev = m_ref[idx]
    m_new = jnp.maximum(m_prev, jnp.max(s, axis=-1, keepdims=True))
    alpha = jnp.exp(m_prev - m_new)
    p = jnp.exp(s - jnp.tile(m_new, (1, s.shape[1] // LANE)))
    acc_ref[idx] = jnp.tile(alpha, (1, 2)) * acc_ref[idx] + _dot(p.astype(BF16), v_ext)
    m_ref[idx] = m_new


def _flash_out(acc):
    d = acc.shape[1] // 2
    return acc[:, :d] / acc[:, d:]


def _softmax_rows(s):
    e = jnp.exp(s - jnp.max(s, axis=-1, keepdims=True))
    return e / jnp.sum(e, axis=-1, keepdims=True)


def _softmax_apply(s, v_ext):
    e = jnp.exp(s - jnp.max(s, axis=-1, keepdims=True))
    return _flash_out(_dot(e.astype(BF16), v_ext))


def _lambda_value(lp_ref, lam_init):
    lp = lp_ref[...]
    a = jnp.sum(lp[0:1] * lp[1:2], axis=-1, keepdims=True)
    b = jnp.sum(lp[2:3] * lp[3:4], axis=-1, keepdims=True)
    return jnp.exp(a) - jnp.exp(b) + lam_init


def _dif_head_out(acc1, acc2, lam, gh, lam_init):
    o = _flash_out(acc1) - lam * _flash_out(acc2)
    o = o * lax.rsqrt(jnp.mean(o * o, axis=-1, keepdims=True) + EPS) * gh
    return o * (1.0 - lam_init)


def _row_select(row, step, vals):
    out = vals[-1]
    for g in range(len(vals) - 2, -1, -1):
        out = jnp.where(row < (g + 1) * step, vals[g], out)
    return out


EP_T = 512


def _even_prompt_body(sl_ref, qa_ref, ka_ref, va_ref, qf_ref, kf_ref, vf_ref, ck_ref, lp_ref, gh_ref,
                      o_ref, qs_ref, m_ref, acc_ref, *, lam_init):
    h = pl.program_id(1)
    qi = pl.program_id(2)
    t = EP_T
    d = HEAD_DIM
    lane = lax.broadcasted_iota(jnp.int32, (t, d), 1)
    zero = jnp.zeros((t, d), BF16)
    for g in range(2):
        q = qa_ref[:, g * d:(g + 1) * d] * jnp.asarray(0.125, BF16)
        qs_ref[g, 0] = jnp.where(lane < d // 2, q, zero)
        qs_ref[g, 1] = jnp.where(lane >= d // 2, q, zero)
        qs_ref[g, 2] = qf_ref[:, g * d:(g + 1) * d]
    m_ref[...] = jnp.full(m_ref.shape, NEG, F32)
    acc_ref[...] = jnp.zeros(acc_ref.shape, F32)
    scale_f = d ** -0.5

    def tile(ki, diag):
        k0 = pl.multiple_of(ki * t, t)
        ka = ka_ref[pl.ds(k0, t), :]
        va = _with_ones(va_ref[pl.ds(k0, t), :])
        kf = kf_ref[pl.ds(k0, t), :]
        vf = _with_ones(vf_ref[pl.ds(k0, t), :])
        rel = (k0 - qi * t) + lax.broadcasted_iota(jnp.int32, (1, t), 1)
        relf = rel.astype(F32)
        if diag:
            mask = rel <= lax.broadcasted_iota(jnp.int32, (t, 1), 0)
        for g in range(2):
            bias_a = sl_ref[2 * h + g] * relf
            bias_f = -ck_ref[g:g + 1, pl.ds(k0, t)]
            for kind in range(3):
                if kind < 2:
                    s = _dot_nt(qs_ref[g, kind], ka) + bias_a
                    v = va
                else:
                    s = _dot_nt(qs_ref[g, 2], kf) * scale_f + bias_f
                    v = vf
                if diag:
                    s = jnp.where(mask, s, NEG)
                _online_update(s, v, m_ref, acc_ref, (g, kind))

    def body(ki, carry):
        tile(ki, False)
        return carry

    lax.fori_loop(0, qi, body, 0)
    tile(qi, True)

    lam = _lambda_value(lp_ref, lam_init)
    gh = gh_ref[...]
    for g in range(2):
        od = _dif_head_out(acc_ref[g, 0], acc_ref[g, 1], lam, gh, lam_init)
        of = _flash_out(acc_ref[g, 2])
        o_ref[:, g * d:(g + 1) * d] = od.astype(o_ref.dtype)
        o_ref[:, (2 + g) * d:(3 + g) * d] = of.astype(o_ref.dtype)


def _even_prompt_attn(qkv16, ck, lam_p, g_head, n_batch, seq, lam_init):
    t = EP_T
    nq = seq // t
    d = HEAD_DIM
    kv_spec = lambda col0: pl.BlockSpec((seq, d), lambda b, h, qi: (b, col0 + h))
    q_spec = lambda col0: pl.BlockSpec((t, 2 * d), lambda b, h, qi: (b * nq + qi, col0 + h))
    return pl.pallas_call(
        functools.partial(_even_prompt_body, lam_init=lam_init),
        grid=(n_batch, A_KV, nq),
        in_specs=[
            pl.BlockSpec(memory_space=pltpu.SMEM),
            q_spec(0),
            kv_spec(8), kv_spec(12),
            q_spec(8),
            kv_spec(24), kv_spec(28),
            pl.BlockSpec((None, None, 2, seq), lambda b, h, qi: (b, h, 0, 0)),
            pl.BlockSpec((4, d // 2), lambda b, h, qi: (0, 0)),
            pl.BlockSpec((1, d), lambda b, h, qi: (0, 0)),
        ],
        out_specs=pl.BlockSpec((t, 4 * d), lambda b, h, qi: (b * nq + qi, h)),
        scratch_shapes=[
            pltpu.VMEM((2, 3, t, d), BF16),
            pltpu.VMEM((2, 3, t, LANE), F32),
            pltpu.VMEM((2, 3, t, 2 * d), F32),
        ],
        out_shape=jax.ShapeDtypeStruct((n_batch * seq, 4 * A_KV * d), BF16),
        compiler_params=_cparams(("parallel", "parallel", "arbitrary")),
    )(_alibi_slopes(A_HEADS), qkv16, qkv16, qkv16, qkv16, qkv16, qkv16, ck, lam_p, g_head.reshape(1, d))


def _even_sample_body(*refs, n_pages, past_len, dec, lam_init):
    p_cnt = PAGES_PER_STEP
    pt_ref, sl_ref, qkv_ref, ck_ref, ckn_ref, lp_ref, gh_ref = refs[:7]
    pages = refs[7:7 + 4 * p_cnt]
    dk, dv, fk, fv = (pages[i * p_cnt:(i + 1) * p_cnt] for i in range(4))
    o_ref, qs_ref, m_ref, acc_ref = refs[7 + 4 * p_cnt:]
    del pt_ref
    c = pl.program_id(1)
    d = HEAD_DIM
    rows = 2 * dec
    row = lax.broadcasted_iota(jnp.int32, (rows, 1), 0)
    first = row < dec
    scale_f = d ** -0.5
    lane = lax.broadcasted_iota(jnp.int32, (rows, d), 1)

    def stacked(col0):
        return jnp.concatenate([qkv_ref[:, col0:col0 + d], qkv_ref[:, col0 + d:col0 + 2 * d]], axis=0)

    @pl.when(c == 0)
    def _():
        for h in range(A_KV):
            q = (stacked(h * 2 * d) * 0.125).astype(BF16)
            zero = jnp.zeros_like(q)
            qs_ref[h, 0] = jnp.where(lane < d // 2, q, zero)
            qs_ref[h, 1] = jnp.where(lane >= d // 2, q, zero)
            qs_ref[h, 2] = stacked(2048 + h * 2 * d).astype(BF16)
        m_ref[...] = jnp.full(m_ref.shape, NEG, F32)
        acc_ref[...] = jnp.zeros(acc_ref.shape, F32)

    def head_keys(prefs, h):
        return jnp.concatenate([_head_rows(r, h, LANE) for r in prefs], axis=0).astype(BF16)

    n_keys = p_cnt * LANE
    relf = (c * n_keys - past_len + lax.broadcasted_iota(jnp.int32, (1, n_keys), 1)).astype(F32)

    def slope_col(h):
        return jnp.where(first, sl_ref[2 * h], sl_ref[2 * h + 1])

    def ck_rows(ref, h):
        return jnp.where(first, ref[2 * h:2 * h + 1, :], ref[2 * h + 1:2 * h + 2, :])

    for h in range(A_KV):
        ka, kf = head_keys(dk, h), head_keys(fk, h)
        va, vf = _with_ones(head_keys(dv, h)), _with_ones(head_keys(fv, h))
        bias_a = slope_col(h) * relf
        bias_f = -ck_rows(ck_ref, h)
        for kind in range(3):
            if kind < 2:
                s = _dot_nt(qs_ref[h, kind], ka) + bias_a
                v = va
            else:
                s = _dot_nt(qs_ref[h, 2], kf) * scale_f + bias_f
                v = vf
            _online_update(s, v, m_ref, acc_ref, (h, kind))

    @pl.when(c == n_pages // p_cnt - 1)
    def _():
        jn = lax.broadcasted_iota(jnp.int32, (1, LANE), 1)
        tok = jnp.where(first, row, row - dec)
        mask_n = (jn < dec) & (jn <= tok)
        reln = jn.astype(F32)
        pad = jnp.zeros((LANE - dec, d), F32)
        lam = _lambda_value(lp_ref, lam_init)
        gh = gh_ref[...]
        for h in range(A_KV):
            def new_rows(col0):
                return jnp.concatenate([qkv_ref[:, col0 + h * d:col0 + (h + 1) * d], pad], axis=0).astype(BF16)
            ka, kf = new_rows(1024), new_rows(3072)
            va, vf = _with_ones(new_rows(1536)), _with_ones(new_rows(3584))
            bias_a = slope_col(h) * reln
            bias_f = -ck_rows(ckn_ref, h)
            for kind in range(3):
                if kind < 2:
                    s = _dot_nt(qs_ref[h, kind], ka) + bias_a
                    v = va
                else:
                    s = _dot_nt(qs_ref[h, 2], kf) * scale_f + bias_f
                    v = vf
                s = jnp.where(mask_n, s, NEG)
                _online_update(s, v, m_ref, acc_ref, (h, kind))
            od = _dif_head_out(acc_ref[h, 0], acc_ref[h, 1], lam, gh, lam_init)
            of = _flash_out(acc_ref[h, 2])
            for g in range(2):
                o_ref[:, (4 * h + g) * d:(4 * h + g + 1) * d] = od[g * dec:(g + 1) * dec]
                o_ref[:, (4 * h + 2 + g) * d:(4 * h + 3 + g) * d] = of[g * dec:(g + 1) * dec]


def _page_spec(p_cnt, i):
    return pl.BlockSpec((None, LANE * KV, HEAD_DIM), lambda s, c, pt: (pt[s, c * p_cnt + i], 0, 0))


def _even_sample_attn(qkv32, caches, page_table, ck, lam_p, g_head, lam_init, dec):
    n_seq, n_pages = page_table.shape
    past_len = n_pages * LANE
    p_cnt = PAGES_PER_STEP
    d = HEAD_DIM
    n_chunks = n_pages // p_cnt
    in_specs = [
        pl.BlockSpec(memory_space=pltpu.SMEM),
        pl.BlockSpec((dec, qkv32.shape[1]), lambda s, c, pt: (s, 0)),
        pl.BlockSpec((None, 8, p_cnt * LANE), lambda s, c, pt: (s, 0, c)),
        pl.BlockSpec((None, 8, LANE), lambda s, c, pt: (s, 0, n_pages)),
        pl.BlockSpec((4, d // 2), lambda s, c, pt: (0, 0)),
        pl.BlockSpec((1, d), lambda s, c, pt: (0, 0)),
    ]
    ins = [_alibi_slopes(A_HEADS), qkv32, ck, ck, lam_p, g_head.reshape(1, d)]
    for arr in caches:
        for i in range(p_cnt):
            in_specs.append(_page_spec(p_cnt, i))
            ins.append(arr)
    rows = 2 * dec
    grid_spec = pltpu.PrefetchScalarGridSpec(
        num_scalar_prefetch=1,
        grid=(n_seq, n_chunks),
        in_specs=in_specs,
        out_specs=pl.BlockSpec((dec, 16 * d), lambda s, c, pt: (s, 0)),
        scratch_shapes=[
            pltpu.VMEM((A_KV, 3, rows, d), BF16),
            pltpu.VMEM((A_KV, 3, rows, LANE), F32),
            pltpu.VMEM((A_KV, 3, rows, 2 * d), F32),
        ],
    )
    return pl.pallas_call(
        functools.partial(_even_sample_body, n_pages=n_pages, past_len=past_len, dec=dec, lam_init=lam_init),
        grid_spec=grid_spec,
        out_shape=jax.ShapeDtypeStruct((n_seq * dec, 16 * d), F32),
        compiler_params=_cparams(("parallel", "arbitrary")),
    )(page_table, *ins)


def _even_out_perm():
    idx = []
    for h in range(A_KV):
        for base in (0, A_HEADS):
            for g in range(2):
                head = base + 2 * h + g
                idx.extend(range(head * HEAD_DIM, (head + 1) * HEAD_DIM))
    return np.asarray(idx, np.int32)


def _pos_softmax_body(pk_ref, pv_ref, wk_ref, wv_ref):
    for src, dst in ((pk_ref, wk_ref), (pv_ref, wv_ref)):
        x = src[...]
        e = jnp.exp(x - jnp.max(x, axis=1, keepdims=True))
        w = e / jnp.sum(e, axis=1, keepdims=True)
        dst[...] = jnp.concatenate([w, w], axis=1)


def _pos_softmax(pos_k, pos_v):
    expand = lambda p: jnp.broadcast_to(p.astype(F32).T[:, :, None], (C_KV, CMP_BLOCK, HEAD_DIM))
    shape = jax.ShapeDtypeStruct((C_KV, 2 * CMP_BLOCK, HEAD_DIM), F32)
    return pl.pallas_call(_pos_softmax_body, out_shape=(shape, shape))(expand(pos_k), expand(pos_v))


def _block_sums(x, w):
    r, width = x.shape
    xw = x.reshape(r // LANE, LANE, width) * w[None]
    return jnp.sum(xw.reshape(r // CMP_BLOCK, CMP_BLOCK, width), axis=1)


def _compress_rows_body(k_ref, v_ref, wk_ref, wv_ref, ok_ref, ov_ref):
    d = HEAD_DIM
    for h in range(C_KV):
        ok_ref[h] = _block_sums(k_ref[:, h * d:(h + 1) * d].astype(F32), wk_ref[h])
        ov_ref[h] = _block_sums(v_ref[:, h * d:(h + 1) * d].astype(F32), wv_ref[h])


def _compress_rows(qkv16, wk, wv, n_batch, seq, col_k, col_v):
    tr = 1024
    nc = seq // tr
    d = HEAD_DIM
    shape = jax.ShapeDtypeStruct((n_batch, C_KV, seq // CMP_BLOCK, d), F32)
    w_spec = pl.BlockSpec((C_KV, 2 * CMP_BLOCK, d), lambda b, i: (0, 0, 0))
    o_spec = pl.BlockSpec((None, C_KV, tr // CMP_BLOCK, d), lambda b, i: (b, 0, i, 0))
    return pl.pallas_call(
        _compress_rows_body,
        grid=(n_batch, nc),
        in_specs=[pl.BlockSpec((tr, KV_COLS), lambda b, i: (b * nc + i, col_k)),
                  pl.BlockSpec((tr, KV_COLS), lambda b, i: (b * nc + i, col_v)), w_spec, w_spec],
        out_specs=(o_spec, o_spec),
        out_shape=(shape, shape),
        compiler_params=_cparams(("parallel", "parallel")),
    )(qkv16, qkv16, wk, wv)


def _compress_paged_body(*refs):
    p_cnt = PAGES_PER_STEP
    kp = refs[1:1 + p_cnt]
    vp = refs[1 + p_cnt:1 + 2 * p_cnt]
    wk_ref, wv_ref, ok_ref, ov_ref = refs[1 + 2 * p_cnt:]
    for h in range(C_KV):
        wk, wv = wk_ref[h], wv_ref[h]
        ok_ref[h] = jnp.concatenate([_block_sums(_head_rows(r, h, LANE), wk) for r in kp], axis=0)
        ov_ref[h] = jnp.concatenate([_block_sums(_head_rows(r, h, LANE), wv) for r in vp], axis=0)


def _compress_paged(cache_k, cache_v, page_table, wk, wv):
    n_seq, n_pages = page_table.shape
    p_cnt = PAGES_PER_STEP
    d = HEAD_DIM
    per_step = p_cnt * LANE // CMP_BLOCK
    w_spec = pl.BlockSpec((C_KV, 2 * CMP_BLOCK, d), lambda s, c, pt: (0, 0, 0))
    o_spec = pl.BlockSpec((None, C_KV, per_step, d), lambda s, c, pt: (s, 0, c, 0))
    shape = jax.ShapeDtypeStruct((n_seq, C_KV, n_pages * LANE // CMP_BLOCK, d), F32)
    grid_spec = pltpu.PrefetchScalarGridSpec(
        num_scalar_prefetch=1,
        grid=(n_seq, n_pages // p_cnt),
        in_specs=[_page_spec(p_cnt, i) for i in range(p_cnt)] * 2 + [w_spec, w_spec],
        out_specs=(o_spec, o_spec),
    )
    return pl.pallas_call(
        _compress_paged_body,
        grid_spec=grid_spec,
        out_shape=(shape, shape),
        compiler_params=_cparams(("parallel", "arbitrary")),
    )(page_table, *([cache_k] * p_cnt), *([cache_v] * p_cnt), wk, wv)


def _top_n_mask(score, n_blocks, valid):
    lane = lax.broadcasted_iota(jnp.int32, score.shape, 1)
    rank = jnp.zeros(score.shape, F32)
    for i in range(n_blocks):
        col = score[:, i:i + 1]
        ahead = (col > score) | ((col == score) & (lane > i))
        rank = rank + jnp.where(ahead, 1.0, 0.0)
    keep = (rank < float(TOP_N)) & (lane < n_blocks) & valid
    return jnp.where(keep, 1.0, 0.0)


def _top_n_mask_cols(score, valid):
    blk = lax.broadcasted_iota(jnp.int32, score.shape, 0)
    rank = jnp.zeros(score.shape, F32)
    for i in range(score.shape[0]):
        row = score[i:i + 1, :]
        ahead = (row > score) | ((row == score) & (blk > i))
        rank = rank + jnp.where(ahead, 1.0, 0.0)
    return jnp.where((rank < float(TOP_N)) & valid, 1.0, 0.0)


def _eye(n, dtype):
    r = lax.broadcasted_iota(jnp.int32, (n, n), 0)
    c = lax.broadcasted_iota(jnp.int32, (n, n), 1)
    return jnp.where(r == c, 1.0, 0.0).astype(dtype)


def _stack_heads(ref, col0, n):
    return jnp.concatenate([ref[:, col0 + g * HEAD_DIM:col0 + (g + 1) * HEAD_DIM] for g in range(n)], axis=0)


OP_T = 256
OP_TK = 512


def _odd_prompt_body(sl_ref, q_ref, ks_ref, vs_ref, kw_ref, vw_ref, kc_ref, vc_ref, gt_ref,
                     o_ref, sx_ref, m_ref, acc_ref, *, seq):
    h = pl.program_id(1)
    qi = pl.program_id(2)
    t = OP_T
    tk = OP_TK
    d = HEAD_DIM
    gc = C_HEADS // C_KV
    rows = gc * t
    nb = seq // CMP_BLOCK
    scale = d ** -0.5
    q = _stack_heads(q_ref, 0, gc)
    row = lax.broadcasted_iota(jnp.int32, (rows, 1), 0)
    slopes = [sl_ref[gc * h + g] for g in range(gc)]
    slope = _row_select(row, t, slopes)
    tok = _row_select(row, t, [row - g * t for g in range(gc)])
    tok1 = lax.broadcasted_iota(jnp.int32, (t, 1), 0)
    q0 = qi * t

    blk = lax.broadcasted_iota(jnp.int32, (1, nb), 1)
    bend_rel = (blk + 1) * CMP_BLOCK - 1 - q0
    mask_c = bend_rel <= tok
    s_c = _dot_nt(q, kc_ref[...].astype(BF16)) * scale + slope * bend_rel.astype(F32)
    p_c = jnp.where(mask_c, _softmax_rows(jnp.where(mask_c, s_c, NEG)), 0.0)
    o_c = _dot(p_c.astype(BF16), vc_ref[...].astype(BF16))

    imp = p_c[0:t]
    for g in range(1, gc):
        imp = imp + p_c[g * t:(g + 1) * t]
    eye_nb = _eye(nb, F32)
    imp_t = lax.dot_general(eye_nb, imp, (((1,), (1,)), ((), ())), preferred_element_type=F32,
                            precision=lax.Precision.HIGHEST)
    blk_c = lax.broadcasted_iota(jnp.int32, (nb, 1), 0)
    pos_r = q0 + lax.broadcasted_iota(jnp.int32, (1, t), 1)
    cur = pos_r >> SEL_SHIFT
    forced = (blk_c == 0) | (blk_c == cur) | (blk_c == cur - 1)
    valid = blk_c * SEL_BLOCK <= pos_r
    score_t = jnp.where(valid, imp_t + jnp.where(forced, FORCE, 0.0), NEG)
    sel_t = _top_n_mask_cols(score_t, valid).astype(BF16)
    sel = _dot_nt(_eye(t, BF16), sel_t).astype(BF16)
    er = lax.broadcasted_iota(jnp.int32, (nb, seq), 0)
    ec = lax.broadcasted_iota(jnp.int32, (nb, seq), 1)
    expand = jnp.where(er == (ec >> SEL_SHIFT), 1.0, 0.0).astype(BF16)
    sel_keys = _dot(sel, expand)
    for kt in range(seq // tk):
        sx_ref[kt] = (sel_keys[:, kt * tk:(kt + 1) * tk] - 1.0) * (-NEG)

    m_ref[...] = jnp.full(m_ref.shape, NEG, F32)
    acc_ref[...] = jnp.zeros(acc_ref.shape, F32)

    def tile(ki, diag):
        k0 = pl.multiple_of(ki * tk, tk)
        k = ks_ref[pl.ds(k0, tk), :]
        v = _with_ones(vs_ref[pl.ds(k0, tk), :])
        rel = (k0 - q0) + lax.broadcasted_iota(jnp.int32, (1, tk), 1)
        relf = rel.astype(F32)
        madd = sx_ref[ki]
        if diag:
            madd = jnp.where(rel <= tok1, madd, NEG)
        s_all = _dot_nt(q, k) * scale
        for g in range(gc):
            s = s_all[g * t:(g + 1) * t] + (slopes[g] * relf + madd)
            _online_update(s, v, m_ref, acc_ref, g)

    def body(ki, carry):
        tile(ki, False)
        return carry

    n_full = (qi * t) // tk
    lax.fori_loop(0, n_full, body, 0)
    tile(n_full, True)

    n_slab = WINDOW + t
    start = pl.multiple_of(jnp.maximum(q0 - WINDOW, 0), t)
    kw = kw_ref[pl.ds(start, n_slab), :]
    vw = _with_ones(vw_ref[pl.ds(start, n_slab), :])
    rel_w = (start - q0) + lax.broadcasted_iota(jnp.int32, (1, n_slab), 1)
    delta = tok - rel_w
    mask_w = (delta >= 0) & (delta < WINDOW)
    s_w = _dot_nt(q, kw) * scale + slope * rel_w.astype(F32)
    o_w = _softmax_apply(jnp.where(mask_w, s_w, NEG), vw)

    gates = jax.nn.sigmoid(gt_ref[...])
    for g in range(gc):
        rs = slice(g * t, (g + 1) * t)
        o = (gates[:, 3 * g:3 * g + 1] * o_c[rs] + gates[:, 3 * g + 1:3 * g + 2] * _flash_out(acc_ref[g])
             + gates[:, 3 * g + 2:3 * g + 3] * o_w[rs])
        o_ref[:, g * d:(g + 1) * d] = o.astype(o_ref.dtype)


def _odd_prompt_attn(qkv16, k_cmp, v_cmp, gates, n_batch, seq):
    t = OP_T
    nq = seq // t
    d = HEAD_DIM
    gc = C_HEADS // C_KV
    assert seq % OP_TK == 0 and OP_TK % t == 0
    kv_spec = lambda col0: pl.BlockSpec((seq, d), lambda b, h, qi: (b, col0 + h))
    cmp_spec = pl.BlockSpec((None, None, seq // CMP_BLOCK, d), lambda b, h, qi: (b, h, 0, 0))
    return pl.pallas_call(
        functools.partial(_odd_prompt_body, seq=seq),
        grid=(n_batch, C_KV, nq),
        in_specs=[
            pl.BlockSpec(memory_space=pltpu.SMEM),
            pl.BlockSpec((t, gc * d), lambda b, h, qi: (b * nq + qi, h)),
            kv_spec(24), kv_spec(28), kv_spec(32), kv_spec(36),
            cmp_spec, cmp_spec,
            pl.BlockSpec((t, d), lambda b, h, qi: (b * nq + qi, h)),
        ],
        out_specs=pl.BlockSpec((t, gc * d), lambda b, h, qi: (b * nq + qi, h)),
        scratch_shapes=[
            pltpu.VMEM((seq // OP_TK, t, OP_TK), F32),
            pltpu.VMEM((gc, t, LANE), F32),
            pltpu.VMEM((gc, t, 2 * d), F32),
        ],
        out_shape=jax.ShapeDtypeStruct((n_batch * seq, C_HEADS * d), BF16),
        compiler_params=_cparams(("parallel", "parallel", "arbitrary")),
    )(_alibi_slopes(C_HEADS), qkv16, qkv16, qkv16, qkv16, qkv16, k_cmp, v_cmp, gates)


def _odd_sample_body(*refs, n_pages, past_len, dec):
    p_cnt = PAGES_PER_STEP
    pt_ref, sl_ref, qkv_ref, gt_ref, kc_ref, vc_ref, wk_ref, wv_ref = refs[:8]
    sk = refs[8:8 + p_cnt]
    sv = refs[8 + p_cnt:8 + 2 * p_cnt]
    o_ref, wko_ref, wvo_ref, qs_ref, sel_ref, oc_ref, m_ref, acc_ref = refs[8 + 2 * p_cnt:]
    del pt_ref
    c = pl.program_id(1)
    d = HEAD_DIM
    gc = C_HEADS // C_KV
    rows = gc * dec
    nb = past_len // CMP_BLOCK
    nbs = -(-(past_len + dec) // SEL_BLOCK)
    sel_w = 2 * LANE
    assert nb == LANE and nbs <= sel_w and (past_len + dec - 1) // SEL_BLOCK == past_len // SEL_BLOCK
    cur = past_len // SEL_BLOCK
    scale = d ** -0.5
    row = lax.broadcasted_iota(jnp.int32, (rows, 1), 0)
    tok = _row_select(row, dec, [row - g * dec for g in range(gc)])
    slopes = [_row_select(row, dec, [sl_ref[gc * h + g] for g in range(gc)]) for h in range(C_KV)]
    c_ks, c_vs, c_kw, c_vw = ((C_HEADS + i * C_KV) * d for i in (2, 3, 4, 5))

    @pl.when(c == 0)
    def _():
        blk = lax.broadcasted_iota(jnp.int32, (1, nb), 1)
        bend_rel = ((blk + 1) * CMP_BLOCK - 1 - past_len).astype(F32)
        lane = lax.broadcasted_iota(jnp.int32, (dec, sel_w), 1)
        forced = (lane == 0) | (lane == cur) | (lane == cur - 1)
        for h in range(C_KV):
            q = _stack_heads(qkv_ref, h * gc * d, gc).astype(BF16)
            qs_ref[h] = q
            p_c = _softmax_rows(_dot_nt(q, kc_ref[h].astype(BF16)) * scale + slopes[h] * bend_rel)
            oc_ref[h] = _dot(p_c.astype(BF16), vc_ref[h].astype(BF16))
            imp = p_c[0:dec]
            for g in range(1, gc):
                imp = imp + p_c[g * dec:(g + 1) * dec]
            imp = jnp.concatenate([imp, jnp.zeros((dec, sel_w - nb), F32)], axis=1)
            score = imp + jnp.where(forced, FORCE, 0.0)
            sel_ref[h] = _top_n_mask(score, nbs, lane < nbs)
        m_ref[...] = jnp.full(m_ref.shape, NEG, F32)
        acc_ref[...] = jnp.zeros(acc_ref.shape, F32)

    n_keys = p_cnt * LANE
    relf = (c * n_keys - past_len + lax.broadcasted_iota(jnp.int32, (1, n_keys), 1)).astype(F32)
    er = lax.broadcasted_iota(jnp.int32, (sel_w, n_keys), 0)
    ec = lax.broadcasted_iota(jnp.int32, (sel_w, n_keys), 1)
    expand = jnp.where(er - c * (n_keys // SEL_BLOCK) == (ec >> SEL_SHIFT), 1.0, 0.0).astype(BF16)
    for h in range(C_KV):
        k = jnp.concatenate([_head_rows(r, h, LANE) for r in sk], axis=0).astype(BF16)
        v = _with_ones(jnp.concatenate([_head_rows(r, h, LANE) for r in sv], axis=0).astype(BF16))
        keys = _dot(sel_ref[h].astype(BF16), expand)
        madd = jnp.concatenate([(keys - 1.0) * (-NEG)] * gc, axis=0)
        s = _dot_nt(qs_ref[h], k) * scale + (slopes[h] * relf + madd)
        _online_update(s, v, m_ref, acc_ref, h)

    @pl.when(c == n_pages // p_cnt - 1)
    def _():
        jn = lax.broadcasted_iota(jnp.int32, (1, LANE), 1)
        reln = jn.astype(F32)
        causal_n = (jn < dec) & (jn <= tok)
        pad = jnp.zeros((LANE - dec, d), F32)
        gates = jax.nn.sigmoid(gt_ref[...])
        n_win = WINDOW + LANE
        je = lax.broadcasted_iota(jnp.int32, (1, n_win), 1)
        rel_w = je - WINDOW
        delta = tok - rel_w
        mask_w = (delta >= 0) & (delta < WINDOW) & (je < WINDOW + dec)
        for h in range(C_KV):
            q = qs_ref[h]
            new = lambda col0: jnp.concatenate([qkv_ref[:, col0 + h * d:col0 + (h + 1) * d], pad], axis=0)
            picked = jnp.concatenate([sel_ref[h][:, cur:cur + 1]] * gc, axis=0) > 0.5
            s = _dot_nt(q, new(c_ks).astype(BF16)) * scale + slopes[h] * reln
            _online_update(jnp.where(causal_n & picked, s, NEG), _with_ones(new(c_vs).astype(BF16)), m_ref, acc_ref, h)
            o_s = _flash_out(acc_ref[h])
            kw = jnp.concatenate([_head_rows(wk_ref, h, WINDOW), new(c_kw)], axis=0).astype(BF16)
            vw = jnp.concatenate([_head_rows(wv_ref, h, WINDOW), new(c_vw)], axis=0).astype(BF16)
            s_w = _dot_nt(q, kw) * scale + slopes[h] * rel_w.astype(F32)
            o_w = _softmax_apply(jnp.where(mask_w, s_w, NEG), _with_ones(vw))
            o_c = oc_ref[h]
            for g in range(gc):
                rs = slice(g * dec, (g + 1) * dec)
                l0 = h * d + 3 * g
                o = (gates[:, l0:l0 + 1] * o_c[rs] + gates[:, l0 + 1:l0 + 2] * o_s[rs]
                     + gates[:, l0 + 2:l0 + 3] * o_w[rs])
                o_ref[:, (gc * h + g) * d:(gc * h + g + 1) * d] = o
            wko_ref[pl.ds((WINDOW - dec) * KV + h, dec, stride=KV), :] = qkv_ref[:, c_kw + h * d:c_kw + (h + 1) * d]
            wvo_ref[pl.ds((WINDOW - dec) * KV + h, dec, stride=KV), :] = qkv_ref[:, c_vw + h * d:c_vw + (h + 1) * d]
        wko_ref[0:(WINDOW - dec) * KV, :] = wk_ref[dec * KV:WINDOW * KV, :]
        wvo_ref[0:(WINDOW - dec) * KV, :] = wv_ref[dec * KV:WINDOW * KV, :]


def _odd_sample_attn(qkv32, gates, gate_block0, k_cmp, v_cmp, win_k, win_v, cache_k, cache_v, page_table, dec):
    n_seq, n_pages = page_table.shape
    past_len = n_pages * LANE
    p_cnt = PAGES_PER_STEP
    d = HEAD_DIM
    gc = C_HEADS // C_KV
    rows = gc * dec
    cmp_spec = pl.BlockSpec((None, C_KV, past_len // CMP_BLOCK, d), lambda s, c, pt: (s, 0, 0, 0))
    win_spec = pl.BlockSpec((None, WINDOW * KV, d), lambda s, c, pt: (s, 0, 0))
    in_specs = [
        pl.BlockSpec(memory_space=pltpu.SMEM),
        pl.BlockSpec((dec, qkv32.shape[1]), lambda s, c, pt: (s, 0)),
        pl.BlockSpec((dec, KV_COLS), lambda s, c, pt: (gate_block0 + s, 0)),
        cmp_spec, cmp_spec, win_spec, win_spec,
    ] + [_page_spec(p_cnt, i) for i in range(p_cnt)] * 2
    grid_spec = pltpu.PrefetchScalarGridSpec(
        num_scalar_prefetch=1,
        grid=(n_seq, n_pages // p_cnt),
        in_specs=in_specs,
        out_specs=(pl.BlockSpec((dec, C_HEADS * d), lambda s, c, pt: (s, 0)), win_spec, win_spec),
        scratch_shapes=[
            pltpu.VMEM((C_KV, rows, d), BF16),
            pltpu.VMEM((C_KV, dec, 2 * LANE), F32),
            pltpu.VMEM((C_KV, rows, d), F32),
            pltpu.VMEM((C_KV, rows, LANE), F32),
            pltpu.VMEM((C_KV, rows, 2 * d), F32),
        ],
    )
    win_shape = jax.ShapeDtypeStruct((n_seq, WINDOW * KV, d), F32)
    return pl.pallas_call(
        functools.partial(_odd_sample_body, n_pages=n_pages, past_len=past_len, dec=dec),
        grid_spec=grid_spec,
        out_shape=(jax.ShapeDtypeStruct((n_seq * dec, C_HEADS * d), F32), win_shape, win_shape),
        compiler_params=_cparams(("parallel", "arbitrary")),
    )(page_table, _alibi_slopes(C_HEADS), qkv32, gates, k_cmp, v_cmp, win_k, win_v,
      *([cache_k] * p_cnt), *([cache_v] * p_cnt))


def _xattn_body(q_ref, k_ref, v_ref, o_ref):
    d = HEAD_DIM
    scale = d ** -0.5
    tq = q_ref.shape[0]
    n_mem = k_ref.shape[0] // KV
    pad_rows = (-tq) % 16
    for h in range(X_HEADS):
        cols = slice(h * d, (h + 1) * d)
        q = q_ref[:, cols]
        if pad_rows:
            q = jnp.concatenate([q, jnp.zeros((pad_rows, d), F32)], axis=0)
        s = _dot_nt(q.astype(BF16), _head_rows(k_ref, h, n_mem).astype(BF16)) * scale
        o = _softmax_apply(s, _with_ones(_head_rows(v_ref, h, n_mem).astype(BF16)))
        o_ref[:, cols] = o[0:tq].astype(o_ref.dtype)


def _xattn(q, mem_k, mem_v, layer, row_block0, tq, n_b, nq):
    width = X_HEADS * HEAD_DIM
    mem_spec = pl.BlockSpec((None, None, mem_k.shape[2], HEAD_DIM), lambda b, i: (layer, b, 0, 0))
    return pl.pallas_call(
        _xattn_body,
        grid=(n_b, nq),
        in_specs=[pl.BlockSpec((tq, width), lambda b, i: (row_block0 + b * nq + i, 0)), mem_spec, mem_spec],
        out_specs=pl.BlockSpec((tq, width), lambda b, i: (b * nq + i, 0)),
        out_shape=jax.ShapeDtypeStruct((n_b * nq * tq, width), F32),
        compiler_params=_cparams(("parallel", "arbitrary")),
    )(q, mem_k, mem_v)


def _router_body(x_ref, g_ref, w_ref, b_ref, o_ref):
    x = x_ref[...]
    hn = x * lax.rsqrt(jnp.mean(x * x, axis=-1, keepdims=True) + EPS) * g_ref[...]
    logits = jnp.dot(hn, w_ref[...], preferred_element_type=F32, precision=lax.Precision.HIGHEST) + b_ref[...]
    lane = lax.broadcasted_iota(jnp.int32, logits.shape, 1).astype(F32)
    big = float(2 * LANE)
    ninf = -jnp.inf

    def first_max(v):
        m = jnp.max(v, axis=-1, keepdims=True)
        return m, jnp.min(jnp.where(v == m, lane, big), axis=-1, keepdims=True)

    lgm = jnp.where(lane < N_GROUPS, logits, ninf)
    mg, gi = first_max(lgm)
    pg_top = 1.0 / jnp.sum(jnp.exp(lgm - mg), axis=-1, keepdims=True)
    lo = N_GROUPS + gi * EXP_PER_GROUP
    lem = jnp.where((lane >= lo) & (lane < lo + EXP_PER_GROUP), logits, ninf)
    m1, i1 = first_max(lem)
    lem2 = jnp.where(lane == i1, ninf, lem)
    m2, i2 = first_max(lem2)
    z = jnp.sum(jnp.exp(lem - m1), axis=-1, keepdims=True)
    pe1 = 1.0 / z
    pe2 = jnp.exp(m2 - m1) / z
    w1 = pg_top * pe1 / (pe1 + pe2)
    w2 = pg_top * pe2 / (pe1 + pe2)
    e1 = i1 - N_GROUPS
    e2 = i2 - N_GROUPS
    o_ref[...] = jnp.where(lane == 0, e1, jnp.where(lane == 1, e2, jnp.where(lane == 2, w1, jnp.where(lane == 3, w2, 0.0))))


def _router(x, g, w_grp, b_grp, w_er, b_er, tm):
    n, d = x.shape
    pad = LANE - N_GROUPS - N_EXPERTS
    w = jnp.pad(jnp.concatenate([w_grp, w_er], axis=1).astype(F32), ((0, 0), (0, pad)))
    b = jnp.pad(jnp.concatenate([b_grp, b_er]).astype(F32), (0, pad)).reshape(1, LANE)
    return pl.pallas_call(
        _router_body,
        grid=(n // tm,),
        in_specs=[pl.BlockSpec((tm, d), lambda i: (i, 0)), pl.BlockSpec((1, d), lambda i: (0, 0)),
                  pl.BlockSpec((d, LANE), lambda i: (0, 0)), pl.BlockSpec((1, LANE), lambda i: (0, 0))],
        out_specs=pl.BlockSpec((tm, LANE), lambda i: (i, 0)),
        out_shape=jax.ShapeDtypeStruct((n, LANE), F32),
        compiler_params=_cparams(("parallel",)),
    )(x, g.reshape(1, d).astype(F32), w, b)


def _moe_max_tiles(n_rows):
    return n_rows // MOE_TM + N_EXPERTS


ROW_ALIGN = 8


def _moe_rows(n):
    return 2 * n + ROW_ALIGN * N_EXPERTS


def _dispatch_tables(e1, e2):
    n = e1.shape[0]
    tm = MOE_TM
    max_tiles = _moe_max_tiles(2 * n)
    e = jnp.concatenate([e1, e2])
    tok = jnp.concatenate([jnp.arange(n, dtype=jnp.int32)] * 2)
    onehot = (e[:, None] == jnp.arange(N_EXPERTS, dtype=jnp.int32)[None, :]).astype(jnp.int32)
    counts = jnp.sum(onehot, axis=0)
    rank = jnp.sum((jnp.cumsum(onehot, axis=0) - onehot) * onehot, axis=1)
    aligned = ((counts + ROW_ALIGN - 1) // ROW_ALIGN) * ROW_ALIGN
    starts = jnp.cumsum(aligned) - aligned
    pos = jnp.sum(starts[None, :] * onehot, axis=1) + rank
    src = jnp.zeros((_moe_rows(n) + tm,), jnp.int32).at[pos].set(tok)
    tiles_per = (counts + tm - 1) // tm
    tile_ends = jnp.cumsum(tiles_per)
    n_used = tile_ends[-1].astype(jnp.int32)
    t = jnp.arange(max_tiles, dtype=jnp.int32)
    tile_e = jnp.minimum(jnp.sum((tile_ends[None, :] <= t[:, None]).astype(jnp.int32), axis=1), N_EXPERTS - 1)
    eh = (tile_e[:, None] == jnp.arange(N_EXPERTS, dtype=jnp.int32)[None, :]).astype(jnp.int32)
    pick = lambda v: jnp.sum(eh * v[None, :], axis=1)
    in_e = t - pick(tile_ends - tiles_per)
    used = t < n_used
    tile_start = jnp.where(used, pick(starts) + in_e * tm, 0).astype(jnp.int32)
    tile_rows = jnp.where(used, jnp.clip(pick(counts) - in_e * tm, 0, tm), 0).astype(jnp.int32)
    used_rows = jnp.stack([n_used, jnp.sum(aligned).astype(jnp.int32)])
    return tile_e.astype(jnp.int32), tile_start, tile_rows, used_rows, src, pos[:n], pos[n:]


ROW_GROUP = 8


def _row_copy(idx_ref, i, r, src_hbm, dst, sem):
    return pltpu.make_async_copy(src_hbm.at[pl.ds(idx_ref[i], 1)], dst.at[pl.ds(r, 1)], sem)


def _start_rows(idx_ref, base, n, src_hbm, dst, sem):
    def start(j, carry):
        for u in range(ROW_GROUP):
            r = j * ROW_GROUP + u
            _row_copy(idx_ref, base + r, r, src_hbm, dst, sem).start(priority=u % 2)
        return carry
    lax.fori_loop(0, (n + ROW_GROUP - 1) // ROW_GROUP, start, 0)


def _wait_rows(idx_ref, base, n, src_hbm, dst, sem):
    def wait(j, carry):
        for u in range(ROW_GROUP):
            r = j * ROW_GROUP + u
            _row_copy(idx_ref, base + r, r, src_hbm, dst, sem).wait()
        return carry
    lax.fori_loop(0, (n + ROW_GROUP - 1) // ROW_GROUP, wait, 0)


def _experts_body(te_ref, ts_ref, tr_ref, nu_ref, src_ref, x_hbm, g_ref, wg_hbm, wu_hbm, wd_hbm, y_hbm,
                  xbuf, ybuf, hn_ref, wgb, wub, wdb, gsem, osem, wsem, *, layer, n_rows):
    t = pl.program_id(0)
    n_used = nu_ref[0]
    slot = t % 2
    tm = MOE_TM
    n_chunks = wgb.shape[0]
    fc = wgb.shape[2]

    def gather(fn, tile, sl):
        fn(src_ref, ts_ref[tile], tr_ref[tile], x_hbm, xbuf.at[sl], gsem.at[sl])

    def w_copies(tile, c):
        e = te_ref[tile]
        cols = pl.ds(c * fc, fc)
        return (pltpu.make_async_copy(wg_hbm.at[layer, e, :, cols], wgb.at[c], wsem.at[c]),
                pltpu.make_async_copy(wu_hbm.at[layer, e, :, cols], wub.at[c], wsem.at[c]),
                pltpu.make_async_copy(wd_hbm.at[layer, e, cols, :], wdb.at[c], wsem.at[c]))

    def out_rows(row0, sl):
        if not isinstance(row0, int):
            row0 = pl.multiple_of(row0, ROW_ALIGN)
        return pltpu.make_async_copy(ybuf.at[sl], y_hbm.at[pl.ds(row0, tm)], osem.at[sl])

    def out_copy(tile, sl):
        return out_rows(ts_ref[tile], sl)

    @pl.when(t == 0)
    def _():
        xbuf[...] = jnp.zeros(xbuf.shape, F32)
        gather(_start_rows, 0, 0)
        for cp in w_copies(0, 0):
            cp.start()

    @pl.when(t < n_used)
    def _():
        @pl.when(t + 1 < n_used)
        def _():
            gather(_start_rows, t + 1, 1 - slot)

        gather(_wait_rows, t, slot)
        x = xbuf[slot]
        hn_ref[...] = (x * lax.rsqrt(jnp.mean(x * x, axis=-1, keepdims=True) + EPS) * g_ref[...]).astype(BF16)
        for c in range(n_chunks):
            if c + 1 < n_chunks:
                for cp in w_copies(t, c + 1):
                    cp.start()
            else:
                @pl.when(t + 1 < n_used)
                def _():
                    for cp in w_copies(t + 1, 0):
                        cp.start()
            for cp in w_copies(t, c):
                cp.wait()
            hn = hn_ref[...]
            gate = _dot(hn, wgb[c].astype(BF16))
            up = _dot(hn, wub[c].astype(BF16))
            act = (gate * jax.nn.sigmoid(gate) * up).astype(BF16)
            part = _dot(act, wdb[c].astype(BF16))
            if c == 0:
                ybuf[slot] = part
            else:
                ybuf[slot] += part

        @pl.when(t >= 1)
        def _():
            out_copy(t - 1, 1 - slot).wait()

        out_copy(t, slot).start()

        @pl.when(t == n_used - 1)
        def _():
            out_copy(t, slot).wait()
            ybuf[1 - slot] = jnp.zeros(ybuf.shape[1:], F32)
            for row0 in (nu_ref[1], n_rows):
                tail = out_rows(row0, 1 - slot)
                tail.start()
                tail.wait()


def _experts(tile_e, tile_start, tile_rows, n_used, src, x, g, w_gate, w_up, w_down, layer):
    n_rows = src.shape[0] - MOE_TM
    assert MOE_TM >= ROW_ALIGN * N_EXPERTS
    d = x.shape[1]
    f = w_gate.shape[3]
    tm = MOE_TM
    fc = f // 2
    assert f % (2 * LANE) == 0
    grid_spec = pltpu.PrefetchScalarGridSpec(
        num_scalar_prefetch=5,
        grid=(tile_e.shape[0],),
        in_specs=[
            pl.BlockSpec(memory_space=pl.ANY),
            pl.BlockSpec((1, d), lambda t, *_: (0, 0)),
            pl.BlockSpec(memory_space=pl.ANY), pl.BlockSpec(memory_space=pl.ANY), pl.BlockSpec(memory_space=pl.ANY),
        ],
        out_specs=pl.BlockSpec(memory_space=pl.ANY),
        scratch_shapes=[pltpu.VMEM((2, tm, d), F32), pltpu.VMEM((2, tm, d), F32), pltpu.VMEM((tm, d), BF16),
                        pltpu.VMEM((2, d, fc), F32), pltpu.VMEM((2, d, fc), F32), pltpu.VMEM((2, fc, d), F32),
                        pltpu.SemaphoreType.DMA((2,)), pltpu.SemaphoreType.DMA((2,)), pltpu.SemaphoreType.DMA((2,))],
    )
    return pl.pallas_call(
        functools.partial(_experts_body, layer=layer, n_rows=n_rows),
        grid_spec=grid_spec,
        out_shape=jax.ShapeDtypeStruct((n_rows + tm, d), F32),
        compiler_params=_cparams(("arbitrary",)),
    )(tile_e, tile_start, tile_rows, n_used, src, x, g.reshape(1, d).astype(F32), w_gate, w_up, w_down)


def _combine_body(p1_ref, p2_ref, x_ref, r_ref, y_hbm, o_ref, ybuf, sem):
    i = pl.program_id(0)
    n = pl.num_programs(0)
    tm = x_ref.shape[0]
    slot = i % 2

    def start(step, sl):
        _start_rows(p1_ref, step * tm, tm, y_hbm, ybuf.at[sl, 0], sem.at[sl])
        _start_rows(p2_ref, step * tm, tm, y_hbm, ybuf.at[sl, 1], sem.at[sl])

    @pl.when(i == 0)
    def _():
        start(0, 0)

    @pl.when(i + 1 < n)
    def _():
        start(i + 1, 1 - slot)

    _wait_rows(p1_ref, i * tm, tm, y_hbm, ybuf.at[slot, 0], sem.at[slot])
    _wait_rows(p2_ref, i * tm, tm, y_hbm, ybuf.at[slot, 1], sem.at[slot])
    r = r_ref[...]
    o_ref[...] = x_ref[...] + r[:, 2:3] * ybuf[slot, 0] + r[:, 3:4] * ybuf[slot, 1]


def _combine(p1, p2, x, r, y, tm):
    n, d = x.shape
    grid_spec = pltpu.PrefetchScalarGridSpec(
        num_scalar_prefetch=2,
        grid=(n // tm,),
        in_specs=[
            pl.BlockSpec((tm, d), lambda i, a, b: (i, 0)),
            pl.BlockSpec((tm, LANE), lambda i, a, b: (i, 0)),
            pl.BlockSpec(memory_space=pl.ANY),
        ],
        out_specs=pl.BlockSpec((tm, d), lambda i, a, b: (i, 0)),
        scratch_shapes=[pltpu.VMEM((2, 2, tm, d), F32), pltpu.SemaphoreType.DMA((2,))],
    )
    return pl.pallas_call(
        _combine_body,
        grid_spec=grid_spec,
        out_shape=jax.ShapeDtypeStruct((n, d), F32),
        compiler_params=_cparams(("arbitrary",)),
    )(p1, p2, x, r, y)


def _moe(x, g, w_grp, b_grp, w_er, b_er, w_gate, w_up, w_down, layer, tm):
    n = x.shape[0]
    r = _router(x, g, w_grp, b_grp, w_er, b_er, tm)
    e1 = r[:, 0].astype(jnp.int32)
    e2 = r[:, 1].astype(jnp.int32)
    tile_e, tile_start, tile_rows, n_used, src, p1, p2 = _dispatch_tables(e1, e2)
    y = _experts(tile_e, tile_start, tile_rows, n_used, src, x, g, w_gate, w_up, w_down, layer)
    return _combine(p1, p2, x, r, y, COMBINE_TM)


def _rmsnorm_body(x_ref, g_ref, o_ref):
    x = x_ref[...]
    o_ref[...] = x * lax.rsqrt(jnp.mean(x * x, axis=-1, keepdims=True) + EPS) * g_ref[...]


def _rmsnorm(x, g, m, row_off, tm):
    d = x.shape[1]
    return pl.pallas_call(
        _rmsnorm_body,
        grid=(m // tm,),
        in_specs=[pl.BlockSpec((tm, d), lambda i: (i + row_off, 0)), pl.BlockSpec((1, d), lambda i: (0, 0))],
        out_specs=pl.BlockSpec((tm, d), lambda i: (i, 0)),
        out_shape=jax.ShapeDtypeStruct((m, d), F32),
        compiler_params=_cparams(("parallel",)),
    )(x, g.reshape(1, d).astype(F32))


def _row_tile(n, cands=(768, 512, 256, 128)):
    for tm in cands:
        if n % tm == 0:
            return tm
    raise ValueError(f"row count {n} has no supported row tile")


def kernel(x_prompt, x_sample, mem_prompt, cache_dif_k, cache_dif_v, cache_fox_k, cache_fox_v, cache_fox_logf, cache_cmp_k, cache_cmp_v, cache_slc_k, cache_slc_v, state_win_k, state_win_v, cache_mem_k, cache_mem_v, page_table, g_mix, w_in_e, b_forget, lam_q1, lam_k1, lam_q2, lam_k2, g_dif_head, w_out_e, w_in_o, cmp_pos_k, cmp_pos_v, w_out_o, g_xattn, g_mem, w_xq, w_xk, w_xv, w_xo, g_ffn, w_grp, b_grp, w_er, b_er, w_gate, w_up, w_down, g_final):
    n_b, seq, d = x_prompt.shape
    n_s, dec, _ = x_sample.shape
    n_mem = mem_prompt.shape[1]
    depth = g_mix.shape[0]
    n_pages = page_table.shape[1]
    past_len = n_pages * LANE
    n_p = n_b * seq
    n_smp = n_s * dec
    n_tok = n_p + n_smp
    hd = HEAD_DIM
    tmj = _row_tile(n_tok)
    tmp = _row_tile(n_p, (1024, 512, 256, 128))
    tms = _row_tile(n_smp, (256, 128))
    assert seq % EP_T == 0 and seq % 1024 == 0 and n_p % tms == 0 and n_pages % PAGES_PER_STEP == 0
    assert cache_dif_k.shape[2] == LANE and state_win_k.shape[2] == WINDOW and n_p % dec == 0

    x = jnp.concatenate([x_prompt.reshape(n_p, d), x_sample.reshape(n_smp, d)], axis=0)
    head_rows = lambda a: a.reshape(a.shape[0], a.shape[1] * KV, hd)
    kv5 = lambda a, lead: a.reshape(*lead, KV, hd)

    mem_k_s = cache_mem_k.reshape(depth, n_s, n_mem * KV, hd)
    mem_v_s = cache_mem_v.reshape(depth, n_s, n_mem * KV, hd)

    mem = mem_prompt.reshape(n_b * n_mem, d)
    mem_k_p, mem_v_p = [], []
    for l in range(depth):
        w_kv = jnp.concatenate([w_xk[l], w_xv[l]], axis=1).astype(BF16)
        mk, mv = _nmm(mem, w_kv, g=g_mem[l], out_dtypes=(), kv_cols=(0, 1),
                      tm=_row_tile(n_b * n_mem, (512, 256, 128)), tn=KV_COLS)
        mem_k_p.append(mk)
        mem_v_p.append(mv)

    even_p, even_s, odd_p, odd_s = [], [], [], []
    for l in range(depth):
        j = l // 2
        if l % 2 == 0:
            w_main = w_in_e[j][:, :E_MAIN].astype(BF16)
            w_fg = jnp.pad(w_in_e[j][:, E_MAIN:], ((0, 0), (0, LANE - F_HEADS))).astype(BF16)
            kvc = (2, 3, 6, 7)
            qkv16, *kv_p = _nmm(x, w_main, g=g_mix[l], out_dtypes=(BF16,), kv_cols=kvc, m=n_p, tm=tmp, tn=KV_COLS)
            qkv32, *kv_s = _nmm(x, w_main, g=g_mix[l], out_dtypes=(F32,), kv_cols=kvc, m=n_smp,
                                row_off=n_p // tms, tm=tms, tn=KV_COLS)
            logf = _nmm(x, w_fg, g=g_mix[l], bias=jnp.pad(b_forget[j], (0, LANE - F_HEADS)), tm=tmj, tn=LANE)
            logf = logf[:, :F_HEADS]
            lf_p = logf[:n_p].reshape(n_b, seq, F_HEADS)
            lf_s = logf[n_p:].reshape(n_s, dec, F_HEADS)
            ck_p = _cumsum(lf_p.transpose(0, 2, 1)).reshape(n_b, F_KV, F_HEADS // F_KV, seq)
            lf_past = cache_fox_logf[j].transpose(0, 2, 1)[page_table]
            lf_past = lf_past.transpose(0, 2, 1, 3).reshape(n_s, F_HEADS, past_len)
            lf_new = jnp.pad(lf_s.transpose(0, 2, 1), ((0, 0), (0, 0), (0, LANE - dec)))
            ck_s = _cumsum(jnp.concatenate([lf_past, lf_new], axis=2))
            lam_p = jnp.stack([lam_q1[j], lam_k1[j], lam_q2[j], lam_k2[j]]).astype(F32)
            lam_init = _lambda_init(l)
            o_p = _even_prompt_attn(qkv16, ck_p, lam_p, g_dif_head[j], n_b, seq, lam_init)
            caches = [head_rows(c[j]) for c in (cache_dif_k, cache_dif_v, cache_fox_k, cache_fox_v)]
            o_s = _even_sample_attn(qkv32, caches, page_table, ck_s, lam_p, g_dif_head[j], lam_init, dec)
            w_out = w_out_e[j][_even_out_perm()].astype(BF16)
            even_p.append([kv5(a, (n_b, seq)) for a in kv_p] + [lf_p])
            even_s.append([kv5(a, (n_s, dec)) for a in kv_s] + [lf_s])
        else:
            w_main = w_in_o[j][:, :O_MAIN].astype(BF16)
            w_g = w_in_o[j][:, O_MAIN:].reshape(d, C_KV, 3 * C_HEADS // C_KV)
            w_g = jnp.pad(w_g, ((0, 0), (0, 0), (0, LANE - w_g.shape[2]))).reshape(d, C_KV * LANE).astype(BF16)
            qkv16, *kv_p = _nmm(x, w_main, g=g_mix[l], out_dtypes=(BF16,), kv_cols=(4, 5, 6, 7, 8, 9),
                                m=n_p, tm=tmp, tn=KV_COLS)
            qkv32, *kv_s = _nmm(x, w_main, g=g_mix[l], out_dtypes=(F32,), kv_cols=(4, 5, 6, 7), m=n_smp,
                                row_off=n_p // tms, tm=tms, tn=KV_COLS)
            gates = _nmm(x, w_g, g=g_mix[l], tm=tmj, tn=KV_COLS)
            wk, wv = _pos_softmax(cmp_pos_k[j], cmp_pos_v[j])
            kc_p, vc_p = _compress_rows(qkv16, wk, wv, n_b, seq, 4, 5)
            kc_s, vc_s = _compress_paged(head_rows(cache_cmp_k[j]), head_rows(cache_cmp_v[j]), page_table, wk, wv)
            o_p = _odd_prompt_attn(qkv16, kc_p, vc_p, gates, n_b, seq)
            o_s, win_k, win_v = _odd_sample_attn(
                qkv32, gates, n_p // dec, kc_s, vc_s, head_rows(state_win_k[j]), head_rows(state_win_v[j]),
                head_rows(cache_slc_k[j]), head_rows(cache_slc_v[j]), page_table, dec)
            w_out = w_out_o[j].astype(BF16)
            n_keep = min(WINDOW, seq)
            rows_p = [kv5(a, (n_b, seq)) for a in kv_p]
            odd_p.append(rows_p[:4] + [a[:, seq - n_keep:] for a in rows_p[4:]])
            odd_s.append([kv5(a, (n_s, dec)) for a in kv_s] + [kv5(win_k, (n_s, WINDOW)), kv5(win_v, (n_s, WINDOW))])
        o = jnp.concatenate([o_p, o_s.astype(BF16)], axis=0)
        x = _nmm(o, w_out, res=x, tm=tmj, tn=d)

        q = _nmm(x, w_xq[l].astype(BF16), g=g_xattn[l], tm=tmj, tn=512)
        ox_p = _xattn(q, mem_k_p[l].reshape(1, n_b, n_mem * KV, hd), mem_v_p[l].reshape(1, n_b, n_mem * KV, hd),
                      0, 0, 512, n_b, seq // 512)
        ox_s = _xattn(q, mem_k_s, mem_v_s, l, n_p // dec, dec, n_s, 1)
        x = _nmm(jnp.concatenate([ox_p, ox_s], axis=0), w_xo[l].astype(BF16), res=x, tm=tmj, tn=d)

        x = _moe(x, g_ffn[l], w_grp[l], b_grp[l], w_er[l], b_er[l], w_gate, w_up, w_down, l, tmj)

    y_prompt = _rmsnorm(x, g_final, n_p, 0, tmp).reshape(n_b, seq, d)
    y_sample = _rmsnorm(x, g_final, n_smp, n_p // tms, tms).reshape(n_s, dec, d)
    stack = lambda rows: [jnp.stack(a) for a in zip(*rows)]
    mem_k = jnp.stack([kv5(a, (n_b, n_mem)) for a in mem_k_p])
    mem_v = jnp.stack([kv5(a, (n_b, n_mem)) for a in mem_v_p])
    return (y_prompt, y_sample, *stack(even_p), *stack(odd_p), mem_k, mem_v, *stack(even_s), *stack(odd_s))
```

```python
import functools
import math

import numpy as np
import jax
import jax.numpy as jnp
from jax import lax
from jax.experimental import pallas as pl
from jax.experimental.pallas import tpu as pltpu

F32 = jnp.float32
BF16 = jnp.bfloat16

HEAD_DIM = 128
A_HEADS, A_KV = 8, 4
F_HEADS, F_KV = 8, 4
C_HEADS, C_KV = 16, 4
KV = 4
CMP_BLOCK = 64
SEL_BLOCK = 64
SEL_SHIFT = 6
TOP_N = 16
WINDOW = 512
X_HEADS = 4
N_GROUPS = 4
EXP_PER_GROUP = 8
N_EXPERTS = N_GROUPS * EXP_PER_GROUP
EPS = 1e-6
NEG = -1e30
FORCE = 1e4
E_MAIN = (A_HEADS + 2 * A_KV + F_HEADS + 2 * F_KV) * HEAD_DIM
O_MAIN = (C_HEADS + 6 * C_KV) * HEAD_DIM

LANE = 128
VMEM_LIMIT = 56 * 1024 * 1024
PAGES_PER_STEP = 16
PAGES_PER_STEP_2 = 32
MOE_TM = 768
COMBINE_TM = 256
KV_COLS = KV * HEAD_DIM


def _cparams(sem):
    return pltpu.CompilerParams(dimension_semantics=sem, vmem_limit_bytes=VMEM_LIMIT)


def _dot_nt(a, b):
    return lax.dot_general(a, b, (((1,), (1,)), ((), ())), preferred_element_type=F32)


def _dot(a, b):
    return jnp.dot(a, b, preferred_element_type=F32)


def _alibi_slopes(n_heads):
    return jnp.asarray(np.exp2(-8.0 * np.arange(1, n_heads + 1) / n_heads).astype(np.float32))


def _lambda_init(layer):
    return 0.8 - 0.6 * math.exp(-0.3 * layer)


def _head_rows(ref, h, n, row0=0):
    return ref[pl.ds(row0 * KV + h, n, stride=KV), :]


def _nmm_body(*refs, norm, epilogue, n_out, kv_cols):
    it = iter(refs)
    a_ref = next(it)
    g_ref = next(it) if norm else None
    b_ref = next(it)
    e_ref = next(it) if epilogue is not None else None
    outs = [next(it) for _ in range(n_out)]
    kv_refs = [next(it) for _ in kv_cols]
    as_ref = next(it)
    tm = a_ref.shape[0]

    @pl.when(pl.program_id(1) == 0)
    def _():
        a = a_ref[...].astype(F32)
        if norm:
            a = a * lax.rsqrt(jnp.mean(a * a, axis=-1, keepdims=True) + EPS) * g_ref[...]
        as_ref[...] = a.astype(BF16)

    acc = _dot(as_ref[...], b_ref[...])
    if epilogue == "res":
        acc = acc + e_ref[...]
    elif epilogue == "logsig":
        z = acc + e_ref[...]
        acc = jnp.minimum(z, 0.0) - jnp.log1p(jnp.exp(-jnp.abs(z)))
    for o in outs:
        o[...] = acc.astype(o.dtype)
    for ref, jcol in zip(kv_refs, kv_cols):
        @pl.when(pl.program_id(1) == jcol)
        def _(ref=ref):
            for h in range(KV):
                ref[pl.ds(h, tm, stride=KV), :] = acc[:, h * HEAD_DIM:(h + 1) * HEAD_DIM]


def _nmm(a, b, *, g=None, res=None, bias=None, out_dtypes=(F32,), kv_cols=(), m=None, row_off=0, tm, tn):
    k = a.shape[1]
    m = a.shape[0] if m is None else m
    n = b.shape[1]
    assert m % tm == 0 and n % tn == 0 and (not kv_cols or tn == KV_COLS)
    norm = g is not None
    epilogue = "res" if res is not None else ("logsig" if bias is not None else None)
    ins = [a]
    specs = [pl.BlockSpec((tm, k), lambda i, j: (i + row_off, 0))]
    if norm:
        ins.append(g.reshape(1, k).astype(F32))
        specs.append(pl.BlockSpec((1, k), lambda i, j: (0, 0)))
    ins.append(b)
    specs.append(pl.BlockSpec((k, tn), lambda i, j: (0, j)))
    if epilogue == "res":
        ins.append(res)
        specs.append(pl.BlockSpec((tm, tn), lambda i, j: (i + row_off, j)))
    elif epilogue == "logsig":
        ins.append(bias.reshape(1, n).astype(F32))
        specs.append(pl.BlockSpec((1, tn), lambda i, j: (0, j)))
    out_shape = [jax.ShapeDtypeStruct((m, n), dt) for dt in out_dtypes]
    out_specs = [pl.BlockSpec((tm, tn), lambda i, j: (i, j)) for _ in out_dtypes]
    out_shape += [jax.ShapeDtypeStruct((m * KV, HEAD_DIM), F32) for _ in kv_cols]
    out_specs += [pl.BlockSpec((tm * KV, HEAD_DIM), lambda i, j: (i, 0)) for _ in kv_cols]
    outs = pl.pallas_call(
        functools.partial(_nmm_body, norm=norm, epilogue=epilogue, n_out=len(out_dtypes), kv_cols=tuple(kv_cols)),
        grid=(m // tm, n // tn),
        in_specs=specs,
        out_specs=out_specs,
        out_shape=out_shape,
        scratch_shapes=[pltpu.VMEM((tm, k), BF16)],
        compiler_params=_cparams(("parallel", "arbitrary")),
    )(*ins)
    return outs[0] if len(outs) == 1 else outs


def _cumsum_body(x_ref, o_ref):
    nbk, h, l = x_ref.shape
    r = lax.broadcasted_iota(jnp.int32, (LANE, LANE), 0)
    c = lax.broadcasted_iota(jnp.int32, (LANE, LANE), 1)
    upper = (r <= c).astype(F32)
    carry = jnp.zeros((nbk * h, 1), F32)
    for j in range(l // LANE):
        blk = x_ref[:, :, j * LANE:(j + 1) * LANE].reshape(nbk * h, LANE)
        cs = jnp.dot(blk, upper, preferred_element_type=F32, precision=lax.Precision.HIGHEST)
        o_ref[:, :, j * LANE:(j + 1) * LANE] = (cs + carry).reshape(nbk, h, LANE)
        carry = carry + cs[:, LANE - 1:LANE]


def _cumsum(x):
    nb, h, l = x.shape
    nbk = math.gcd(nb, 8)
    return pl.pallas_call(
        _cumsum_body,
        grid=(nb // nbk,),
        in_specs=[pl.BlockSpec((nbk, h, l), lambda i: (i, 0, 0))],
        out_specs=pl.BlockSpec((nbk, h, l), lambda i: (i, 0, 0)),
        out_shape=jax.ShapeDtypeStruct((nb, h, l), F32),
        compiler_params=_cparams(("parallel",)),
    )(x)


def _with_ones(v):
    return jnp.concatenate([v, jnp.ones_like(v)], axis=1)


def _online_update(s, v_ext, m_ref, acc_ref, idx):
    m_prev = m_ref[idx]
    m_new = jnp.maximum(m_prev, jnp.max(s, axis=-1, keepdims=True))
    alpha = jnp.exp(m_prev - m_new)
    p = jnp.exp(s - jnp.tile(m_new, (1, s.shape[1] // LANE)))
    acc_ref[idx] = jnp.tile(alpha, (1, 2)) * acc_ref[idx] + _dot(p.astype(BF16), v_ext)
    m_ref[idx] = m_new


def _flash_out(acc):
    d = acc.shape[1] // 2
    return acc[:, :d] / acc[:, d:]


def _softmax_rows(s):
    e = jnp.exp(s - jnp.max(s, axis=-1, keepdims=True))
    return e / jnp.sum(e, axis=-1, keepdims=True)


def _softmax_apply(s, v_ext):
    e = jnp.exp(s - jnp.max(s, axis=-1, keepdims=True))
    return _flash_out(_dot(e.astype(BF16), v_ext))


def _lambda_value(lp_ref, lam_init):
    lp = lp_ref[...]
    a = jnp.sum(lp[0:1] * lp[1:2], axis=-1, keepdims=True)
    b = jnp.sum(lp[2:3] * lp[3:4], axis=-1, keepdims=True)
    return jnp.exp(a) - jnp.exp(b) + lam_init


def _dif_head_out(acc1, acc2, lam, gh, lam_init):
    o = _flash_out(acc1) - lam * _flash_out(acc2)
    o = o * lax.rsqrt(jnp.mean(o * o, axis=-1, keepdims=True) + EPS) * gh
    return o * (1.0 - lam_init)


def _row_select(row, step, vals):
    out = vals[-1]
    for g in range(len(vals) - 2, -1, -1):
        out = jnp.where(row < (g + 1) * step, vals[g], out)
    return out


EP_T = 512


def _even_prompt_body(sl_ref, qa_ref, ka_ref, va_ref, qf_ref, kf_ref, vf_ref, ck_ref, lp_ref, gh_ref,
                      o_ref, qs_ref, m_ref, acc_ref, *, lam_init):
    h = pl.program_id(1)
    qi = pl.program_id(2)
    t = EP_T
    d = HEAD_DIM
    lane = lax.broadcasted_iota(jnp.int32, (t, d), 1)
    zero = jnp.zeros((t, d), BF16)
    for g in range(2):
        q = qa_ref[:, g * d:(g + 1) * d] * jnp.asarray(0.125, BF16)
        qs_ref[g, 0] = jnp.where(lane < d // 2, q, zero)
        qs_ref[g, 1] = jnp.where(lane >= d // 2, q, zero)
        qs_ref[g, 2] = qf_ref[:, g * d:(g + 1) * d]
    m_ref[...] = jnp.full(m_ref.shape, NEG, F32)
    acc_ref[...] = jnp.zeros(acc_ref.shape, F32)
    scale_f = d ** -0.5

    def tile(ki, diag):
        k0 = pl.multiple_of(ki * t, t)
        ka = ka_ref[pl.ds(k0, t), :]
        va = _with_ones(va_ref[pl.ds(k0, t), :])
        kf = kf_ref[pl.ds(k0, t), :]
        vf = _with_ones(vf_ref[pl.ds(k0, t), :])
        rel = (k0 - qi * t) + lax.broadcasted_iota(jnp.int32, (1, t), 1)
        relf = rel.astype(F32)
        if diag:
            mask = rel <= lax.broadcasted_iota(jnp.int32, (t, 1), 0)
        for g in range(2):
            bias_a = sl_ref[2 * h + g] * relf
            bias_f = -ck_ref[g:g + 1, pl.ds(k0, t)]
            for kind in range(3):
                if kind < 2:
                    s = _dot_nt(qs_ref[g, kind], ka) + bias_a
                    v = va
                else:
                    s = _dot_nt(qs_ref[g, 2], kf) * scale_f + bias_f
                    v = vf
                if diag:
                    s = jnp.where(mask, s, NEG)
                _online_update(s, v, m_ref, acc_ref, (g, kind))

    def body(ki, carry):
        tile(ki, False)
        return carry

    lax.fori_loop(0, qi, body, 0)
    tile(qi, True)

    lam = _lambda_value(lp_ref, lam_init)
    gh = gh_ref[...]
    for g in range(2):
        od = _dif_head_out(acc_ref[g, 0], acc_ref[g, 1], lam, gh, lam_init)
        of = _flash_out(acc_ref[g, 2])
        o_ref[:, g * d:(g + 1) * d] = od.astype(o_ref.dtype)
        o_ref[:, (2 + g) * d:(3 + g) * d] = of.astype(o_ref.dtype)


def _even_prompt_attn(qkv16, ck, lam_p, g_head, n_batch, seq, lam_init):
    t = EP_T
    nq = seq // t
    d = HEAD_DIM
    kv_spec = lambda col0: pl.BlockSpec((seq, d), lambda b, h, qi: (b, col0 + h))
    q_spec = lambda col0: pl.BlockSpec((t, 2 * d), lambda b, h, qi: (b * nq + qi, col0 + h))
    return pl.pallas_call(
        functools.partial(_even_prompt_body, lam_init=lam_init),
        grid=(n_batch, A_KV, nq),
        in_specs=[
            pl.BlockSpec(memory_space=pltpu.SMEM),
            q_spec(0),
            kv_spec(8), kv_spec(12),
            q_spec(8),
            kv_spec(24), kv_spec(28),
            pl.BlockSpec((None, None, 2, seq), lambda b, h, qi: (b, h, 0, 0)),
            pl.BlockSpec((4, d // 2), lambda b, h, qi: (0, 0)),
            pl.BlockSpec((1, d), lambda b, h, qi: (0, 0)),
        ],
        out_specs=pl.BlockSpec((t, 4 * d), lambda b, h, qi: (b * nq + qi, h)),
        scratch_shapes=[
            pltpu.VMEM((2, 3, t, d), BF16),
            pltpu.VMEM((2, 3, t, LANE), F32),
            pltpu.VMEM((2, 3, t, 2 * d), F32),
        ],
        out_shape=jax.ShapeDtypeStruct((n_batch * seq, 4 * A_KV * d), BF16),
        compiler_params=_cparams(("parallel", "parallel", "arbitrary")),
    )(_alibi_slopes(A_HEADS), qkv16, qkv16, qkv16, qkv16, qkv16, qkv16, ck, lam_p, g_head.reshape(1, d))


def _even_sample_body(*refs, n_pages, past_len, dec, lam_init):
    p_cnt = PAGES_PER_STEP
    pt_ref, sl_ref, qkv_ref, ck_ref, ckn_ref, lp_ref, gh_ref = refs[:7]
    pages = refs[7:7 + 4 * p_cnt]
    dk, dv, fk, fv = (pages[i * p_cnt:(i + 1) * p_cnt] for i in range(4))
    o_ref, qs_ref, m_ref, acc_ref = refs[7 + 4 * p_cnt:]
    del pt_ref
    c = pl.program_id(1)
    d = HEAD_DIM
    rows = 2 * dec
    row = lax.broadcasted_iota(jnp.int32, (rows, 1), 0)
    first = row < dec
    scale_f = d ** -0.5
    lane = lax.broadcasted_iota(jnp.int32, (rows, d), 1)

    def stacked(col0):
        return jnp.concatenate([qkv_ref[:, col0:col0 + d], qkv_ref[:, col0 + d:col0 + 2 * d]], axis=0)

    @pl.when(c == 0)
    def _():
        for h in range(A_KV):
            q = (stacked(h * 2 * d) * 0.125).astype(BF16)
            zero = jnp.zeros_like(q)
            qs_ref[h, 0] = jnp.where(lane < d // 2, q, zero)
            qs_ref[h, 1] = jnp.where(lane >= d // 2, q, zero)
            qs_ref[h, 2] = stacked(2048 + h * 2 * d).astype(BF16)
        m_ref[...] = jnp.full(m_ref.shape, NEG, F32)
        acc_ref[...] = jnp.zeros(acc_ref.shape, F32)

    def head_keys(prefs, h):
        return jnp.concatenate([_head_rows(r, h, LANE) for r in prefs], axis=0).astype(BF16)

    n_keys = p_cnt * LANE
    relf = (c * n_keys - past_len + lax.broadcasted_iota(jnp.int32, (1, n_keys), 1)).astype(F32)

    def slope_col(h):
        return jnp.where(first, sl_ref[2 * h], sl_ref[2 * h + 1])

    def ck_rows(ref, h):
        return jnp.where(first, ref[2 * h:2 * h + 1, :], ref[2 * h + 1:2 * h + 2, :])

    for h in range(A_KV):
        ka, kf = head_keys(dk, h), head_keys(fk, h)
        va, vf = _with_ones(head_keys(dv, h)), _with_ones(head_keys(fv, h))
        bias_a = slope_col(h) * relf
        bias_f = -ck_rows(ck_ref, h)
        for kind in range(3):
            if kind < 2:
                s = _dot_nt(qs_ref[h, kind], ka) + bias_a
                v = va
            else:
                s = _dot_nt(qs_ref[h, 2], kf) * scale_f + bias_f
                v = vf
            _online_update(s, v, m_ref, acc_ref, (h, kind))

    @pl.when(c == n_pages // p_cnt - 1)
    def _():
        jn = lax.broadcasted_iota(jnp.int32, (1, LANE), 1)
        tok = jnp.where(first, row, row - dec)
        mask_n = (jn < dec) & (jn <= tok)
        reln = jn.astype(F32)
        pad = jnp.zeros((LANE - dec, d), F32)
        lam = _lambda_value(lp_ref, lam_init)
        gh = gh_ref[...]
        for h in range(A_KV):
            def new_rows(col0):
                return jnp.concatenate([qkv_ref[:, col0 + h * d:col0 + (h + 1) * d], pad], axis=0).astype(BF16)
            ka, kf = new_rows(1024), new_rows(3072)
            va, vf = _with_ones(new_rows(1536)), _with_ones(new_rows(3584))
            bias_a = slope_col(h) * reln
            bias_f = -ck_rows(ckn_ref, h)
            for kind in range(3):
                if kind < 2:
                    s = _dot_nt(qs_ref[h, kind], ka) + bias_a
                    v = va
                else:
                    s = _dot_nt(qs_ref[h, 2], kf) * scale_f + bias_f
                    v = vf
                s = jnp.where(mask_n, s, NEG)
                _online_update(s, v, m_ref, acc_ref, (h, kind))
            od = _dif_head_out(acc_ref[h, 0], acc_ref[h, 1], lam, gh, lam_init)
            of = _flash_out(acc_ref[h, 2])
            for g in range(2):
                o_ref[:, (4 * h + g) * d:(4 * h + g + 1) * d] = od[g * dec:(g + 1) * dec]
                o_ref[:, (4 * h + 2 + g) * d:(4 * h + 3 + g) * d] = of[g * dec:(g + 1) * dec]


def _page_spec(p_cnt, i):
    return pl.BlockSpec((None, LANE * KV, HEAD_DIM), lambda s, c, pt: (pt[s, c * p_cnt + i], 0, 0))


def _even_sample_attn(qkv32, caches, page_table, ck, lam_p, g_head, lam_init, dec):
    n_seq, n_pages = page_table.shape
    past_len = n_pages * LANE
    p_cnt = PAGES_PER_STEP
    d = HEAD_DIM
    n_chunks = n_pages // p_cnt
    in_specs = [
        pl.BlockSpec(memory_space=pltpu.SMEM),
        pl.BlockSpec((dec, qkv32.shape[1]), lambda s, c, pt: (s, 0)),
        pl.BlockSpec((None, 8, p_cnt * LANE), lambda s, c, pt: (s, 0, c)),
        pl.BlockSpec((None, 8, LANE), lambda s, c, pt: (s, 0, n_pages)),
        pl.BlockSpec((4, d // 2), lambda s, c, pt: (0, 0)),
        pl.BlockSpec((1, d), lambda s, c, pt: (0, 0)),
    ]
    ins = [_alibi_slopes(A_HEADS), qkv32, ck, ck, lam_p, g_head.reshape(1, d)]
    for arr in caches:
        for i in range(p_cnt):
            in_specs.append(_page_spec(p_cnt, i))
            ins.append(arr)
    rows = 2 * dec
    grid_spec = pltpu.PrefetchScalarGridSpec(
        num_scalar_prefetch=1,
        grid=(n_seq, n_chunks),
        in_specs=in_specs,
        out_specs=pl.BlockSpec((dec, 16 * d), lambda s, c, pt: (s, 0)),
        scratch_shapes=[
            pltpu.VMEM((A_KV, 3, rows, d), BF16),
            pltpu.VMEM((A_KV, 3, rows, LANE), F32),
            pltpu.VMEM((A_KV, 3, rows, 2 * d), F32),
        ],
    )
    return pl.pallas_call(
        functools.partial(_even_sample_body, n_pages=n_pages, past_len=past_len, dec=dec, lam_init=lam_init),
        grid_spec=grid_spec,
        out_shape=jax.ShapeDtypeStruct((n_seq * dec, 16 * d), F32),
        compiler_params=_cparams(("parallel", "arbitrary")),
    )(page_table, *ins)


def _even_out_perm():
    idx = []
    for h in range(A_KV):
        for base in (0, A_HEADS):
            for g in range(2):
                head = base + 2 * h + g
                idx.extend(range(head * HEAD_DIM, (head + 1) * HEAD_DIM))
    return np.asarray(idx, np.int32)


def _pos_softmax_body(pk_ref, pv_ref, wk_ref, wv_ref):
    for src, dst in ((pk_ref, wk_ref), (pv_ref, wv_ref)):
        x = src[...]
        e = jnp.exp(x - jnp.max(x, axis=1, keepdims=True))
        w = e / jnp.sum(e, axis=1, keepdims=True)
        dst[...] = jnp.concatenate([w, w], axis=1)


def _pos_softmax(pos_k, pos_v):
    expand = lambda p: jnp.broadcast_to(p.astype(F32).T[:, :, None], (C_KV, CMP_BLOCK, HEAD_DIM))
    shape = jax.ShapeDtypeStruct((C_KV, 2 * CMP_BLOCK, HEAD_DIM), F32)
    return pl.pallas_call(_pos_softmax_body, out_shape=(shape, shape))(expand(pos_k), expand(pos_v))


def _block_sums(x, w):
    r, width = x.shape
    xw = x.reshape(r // LANE, LANE, width) * w[None]
    return jnp.sum(xw.reshape(r // CMP_BLOCK, CMP_BLOCK, width), axis=1)


def _compress_rows_body(k_ref, v_ref, wk_ref, wv_ref, ok_ref, ov_ref):
    d = HEAD_DIM
    for h in range(C_KV):
        ok_ref[h] = _block_sums(k_ref[:, h * d:(h + 1) * d].astype(F32), wk_ref[h])
        ov_ref[h] = _block_sums(v_ref[:, h * d:(h + 1) * d].astype(F32), wv_ref[h])


def _compress_rows(qkv16, wk, wv, n_batch, seq, col_k, col_v):
    tr = 1024
    nc = seq // tr
    d = HEAD_DIM
    shape = jax.ShapeDtypeStruct((n_batch, C_KV, seq // CMP_BLOCK, d), F32)
    w_spec = pl.BlockSpec((C_KV, 2 * CMP_BLOCK, d), lambda b, i: (0, 0, 0))
    o_spec = pl.BlockSpec((None, C_KV, tr // CMP_BLOCK, d), lambda b, i: (b, 0, i, 0))
    return pl.pallas_call(
        _compress_rows_body,
        grid=(n_batch, nc),
        in_specs=[pl.BlockSpec((tr, KV_COLS), lambda b, i: (b * nc + i, col_k)),
                  pl.BlockSpec((tr, KV_COLS), lambda b, i: (b * nc + i, col_v)), w_spec, w_spec],
        out_specs=(o_spec, o_spec),
        out_shape=(shape, shape),
        compiler_params=_cparams(("parallel", "parallel")),
    )(qkv16, qkv16, wk, wv)


def _compress_paged_body(*refs):
    p_cnt = PAGES_PER_STEP_2
    kp = refs[1:1 + p_cnt]
    vp = refs[1 + p_cnt:1 + 2 * p_cnt]
    wk_ref, wv_ref, ok_ref, ov_ref = refs[1 + 2 * p_cnt:]
    for h in range(C_KV):
        wk, wv = wk_ref[h], wv_ref[h]
        ok_ref[h] = jnp.concatenate([_block_sums(_head_rows(r, h, LANE), wk) for r in kp], axis=0)
        ov_ref[h] = jnp.concatenate([_block_sums(_head_rows(r, h, LANE), wv) for r in vp], axis=0)


def _compress_paged(cache_k, cache_v, page_table, wk, wv):
    n_seq, n_pages = page_table.shape
    p_cnt = PAGES_PER_STEP_2
    d = HEAD_DIM
    per_step = p_cnt * LANE // CMP_BLOCK
    w_spec = pl.BlockSpec((C_KV, 2 * CMP_BLOCK, d), lambda s, c, pt: (0, 0, 0))
    o_spec = pl.BlockSpec((None, C_KV, per_step, d), lambda s, c, pt: (s, 0, c, 0))
    shape = jax.ShapeDtypeStruct((n_seq, C_KV, n_pages * LANE // CMP_BLOCK, d), F32)
    grid_spec = pltpu.PrefetchScalarGridSpec(
        num_scalar_prefetch=1,
        grid=(n_seq, n_pages // p_cnt),
        in_specs=[_page_spec(p_cnt, i) for i in range(p_cnt)] * 2 + [w_spec, w_spec],
        out_specs=(o_spec, o_spec),
    )
    return pl.pallas_call(
        _compress_paged_body,
        grid_spec=grid_spec,
        out_shape=(shape, shape),
        compiler_params=_cparams(("parallel", "arbitrary")),
    )(page_table, *([cache_k] * p_cnt), *([cache_v] * p_cnt), wk, wv)


def _top_n_mask(score, n_blocks, valid):
    lane = lax.broadcasted_iota(jnp.int32, score.shape, 1)
    rank = jnp.zeros(score.shape, F32)
    for i in range(n_blocks):
        col = score[:, i:i + 1]
        ahead = (col > score) | ((col == score) & (lane > i))
        rank = rank + jnp.where(ahead, 1.0, 0.0)
    keep = (rank < float(TOP_N)) & (lane < n_blocks) & valid
    return jnp.where(keep, 1.0, 0.0)


def _top_n_mask_cols(score, valid):
    blk = lax.broadcasted_iota(jnp.int32, score.shape, 0)
    rank = jnp.zeros(score.shape, F32)
    for i in range(score.shape[0]):
        row = score[i:i + 1, :]
        ahead = (row > score) | ((row == score) & (blk > i))
        rank = rank + jnp.where(ahead, 1.0, 0.0)
    return jnp.where((rank < float(TOP_N)) & valid, 1.0, 0.0)


def _eye(n, dtype):
    r = lax.broadcasted_iota(jnp.int32, (n, n), 0)
    c = lax.broadcasted_iota(jnp.int32, (n, n), 1)
    return jnp.where(r == c, 1.0, 0.0).astype(dtype)


def _stack_heads(ref, col0, n):
    return jnp.concatenate([ref[:, col0 + g * HEAD_DIM:col0 + (g + 1) * HEAD_DIM] for g in range(n)], axis=0)


OP_T = 256
OP_TK = 512


def _odd_prompt_body(sl_ref, q_ref, ks_ref, vs_ref, kw_ref, vw_ref, kc_ref, vc_ref, gt_ref,
                     o_ref, sx_ref, m_ref, acc_ref, *, seq):
    h = pl.program_id(1)
    qi = pl.program_id(2)
    t = OP_T
    tk = OP_TK
    d = HEAD_DIM
    gc = C_HEADS // C_KV
    rows = gc * t
    nb = seq // CMP_BLOCK
    scale = d ** -0.5
    q = _stack_heads(q_ref, 0, gc)
    row = lax.broadcasted_iota(jnp.int32, (rows, 1), 0)
    slopes = [sl_ref[gc * h + g] for g in range(gc)]
    slope = _row_select(row, t, slopes)
    tok = _row_select(row, t, [row - g * t for g in range(gc)])
    tok1 = lax.broadcasted_iota(jnp.int32, (t, 1), 0)
    q0 = qi * t

    blk = lax.broadcasted_iota(jnp.int32, (1, nb), 1)
    bend_rel = (blk + 1) * CMP_BLOCK - 1 - q0
    mask_c = bend_rel <= tok
    s_c = _dot_nt(q, kc_ref[...].astype(BF16)) * scale + slope * bend_rel.astype(F32)
    p_c = jnp.where(mask_c, _softmax_rows(jnp.where(mask_c, s_c, NEG)), 0.0)
    o_c = _dot(p_c.astype(BF16), vc_ref[...].astype(BF16))

    imp = p_c[0:t]
    for g in range(1, gc):
        imp = imp + p_c[g * t:(g + 1) * t]
    eye_nb = _eye(nb, F32)
    imp_t = lax.dot_general(eye_nb, imp, (((1,), (1,)), ((), ())), preferred_element_type=F32,
                            precision=lax.Precision.HIGHEST)
    blk_c = lax.broadcasted_iota(jnp.int32, (nb, 1), 0)
    pos_r = q0 + lax.broadcasted_iota(jnp.int32, (1, t), 1)
    cur = pos_r >> SEL_SHIFT
    forced = (blk_c == 0) | (blk_c == cur) | (blk_c == cur - 1)
    valid = blk_c * SEL_BLOCK <= pos_r
    score_t = jnp.where(valid, imp_t + jnp.where(forced, FORCE, 0.0), NEG)
    sel_t = _top_n_mask_cols(score_t, valid).astype(BF16)
    sel = _dot_nt(_eye(t, BF16), sel_t).astype(BF16)
    er = lax.broadcasted_iota(jnp.int32, (nb, seq), 0)
    ec = lax.broadcasted_iota(jnp.int32, (nb, seq), 1)
    expand = jnp.where(er == (ec >> SEL_SHIFT), 1.0, 0.0).astype(BF16)
    sel_keys = _dot(sel, expand)
    for kt in range(seq // tk):
        sx_ref[kt] = (sel_keys[:, kt * tk:(kt + 1) * tk] - 1.0) * (-NEG)

    m_ref[...] = jnp.full(m_ref.shape, NEG, F32)
    acc_ref[...] = jnp.zeros(acc_ref.shape, F32)

    def tile(ki, diag):
        k0 = pl.multiple_of(ki * tk, tk)
        k = ks_ref[pl.ds(k0, tk), :]
        v = _with_ones(vs_ref[pl.ds(k0, tk), :])
        rel = (k0 - q0) + lax.broadcasted_iota(jnp.int32, (1, tk), 1)
        relf = rel.astype(F32)
        madd = sx_ref[ki]
        if diag:
            madd = jnp.where(rel <= tok1, madd, NEG)
        s_all = _dot_nt(q, k) * scale
        for g in range(gc):
            s = s_all[g * t:(g + 1) * t] + (slopes[g] * relf + madd)
            _online_update(s, v, m_ref, acc_ref, g)

    def body(ki, carry):
        tile(ki, False)
        return carry

    n_full = (qi * t) // tk
    lax.fori_loop(0, n_full, body, 0)
    tile(n_full, True)

    n_slab = WINDOW + t
    start = pl.multiple_of(jnp.maximum(q0 - WINDOW, 0), t)
    kw = kw_ref[pl.ds(start, n_slab), :]
    vw = _with_ones(vw_ref[pl.ds(start, n_slab), :])
    rel_w = (start - q0) + lax.broadcasted_iota(jnp.int32, (1, n_slab), 1)
    delta = tok - rel_w
    mask_w = (delta >= 0) & (delta < WINDOW)
    s_w = _dot_nt(q, kw) * scale + slope * rel_w.astype(F32)
    o_w = _softmax_apply(jnp.where(mask_w, s_w, NEG), vw)

    gates = jax.nn.sigmoid(gt_ref[...])
    for g in range(gc):
        rs = slice(g * t, (g + 1) * t)
        o = (gates[:, 3 * g:3 * g + 1] * o_c[rs] + gates[:, 3 * g + 1:3 * g + 2] * _flash_out(acc_ref[g])
             + gates[:, 3 * g + 2:3 * g + 3] * o_w[rs])
        o_ref[:, g * d:(g + 1) * d] = o.astype(o_ref.dtype)


def _odd_prompt_attn(qkv16, k_cmp, v_cmp, gates, n_batch, seq):
    t = OP_T
    nq = seq // t
    d = HEAD_DIM
    gc = C_HEADS // C_KV
    assert seq % OP_TK == 0 and OP_TK % t == 0
    kv_spec = lambda col0: pl.BlockSpec((seq, d), lambda b, h, qi: (b, col0 + h))
    cmp_spec = pl.BlockSpec((None, None, seq // CMP_BLOCK, d), lambda b, h, qi: (b, h, 0, 0))
    return pl.pallas_call(
        functools.partial(_odd_prompt_body, seq=seq),
        grid=(n_batch, C_KV, nq),
        in_specs=[
            pl.BlockSpec(memory_space=pltpu.SMEM),
            pl.BlockSpec((t, gc * d), lambda b, h, qi: (b * nq + qi, h)),
            kv_spec(24), kv_spec(28), kv_spec(32), kv_spec(36),
            cmp_spec, cmp_spec,
            pl.BlockSpec((t, d), lambda b, h, qi: (b * nq + qi, h)),
        ],
        out_specs=pl.BlockSpec((t, gc * d), lambda b, h, qi: (b * nq + qi, h)),
        scratch_shapes=[
            pltpu.VMEM((seq // OP_TK, t, OP_TK), F32),
            pltpu.VMEM((gc, t, LANE), F32),
            pltpu.VMEM((gc, t, 2 * d), F32),
        ],
        out_shape=jax.ShapeDtypeStruct((n_batch * seq, C_HEADS * d), BF16),
        compiler_params=_cparams(("parallel", "parallel", "arbitrary")),
    )(_alibi_slopes(C_HEADS), qkv16, qkv16, qkv16, qkv16, qkv16, k_cmp, v_cmp, gates)


def _odd_sample_body(*refs, n_pages, past_len, dec):
    p_cnt = PAGES_PER_STEP_2
    pt_ref, sl_ref, qkv_ref, gt_ref, kc_ref, vc_ref, wk_ref, wv_ref = refs[:8]
    sk = refs[8:8 + p_cnt]
    sv = refs[8 + p_cnt:8 + 2 * p_cnt]
    o_ref, wko_ref, wvo_ref, qs_ref, sel_ref, oc_ref, m_ref, acc_ref = refs[8 + 2 * p_cnt:]
    del pt_ref
    c = pl.program_id(1)
    d = HEAD_DIM
    gc = C_HEADS // C_KV
    rows = gc * dec
    nb = past_len // CMP_BLOCK
    nbs = -(-(past_len + dec) // SEL_BLOCK)
    sel_w = 2 * LANE
    assert nb == LANE and nbs <= sel_w and (past_len + dec - 1) // SEL_BLOCK == past_len // SEL_BLOCK
    cur = past_len // SEL_BLOCK
    scale = d ** -0.5
    row = lax.broadcasted_iota(jnp.int32, (rows, 1), 0)
    tok = _row_select(row, dec, [row - g * dec for g in range(gc)])
    slopes = [_row_select(row, dec, [sl_ref[gc * h + g] for g in range(gc)]) for h in range(C_KV)]
    c_ks, c_vs, c_kw, c_vw = ((C_HEADS + i * C_KV) * d for i in (2, 3, 4, 5))

    @pl.when(c == 0)
    def _():
        blk = lax.broadcasted_iota(jnp.int32, (1, nb), 1)
        bend_rel = ((blk + 1) * CMP_BLOCK - 1 - past_len).astype(F32)
        lane = lax.broadcasted_iota(jnp.int32, (dec, sel_w), 1)
        forced = (lane == 0) | (lane == cur) | (lane == cur - 1)
        for h in range(C_KV):
            q = _stack_heads(qkv_ref, h * gc * d, gc).astype(BF16)
            qs_ref[h] = q
            p_c = _softmax_rows(_dot_nt(q, kc_ref[h].astype(BF16)) * scale + slopes[h] * bend_rel)
            oc_ref[h] = _dot(p_c.astype(BF16), vc_ref[h].astype(BF16))
            imp = p_c[0:dec]
            for g in range(1, gc):
                imp = imp + p_c[g * dec:(g + 1) * dec]
            imp = jnp.concatenate([imp, jnp.zeros((dec, sel_w - nb), F32)], axis=1)
            score = imp + jnp.where(forced, FORCE, 0.0)
            sel_ref[h] = _top_n_mask(score, nbs, lane < nbs)
        m_ref[...] = jnp.full(m_ref.shape, NEG, F32)
        acc_ref[...] = jnp.zeros(acc_ref.shape, F32)

    n_keys = p_cnt * LANE
    relf = (c * n_keys - past_len + lax.broadcasted_iota(jnp.int32, (1, n_keys), 1)).astype(F32)
    er = lax.broadcasted_iota(jnp.int32, (sel_w, n_keys), 0)
    ec = lax.broadcasted_iota(jnp.int32, (sel_w, n_keys), 1)
    expand = jnp.where(er - c * (n_keys // SEL_BLOCK) == (ec >> SEL_SHIFT), 1.0, 0.0).astype(BF16)
    for h in range(C_KV):
        k = jnp.concatenate([_head_rows(r, h, LANE) for r in sk], axis=0).astype(BF16)
        v = _with_ones(jnp.concatenate([_head_rows(r, h, LANE) for r in sv], axis=0).astype(BF16))
        keys = _dot(sel_ref[h].astype(BF16), expand)
        madd = jnp.concatenate([(keys - 1.0) * (-NEG)] * gc, axis=0)
        s = _dot_nt(qs_ref[h], k) * scale + (slopes[h] * relf + madd)
        _online_update(s, v, m_ref, acc_ref, h)

    @pl.when(c == n_pages // p_cnt - 1)
    def _():
        jn = lax.broadcasted_iota(jnp.int32, (1, LANE), 1)
        reln = jn.astype(F32)
        causal_n = (jn < dec) & (jn <= tok)
        pad = jnp.zeros((LANE - dec, d), F32)
        gates = jax.nn.sigmoid(gt_ref[...])
        n_win = WINDOW + LANE
        je = lax.broadcasted_iota(jnp.int32, (1, n_win), 1)
        rel_w = je - WINDOW
        delta = tok - rel_w
        mask_w = (delta >= 0) & (delta < WINDOW) & (je < WINDOW + dec)
        for h in range(C_KV):
            q = qs_ref[h]
            new = lambda col0: jnp.concatenate([qkv_ref[:, col0 + h * d:col0 + (h + 1) * d], pad], axis=0)
            picked = jnp.concatenate([sel_ref[h][:, cur:cur + 1]] * gc, axis=0) > 0.5
            s = _dot_nt(q, new(c_ks).astype(BF16)) * scale + slopes[h] * reln
            _online_update(jnp.where(causal_n & picked, s, NEG), _with_ones(new(c_vs).astype(BF16)), m_ref, acc_ref, h)
            o_s = _flash_out(acc_ref[h])
            kw = jnp.concatenate([_head_rows(wk_ref, h, WINDOW), new(c_kw)], axis=0).astype(BF16)
            vw = jnp.concatenate([_head_rows(wv_ref, h, WINDOW), new(c_vw)], axis=0).astype(BF16)
            s_w = _dot_nt(q, kw) * scale + slopes[h] * rel_w.astype(F32)
            o_w = _softmax_apply(jnp.where(mask_w, s_w, NEG), _with_ones(vw))
            o_c = oc_ref[h]
            for g in range(gc):
                rs = slice(g * dec, (g + 1) * dec)
                l0 = h * d + 3 * g
                o = (gates[:, l0:l0 + 1] * o_c[rs] + gates[:, l0 + 1:l0 + 2] * o_s[rs]
                     + gates[:, l0 + 2:l0 + 3] * o_w[rs])
                o_ref[:, (gc * h + g) * d:(gc * h + g + 1) * d] = o
            wko_ref[pl.ds((WINDOW - dec) * KV + h, dec, stride=KV), :] = qkv_ref[:, c_kw + h * d:c_kw + (h + 1) * d]
            wvo_ref[pl.ds((WINDOW - dec) * KV + h, dec, stride=KV), :] = qkv_ref[:, c_vw + h * d:c_vw + (h + 1) * d]
        wko_ref[0:(WINDOW - dec) * KV, :] = wk_ref[dec * KV:WINDOW * KV, :]
        wvo_ref[0:(WINDOW - dec) * KV, :] = wv_ref[dec * KV:WINDOW * KV, :]


def _odd_sample_attn(qkv32, gates, gate_block0, k_cmp, v_cmp, win_k, win_v, cache_k, cache_v, page_table, dec):
    n_seq, n_pages = page_table.shape
    past_len = n_pages * LANE
    p_cnt = PAGES_PER_STEP_2
    d = HEAD_DIM
    gc = C_HEADS // C_KV
    rows = gc * dec
    cmp_spec = pl.BlockSpec((None, C_KV, past_len // CMP_BLOCK, d), lambda s, c, pt: (s, 0, 0, 0))
    win_spec = pl.BlockSpec((None, WINDOW * KV, d), lambda s, c, pt: (s, 0, 0))
    in_specs = [
        pl.BlockSpec(memory_space=pltpu.SMEM),
        pl.BlockSpec((dec, qkv32.shape[1]), lambda s, c, pt: (s, 0)),
        pl.BlockSpec((dec, KV_COLS), lambda s, c, pt: (gate_block0 + s, 0)),
        cmp_spec, cmp_spec, win_spec, win_spec,
    ] + [_page_spec(p_cnt, i) for i in range(p_cnt)] * 2
    grid_spec = pltpu.PrefetchScalarGridSpec(
        num_scalar_prefetch=1,
        grid=(n_seq, n_pages // p_cnt),
        in_specs=in_specs,
        out_specs=(pl.BlockSpec((dec, C_HEADS * d), lambda s, c, pt: (s, 0)), win_spec, win_spec),
        scratch_shapes=[
            pltpu.VMEM((C_KV, rows, d), BF16),
            pltpu.VMEM((C_KV, dec, 2 * LANE), F32),
            pltpu.VMEM((C_KV, rows, d), F32),
            pltpu.VMEM((C_KV, rows, LANE), F32),
            pltpu.VMEM((C_KV, rows, 2 * d), F32),
        ],
    )
    win_shape = jax.ShapeDtypeStruct((n_seq, WINDOW * KV, d), F32)
    return pl.pallas_call(
        functools.partial(_odd_sample_body, n_pages=n_pages, past_len=past_len, dec=dec),
        grid_spec=grid_spec,
        out_shape=(jax.ShapeDtypeStruct((n_seq * dec, C_HEADS * d), F32), win_shape, win_shape),
        compiler_params=_cparams(("parallel", "arbitrary")),
    )(page_table, _alibi_slopes(C_HEADS), qkv32, gates, k_cmp, v_cmp, win_k, win_v,
      *([cache_k] * p_cnt), *([cache_v] * p_cnt))


def _xattn_body(q_ref, k_ref, v_ref, o_ref):
    d = HEAD_DIM
    scale = d ** -0.5
    tq = q_ref.shape[0]
    n_mem = k_ref.shape[0] // KV
    pad_rows = (-tq) % 16
    for h in range(X_HEADS):
        cols = slice(h * d, (h + 1) * d)
        q = q_ref[:, cols]
        if pad_rows:
            q = jnp.concatenate([q, jnp.zeros((pad_rows, d), F32)], axis=0)
        s = _dot_nt(q.astype(BF16), _head_rows(k_ref, h, n_mem).astype(BF16)) * scale
        o = _softmax_apply(s, _with_ones(_head_rows(v_ref, h, n_mem).astype(BF16)))
        o_ref[:, cols] = o[0:tq].astype(o_ref.dtype)


def _xattn(q, mem_k, mem_v, layer, row_block0, tq, n_b, nq):
    width = X_HEADS * HEAD_DIM
    mem_spec = pl.BlockSpec((None, None, mem_k.shape[2], HEAD_DIM), lambda b, i: (layer, b, 0, 0))
    return pl.pallas_call(
        _xattn_body,
        grid=(n_b, nq),
        in_specs=[pl.BlockSpec((tq, width), lambda b, i: (row_block0 + b * nq + i, 0)), mem_spec, mem_spec],
        out_specs=pl.BlockSpec((tq, width), lambda b, i: (b * nq + i, 0)),
        out_shape=jax.ShapeDtypeStruct((n_b * nq * tq, width), F32),
        compiler_params=_cparams(("parallel", "arbitrary")),
    )(q, mem_k, mem_v)


def _router_body(x_ref, g_ref, w_ref, b_ref, o_ref):
    x = x_ref[...]
    hn = x * lax.rsqrt(jnp.mean(x * x, axis=-1, keepdims=True) + EPS) * g_ref[...]
    logits = jnp.dot(hn, w_ref[...], preferred_element_type=F32, precision=lax.Precision.HIGHEST) + b_ref[...]
    lane = lax.broadcasted_iota(jnp.int32, logits.shape, 1).astype(F32)
    big = float(2 * LANE)
    ninf = -jnp.inf

    def first_max(v):
        m = jnp.max(v, axis=-1, keepdims=True)
        return m, jnp.min(jnp.where(v == m, lane, big), axis=-1, keepdims=True)

    lgm = jnp.where(lane < N_GROUPS, logits, ninf)
    mg, gi = first_max(lgm)
    pg_top = 1.0 / jnp.sum(jnp.exp(lgm - mg), axis=-1, keepdims=True)
    lo = N_GROUPS + gi * EXP_PER_GROUP
    lem = jnp.where((lane >= lo) & (lane < lo + EXP_PER_GROUP), logits, ninf)
    m1, i1 = first_max(lem)
    lem2 = jnp.where(lane == i1, ninf, lem)
    m2, i2 = first_max(lem2)
    z = jnp.sum(jnp.exp(lem - m1), axis=-1, keepdims=True)
    pe1 = 1.0 / z
    pe2 = jnp.exp(m2 - m1) / z
    w1 = pg_top * pe1 / (pe1 + pe2)
    w2 = pg_top * pe2 / (pe1 + pe2)
    e1 = i1 - N_GROUPS
    e2 = i2 - N_GROUPS
    o_ref[...] = jnp.where(lane == 0, e1, jnp.where(lane == 1, e2, jnp.where(lane == 2, w1, jnp.where(lane == 3, w2, 0.0))))


def _router(x, g, w_grp, b_grp, w_er, b_er, tm):
    n, d = x.shape
    pad = LANE - N_GROUPS - N_EXPERTS
    w = jnp.pad(jnp.concatenate([w_grp, w_er], axis=1).astype(F32), ((0, 0), (0, pad)))
    b = jnp.pad(jnp.concatenate([b_grp, b_er]).astype(F32), (0, pad)).reshape(1, LANE)
    return pl.pallas_call(
        _router_body,
        grid=(n // tm,),
        in_specs=[pl.BlockSpec((tm, d), lambda i: (i, 0)), pl.BlockSpec((1, d), lambda i: (0, 0)),
                  pl.BlockSpec((d, LANE), lambda i: (0, 0)), pl.BlockSpec((1, LANE), lambda i: (0, 0))],
        out_specs=pl.BlockSpec((tm, LANE), lambda i: (i, 0)),
        out_shape=jax.ShapeDtypeStruct((n, LANE), F32),
        compiler_params=_cparams(("parallel",)),
    )(x, g.reshape(1, d).astype(F32), w, b)


def _moe_max_tiles(n_rows):
    return n_rows // MOE_TM + N_EXPERTS


ROW_ALIGN = 8


def _moe_rows(n):
    return 2 * n + ROW_ALIGN * N_EXPERTS


def _dispatch_tables(e1, e2):
    n = e1.shape[0]
    tm = MOE_TM
    max_tiles = _moe_max_tiles(2 * n)
    e = jnp.concatenate([e1, e2])
    tok = jnp.concatenate([jnp.arange(n, dtype=jnp.int32)] * 2)
    onehot = (e[:, None] == jnp.arange(N_EXPERTS, dtype=jnp.int32)[None, :]).astype(jnp.int32)
    counts = jnp.sum(onehot, axis=0)
    rank = jnp.sum((jnp.cumsum(onehot, axis=0) - onehot) * onehot, axis=1)
    aligned = ((counts + ROW_ALIGN - 1) // ROW_ALIGN) * ROW_ALIGN
    starts = jnp.cumsum(aligned) - aligned
    pos = jnp.sum(starts[None, :] * onehot, axis=1) + rank
    src = jnp.zeros((_moe_rows(n) + tm,), jnp.int32).at[pos].set(tok)
    tiles_per = (counts + tm - 1) // tm
    tile_ends = jnp.cumsum(tiles_per)
    n_used = tile_ends[-1].astype(jnp.int32)
    t = jnp.arange(max_tiles, dtype=jnp.int32)
    tile_e = jnp.minimum(jnp.sum((tile_ends[None, :] <= t[:, None]).astype(jnp.int32), axis=1), N_EXPERTS - 1)
    eh = (tile_e[:, None] == jnp.arange(N_EXPERTS, dtype=jnp.int32)[None, :]).astype(jnp.int32)
    pick = lambda v: jnp.sum(eh * v[None, :], axis=1)
    in_e = t - pick(tile_ends - tiles_per)
    used = t < n_used
    tile_start = jnp.where(used, pick(starts) + in_e * tm, 0).astype(jnp.int32)
    tile_rows = jnp.where(used, jnp.clip(pick(counts) - in_e * tm, 0, tm), 0).astype(jnp.int32)
    used_rows = jnp.stack([n_used, jnp.sum(aligned).astype(jnp.int32)])
    return tile_e.astype(jnp.int32), tile_start, tile_rows, used_rows, src, pos[:n], pos[n:]


ROW_GROUP = 8


def _row_copy(idx_ref, i, r, src_hbm, dst, sem):
    return pltpu.make_async_copy(src_hbm.at[pl.ds(idx_ref[i], 1)], dst.at[pl.ds(r, 1)], sem)


def _start_rows(idx_ref, base, n, src_hbm, dst, sem):
    def start(j, carry):
        for u in range(ROW_GROUP):
            r = j * ROW_GROUP + u
            _row_copy(idx_ref, base + r, r, src_hbm, dst, sem).start(priority=u % 2)
        return carry
    lax.fori_loop(0, (n + ROW_GROUP - 1) // ROW_GROUP, start, 0)


def _wait_rows(idx_ref, base, n, src_hbm, dst, sem):
    def wait(j, carry):
        for u in range(ROW_GROUP):
            r = j * ROW_GROUP + u
            _row_copy(idx_ref, base + r, r, src_hbm, dst, sem).wait()
        return carry
    lax.fori_loop(0, (n + ROW_GROUP - 1) // ROW_GROUP, wait, 0)


def _experts_body(te_ref, ts_ref, tr_ref, nu_ref, src_ref, x_hbm, g_ref, wg_hbm, wu_hbm, wd_hbm, y_hbm,
                  xbuf, ybuf, hn_ref, wgb, wub, wdb, gsem, osem, wsem, *, layer, n_rows):
    t = pl.program_id(0)
    n_used = nu_ref[0]
    slot = t % 2
    tm = MOE_TM
    n_chunks = wgb.shape[0]
    fc = wgb.shape[2]

    def gather(fn, tile, sl):
        fn(src_ref, ts_ref[tile], tr_ref[tile], x_hbm, xbuf.at[sl], gsem.at[sl])

    def w_copies(tile, c):
        e = te_ref[tile]
        cols = pl.ds(c * fc, fc)
        return (pltpu.make_async_copy(wg_hbm.at[layer, e, :, cols], wgb.at[c], wsem.at[c]),
                pltpu.make_async_copy(wu_hbm.at[layer, e, :, cols], wub.at[c], wsem.at[c]),
                pltpu.make_async_copy(wd_hbm.at[layer, e, cols, :], wdb.at[c], wsem.at[c]))

    def out_rows(row0, sl):
        if not isinstance(row0, int):
            row0 = pl.multiple_of(row0, ROW_ALIGN)
        return pltpu.make_async_copy(ybuf.at[sl], y_hbm.at[pl.ds(row0, tm)], osem.at[sl])

    def out_copy(tile, sl):
        return out_rows(ts_ref[tile], sl)

    @pl.when(t == 0)
    def _():
        xbuf[...] = jnp.zeros(xbuf.shape, F32)
        gather(_start_rows, 0, 0)
        for cp in w_copies(0, 0):
            cp.start()

    @pl.when(t < n_used)
    def _():
        @pl.when(t + 1 < n_used)
        def _():
            gather(_start_rows, t + 1, 1 - slot)

        gather(_wait_rows, t, slot)
        x = xbuf[slot]
        hn_ref[...] = (x * lax.rsqrt(jnp.mean(x * x, axis=-1, keepdims=True) + EPS) * g_ref[...]).astype(BF16)
        for c in range(n_chunks):
            if c + 1 < n_chunks:
                for cp in w_copies(t, c + 1):
                    cp.start()
            else:
                @pl.when(t + 1 < n_used)
                def _():
                    for cp in w_copies(t + 1, 0):
                        cp.start()
            for cp in w_copies(t, c):
                cp.wait()
            hn = hn_ref[...]
            gate = _dot(hn, wgb[c].astype(BF16))
            up = _dot(hn, wub[c].astype(BF16))
            act = (gate * jax.nn.sigmoid(gate) * up).astype(BF16)
            part = _dot(act, wdb[c].astype(BF16))
            if c == 0:
                ybuf[slot] = part
            else:
                ybuf[slot] += part

        @pl.when(t >= 1)
        def _():
            out_copy(t - 1, 1 - slot).wait()

        out_copy(t, slot).start()

        @pl.when(t == n_used - 1)
        def _():
            out_copy(t, slot).wait()
            ybuf[1 - slot] = jnp.zeros(ybuf.shape[1:], F32)
            for row0 in (nu_ref[1], n_rows):
                tail = out_rows(row0, 1 - slot)
                tail.start()
                tail.wait()


def _experts(tile_e, tile_start, tile_rows, n_used, src, x, g, w_gate, w_up, w_down, layer):
    n_rows = src.shape[0] - MOE_TM
    assert MOE_TM >= ROW_ALIGN * N_EXPERTS
    d = x.shape[1]
    f = w_gate.shape[3]
    tm = MOE_TM
    fc = f // 2
    assert f % (2 * LANE) == 0
    grid_spec = pltpu.PrefetchScalarGridSpec(
        num_scalar_prefetch=5,
        grid=(tile_e.shape[0],),
        in_specs=[
            pl.BlockSpec(memory_space=pl.ANY),
            pl.BlockSpec((1, d), lambda t, *_: (0, 0)),
            pl.BlockSpec(memory_space=pl.ANY), pl.BlockSpec(memory_space=pl.ANY), pl.BlockSpec(memory_space=pl.ANY),
        ],
        out_specs=pl.BlockSpec(memory_space=pl.ANY),
        scratch_shapes=[pltpu.VMEM((2, tm, d), F32), pltpu.VMEM((2, tm, d), F32), pltpu.VMEM((tm, d), BF16),
                        pltpu.VMEM((2, d, fc), F32), pltpu.VMEM((2, d, fc), F32), pltpu.VMEM((2, fc, d), F32),
                        pltpu.SemaphoreType.DMA((2,)), pltpu.SemaphoreType.DMA((2,)), pltpu.SemaphoreType.DMA((2,))],
    )
    return pl.pallas_call(
        functools.partial(_experts_body, layer=layer, n_rows=n_rows),
        grid_spec=grid_spec,
        out_shape=jax.ShapeDtypeStruct((n_rows + tm, d), F32),
        compiler_params=_cparams(("arbitrary",)),
    )(tile_e, tile_start, tile_rows, n_used, src, x, g.reshape(1, d).astype(F32), w_gate, w_up, w_down)


def _combine_body(p1_ref, p2_ref, x_ref, r_ref, y_hbm, o_ref, ybuf, sem):
    i = pl.program_id(0)
    n = pl.num_programs(0)
    tm = x_ref.shape[0]
    slot = i % 2

    def start(step, sl):
        _start_rows(p1_ref, step * tm, tm, y_hbm, ybuf.at[sl, 0], sem.at[sl])
        _start_rows(p2_ref, step * tm, tm, y_hbm, ybuf.at[sl, 1], sem.at[sl])

    @pl.when(i == 0)
    def _():
        start(0, 0)

    @pl.when(i + 1 < n)
    def _():
        start(i + 1, 1 - slot)

    _wait_rows(p1_ref, i * tm, tm, y_hbm, ybuf.at[slot, 0], sem.at[slot])
    _wait_rows(p2_ref, i * tm, tm, y_hbm, ybuf.at[slot, 1], sem.at[slot])
    r = r_ref[...]
    o_ref[...] = x_ref[...] + r[:, 2:3] * ybuf[slot, 0] + r[:, 3:4] * ybuf[slot, 1]


def _combine(p1, p2, x, r, y, tm):
    n, d = x.shape
    grid_spec = pltpu.PrefetchScalarGridSpec(
        num_scalar_prefetch=2,
        grid=(n // tm,),
        in_specs=[
            pl.BlockSpec((tm, d), lambda i, a, b: (i, 0)),
            pl.BlockSpec((tm, LANE), lambda i, a, b: (i, 0)),
            pl.BlockSpec(memory_space=pl.ANY),
        ],
        out_specs=pl.BlockSpec((tm, d), lambda i, a, b: (i, 0)),
        scratch_shapes=[pltpu.VMEM((2, 2, tm, d), F32), pltpu.SemaphoreType.DMA((2,))],
    )
    return pl.pallas_call(
        _combine_body,
        grid_spec=grid_spec,
        out_shape=jax.ShapeDtypeStruct((n, d), F32),
        compiler_params=_cparams(("arbitrary",)),
    )(p1, p2, x, r, y)


def _moe(x, g, w_grp, b_grp, w_er, b_er, w_gate, w_up, w_down, layer, tm):
    n = x.shape[0]
    r = _router(x, g, w_grp, b_grp, w_er, b_er, tm)
    e1 = r[:, 0].astype(jnp.int32)
    e2 = r[:, 1].astype(jnp.int32)
    tile_e, tile_start, tile_rows, n_used, src, p1, p2 = _dispatch_tables(e1, e2)
    y = _experts(tile_e, tile_start, tile_rows, n_used, src, x, g, w_gate, w_up, w_down, layer)
    return _combine(p1, p2, x, r, y, COMBINE_TM)


def _rmsnorm_body(x_ref, g_ref, o_ref):
    x = x_ref[...]
    o_ref[...] = x * lax.rsqrt(jnp.mean(x * x, axis=-1, keepdims=True) + EPS) * g_ref[...]


def _rmsnorm(x, g, m, row_off, tm):
    d = x.shape[1]
    return pl.pallas_call(
        _rmsnorm_body,
        grid=(m // tm,),
        in_specs=[pl.BlockSpec((tm, d), lambda i: (i + row_off, 0)), pl.BlockSpec((1, d), lambda i: (0, 0))],
        out_specs=pl.BlockSpec((tm, d), lambda i: (i, 0)),
        out_shape=jax.ShapeDtypeStruct((m, d), F32),
        compiler_params=_cparams(("parallel",)),
    )(x, g.reshape(1, d).astype(F32))


def _row_tile(n, cands=(768, 512, 256, 128)):
    for tm in cands:
        if n % tm == 0:
            return tm
    raise ValueError(f"row count {n} has no supported row tile")


def kernel(x_prompt, x_sample, mem_prompt, cache_dif_k, cache_dif_v, cache_fox_k, cache_fox_v, cache_fox_logf, cache_cmp_k, cache_cmp_v, cache_slc_k, cache_slc_v, state_win_k, state_win_v, cache_mem_k, cache_mem_v, page_table, g_mix, w_in_e, b_forget, lam_q1, lam_k1, lam_q2, lam_k2, g_dif_head, w_out_e, w_in_o, cmp_pos_k, cmp_pos_v, w_out_o, g_xattn, g_mem, w_xq, w_xk, w_xv, w_xo, g_ffn, w_grp, b_grp, w_er, b_er, w_gate, w_up, w_down, g_final):
    n_b, seq, d = x_prompt.shape
    n_s, dec, _ = x_sample.shape
    n_mem = mem_prompt.shape[1]
    depth = g_mix.shape[0]
    n_pages = page_table.shape[1]
    past_len = n_pages * LANE
    n_p = n_b * seq
    n_smp = n_s * dec
    n_tok = n_p + n_smp
    hd = HEAD_DIM
    tmj = _row_tile(n_tok)
    tmp = _row_tile(n_p, (1024, 512, 256, 128))
    tms = _row_tile(n_smp, (256, 128))
    assert seq % EP_T == 0 and seq % 1024 == 0 and n_p % tms == 0 and n_pages % PAGES_PER_STEP_2 == 0
    assert cache_dif_k.shape[2] == LANE and state_win_k.shape[2] == WINDOW and n_p % dec == 0

    x = jnp.concatenate([x_prompt.reshape(n_p, d), x_sample.reshape(n_smp, d)], axis=0)
    head_rows = lambda a: a.reshape(a.shape[0], a.shape[1] * KV, hd)
    kv5 = lambda a, lead: a.reshape(*lead, KV, hd)

    mem_k_s = cache_mem_k.reshape(depth, n_s, n_mem * KV, hd)
    mem_v_s = cache_mem_v.reshape(depth, n_s, n_mem * KV, hd)

    mem = mem_prompt.reshape(n_b * n_mem, d)
    mem_k_p, mem_v_p = [], []
    for l in range(depth):
        w_kv = jnp.concatenate([w_xk[l], w_xv[l]], axis=1).astype(BF16)
        mk, mv = _nmm(mem, w_kv, g=g_mem[l], out_dtypes=(), kv_cols=(0, 1),
                      tm=_row_tile(n_b * n_mem, (512, 256, 128)), tn=KV_COLS)
        mem_k_p.append(mk)
        mem_v_p.append(mv)

    even_p, even_s, odd_p, odd_s = [], [], [], []
    for l in range(depth):
        j = l // 2
        if l % 2 == 0:
            w_main = w_in_e[j][:, :E_MAIN].astype(BF16)
            w_fg = jnp.pad(w_in_e[j][:, E_MAIN:], ((0, 0), (0, LANE - F_HEADS))).astype(BF16)
            kvc = (2, 3, 6, 7)
            qkv16, *kv_p = _nmm(x, w_main, g=g_mix[l], out_dtypes=(BF16,), kv_cols=kvc, m=n_p, tm=tmp, tn=KV_COLS)
            qkv32, *kv_s = _nmm(x, w_main, g=g_mix[l], out_dtypes=(F32,), kv_cols=kvc, m=n_smp,
                                row_off=n_p // tms, tm=tms, tn=KV_COLS)
            logf = _nmm(x, w_fg, g=g_mix[l], bias=jnp.pad(b_forget[j], (0, LANE - F_HEADS)), tm=tmj, tn=LANE)
            logf = logf[:, :F_HEADS]
            lf_p = logf[:n_p].reshape(n_b, seq, F_HEADS)
            lf_s = logf[n_p:].reshape(n_s, dec, F_HEADS)
            ck_p = _cumsum(lf_p.transpose(0, 2, 1)).reshape(n_b, F_KV, F_HEADS // F_KV, seq)
            lf_past = cache_fox_logf[j].transpose(0, 2, 1)[page_table]
            lf_past = lf_past.transpose(0, 2, 1, 3).reshape(n_s, F_HEADS, past_len)
            lf_new = jnp.pad(lf_s.transpose(0, 2, 1), ((0, 0), (0, 0), (0, LANE - dec)))
            ck_s = _cumsum(jnp.concatenate([lf_past, lf_new], axis=2))
            lam_p = jnp.stack([lam_q1[j], lam_k1[j], lam_q2[j], lam_k2[j]]).astype(F32)
            lam_init = _lambda_init(l)
            o_p = _even_prompt_attn(qkv16, ck_p, lam_p, g_dif_head[j], n_b, seq, lam_init)
            caches = [head_rows(c[j]) for c in (cache_dif_k, cache_dif_v, cache_fox_k, cache_fox_v)]
            o_s = _even_sample_attn(qkv32, caches, page_table, ck_s, lam_p, g_dif_head[j], lam_init, dec)
            w_out = w_out_e[j][_even_out_perm()].astype(BF16)
            even_p.append([kv5(a, (n_b, seq)) for a in kv_p] + [lf_p])
            even_s.append([kv5(a, (n_s, dec)) for a in kv_s] + [lf_s])
        else:
            w_main = w_in_o[j][:, :O_MAIN].astype(BF16)
            w_g = w_in_o[j][:, O_MAIN:].reshape(d, C_KV, 3 * C_HEADS // C_KV)
            w_g = jnp.pad(w_g, ((0, 0), (0, 0), (0, LANE - w_g.shape[2]))).reshape(d, C_KV * LANE).astype(BF16)
            qkv16, *kv_p = _nmm(x, w_main, g=g_mix[l], out_dtypes=(BF16,), kv_cols=(4, 5, 6, 7, 8, 9),
                                m=n_p, tm=tmp, tn=KV_COLS)
            qkv32, *kv_s = _nmm(x, w_main, g=g_mix[l], out_dtypes=(F32,), kv_cols=(4, 5, 6, 7), m=n_smp,
                                row_off=n_p // tms, tm=tms, tn=KV_COLS)
            gates = _nmm(x, w_g, g=g_mix[l], tm=tmj, tn=KV_COLS)
            wk, wv = _pos_softmax(cmp_pos_k[j], cmp_pos_v[j])
            kc_p, vc_p = _compress_rows(qkv16, wk, wv, n_b, seq, 4, 5)
            kc_s, vc_s = _compress_paged(head_rows(cache_cmp_k[j]), head_rows(cache_cmp_v[j]), page_table, wk, wv)
            o_p = _odd_prompt_attn(qkv16, kc_p, vc_p, gates, n_b, seq)
            o_s, win_k, win_v = _odd_sample_attn(
                qkv32, gates, n_p // dec, kc_s, vc_s, head_rows(state_win_k[j]), head_rows(state_win_v[j]),
                head_rows(cache_slc_k[j]), head_rows(cache_slc_v[j]), page_table, dec)
            w_out = w_out_o[j].astype(BF16)
            n_keep = min(WINDOW, seq)
            rows_p = [kv5(a, (n_b, seq)) for a in kv_p]
            odd_p.append(rows_p[:4] + [a[:, seq - n_keep:] for a in rows_p[4:]])
            odd_s.append([kv5(a, (n_s, dec)) for a in kv_s] + [kv5(win_k, (n_s, WINDOW)), kv5(win_v, (n_s, WINDOW))])
        o = jnp.concatenate([o_p, o_s.astype(BF16)], axis=0)
        x = _nmm(o, w_out, res=x, tm=tmj, tn=d)

        q = _nmm(x, w_xq[l].astype(BF16), g=g_xattn[l], tm=tmj, tn=512)
        ox_p = _xattn(q, mem_k_p[l].reshape(1, n_b, n_mem * KV, hd), mem_v_p[l].reshape(1, n_b, n_mem * KV, hd),
                      0, 0, 512, n_b, seq // 512)
        ox_s = _xattn(q, mem_k_s, mem_v_s, l, n_p // dec, dec, n_s, 1)
        x = _nmm(jnp.concatenate([ox_p, ox_s], axis=0), w_xo[l].astype(BF16), res=x, tm=tmj, tn=d)

        x = _moe(x, g_ffn[l], w_grp[l], b_grp[l], w_er[l], b_er[l], w_gate, w_up, w_down, l, tmj)

    y_prompt = _rmsnorm(x, g_final, n_p, 0, tmp).reshape(n_b, seq, d)
    y_sample = _rmsnorm(x, g_final, n_smp, n_p // tms, tms).reshape(n_s, dec, d)
    stack = lambda rows: [jnp.stack(a) for a in zip(*rows)]
    mem_k = jnp.stack([kv5(a, (n_b, n_mem)) for a in mem_k_p])
    mem_v = jnp.stack([kv5(a, (n_b, n_mem)) for a in mem_v_p])
    return (y_prompt, y_sample, *stack(even_p), *stack(odd_p), mem_k, mem_v, *stack(even_s), *stack(odd_s))
```
